```python
import jax, jax.numpy as jnp
from jax import lax
import numpy as np

D_MODEL = 1024
BATCH = 8
SEQ = 8192
DEPTH = 2

NSA_HEADS = 8
NSA_KV_HEADS = 2
NSA_GROUP = NSA_HEADS // NSA_KV_HEADS
NSA_HEAD_DIM = 64
NSA_Q_W = NSA_HEADS * NSA_HEAD_DIM
NSA_KV_W = NSA_KV_HEADS * NSA_HEAD_DIM
CMP_STRIDE = 16
CMP_LEN = 2 * CMP_STRIDE
CMP_HIDDEN = 128
SEL_BLOCK = 64
N_SEL = 8
WINDOW = 512
Q_BLOCK = 128
SEL_FORCE = 1e4
CONV_CH = 512
CONV_K = 31
RWKV_HEADS = 8
RWKV_HEAD_DIM = 64
RWKV_W = RWKV_HEADS * RWKV_HEAD_DIM
DECAY_LORA = 64
ICLR_LORA = 64
GATE_LORA = 160
VRES_LORA = 32
N_BRANCH = 3
DENSE_FF = 2816
N_EXPERTS = 8
TOP_K = 2
EXPERT_FF = 3584
MOE_BLOCK = 512
ROPE_THETA = 10000.0
RMS_EPS = 1e-6
LN_EPS = 1e-5
GN_EPS = 64e-5
NEG_INF = -1e30

NSA_IN = NSA_Q_W + 6 * NSA_KV_W + 3 * NSA_HEADS
CONV_IN = 2 * CONV_CH
GATE_IN = N_BRANCH * D_MODEL
RWKV_IN_FIRST = 3 * RWKV_W + DECAY_LORA + ICLR_LORA + GATE_LORA
RWKV_IN_REST = RWKV_IN_FIRST + VRES_LORA
OFF_CONV = NSA_IN
OFF_GATE = OFF_CONV + CONV_IN
OFF_RWKV = OFF_GATE + GATE_IN
N_IN_FIRST = OFF_RWKV + RWKV_IN_FIRST
N_IN_REST = OFF_RWKV + RWKV_IN_REST

kernel_name = 'hybrid_nsa_conformer_rwkv7_moe'


def rms_norm(x, g):
    xf = x.astype(jnp.float32)
    y = xf * lax.rsqrt(jnp.mean(xf * xf, axis=-1, keepdims=True) + RMS_EPS)
    return (y * g.astype(jnp.float32)).astype(x.dtype)


def rope(x, pos):
    half = x.shape[-1] // 2
    inv = ROPE_THETA ** (-jnp.arange(half, dtype=jnp.float32) / half)
    ang = pos.astype(jnp.float32)[:, None] * inv[None, :]
    cos, sin = jnp.cos(ang), jnp.sin(ang)
    x1 = x[..., :half].astype(jnp.float32)
    x2 = x[..., half:].astype(jnp.float32)
    return jnp.concatenate([x1 * cos - x2 * sin, x2 * cos + x1 * sin], axis=-1).astype(x.dtype)


def masked_softmax(s, mask):
    return jax.nn.softmax(jnp.where(mask, s, NEG_INF), axis=-1) * mask


def compress_blocks(t, pos_emb, w1, w2):
    B, G, S, DH = t.shape
    ch = t.reshape(B, G, S // CMP_STRIDE, CMP_STRIDE, DH)
    blocks = jnp.concatenate([ch[:, :, :-1], ch[:, :, 1:]], axis=3) + pos_emb
    flat = blocks.reshape(B, G, S // CMP_STRIDE - 1, CMP_LEN * DH)
    return jax.nn.gelu(flat @ w1) @ w2


def nsa_mixer(p, qk_g, cmp_pos, cmp_w1, cmp_w2):
    B, S, _ = p.shape
    G, HG, DH = NSA_KV_HEADS, NSA_GROUP, NSA_HEAD_DIM
    pos = jnp.arange(S)

    def heads(t, n):
        return t.reshape(B, S, n, DH).transpose(0, 2, 1, 3)

    q = heads(p[..., :NSA_Q_W], NSA_HEADS)
    kc, vc, ks, vs, kw, vw = [heads(p[..., NSA_Q_W + i * NSA_KV_W:NSA_Q_W + (i + 1) * NSA_KV_W], G)
                              for i in range(6)]
    g_off = NSA_Q_W + 6 * NSA_KV_W
    gates = jax.nn.sigmoid(p[..., g_off:g_off + 3 * NSA_HEADS].astype(jnp.float32))
    gates = gates.reshape(B, S, G, HG, 3).transpose(0, 2, 3, 1, 4)

    q = rope(rms_norm(q, qk_g[0]), pos).reshape(B, G, HG, S, DH)
    nc = S // CMP_STRIDE - 1
    ns = S // SEL_BLOCK
    c_start = jnp.arange(nc) * CMP_STRIDE
    cmp_end = c_start + CMP_LEN - 1
    k_cmp = rope(rms_norm(compress_blocks(kc, cmp_pos[0], cmp_w1[0], cmp_w2[0]), qk_g[1]), cmp_end)
    v_cmp = compress_blocks(vc, cmp_pos[1], cmp_w1[1], cmp_w2[1])
    ks = rope(rms_norm(ks, qk_g[2]), pos)
    kw = rope(rms_norm(kw, qk_g[3]), pos)
    ks_blk = ks.reshape(B, G, ns, SEL_BLOCK, DH)
    vs_blk = vs.reshape(B, G, ns, SEL_BLOCK, DH)
    pad = ((0, 0), (0, 0), (WINDOW, 0), (0, 0))
    kw_pad = jnp.pad(kw, pad)
    vw_pad = jnp.pad(vw, pad)

    s_start = jnp.arange(ns) * SEL_BLOCK
    overlap = ((c_start[:, None] < s_start[None, :] + SEL_BLOCK)
               & (c_start[:, None] + CMP_LEN > s_start[None, :])).astype(jnp.float32)
    n_sel = min(N_SEL, ns)
    scale = DH ** -0.5
    b_ix = jnp.arange(B)[:, None, None, None]
    g_ix = jnp.arange(G)[None, :, None, None]
    blk_ids = jnp.arange(ns)

    def query_block(qb):
        t0 = qb * Q_BLOCK
        t = t0 + jnp.arange(Q_BLOCK)
        q_b = lax.dynamic_slice_in_dim(q, t0, Q_BLOCK, axis=3)
        g_b = lax.dynamic_slice_in_dim(gates, t0, Q_BLOCK, axis=3)
        s_c = jnp.einsum('bghqd,bgcd->bghqc', q_b, k_cmp, preferred_element_type=jnp.float32) * scale
        p_c = masked_softmax(s_c, cmp_end[None, :] <= t[:, None])
        o_c = jnp.einsum('bghqc,bgcd->bghqd', p_c, v_cmp)
        imp = jnp.einsum('bghqc,cj->bgqj', p_c, overlap)
        cur = (t // SEL_BLOCK)[:, None]
        j = blk_ids[None, :]
        forced = (j == 0) | (j == cur) | (j == cur - 1)
        score = jnp.where(forced, SEL_FORCE, jnp.where(j <= cur, imp, -1.0))
        _, idx = lax.top_k(score, n_sel)
        k_sel = ks_blk[b_ix, g_ix, idx]
        v_sel = vs_blk[b_ix, g_ix, idx]
        s_s = jnp.einsum('bghqd,bgqnkd->bghqnk', q_b, k_sel, preferred_element_type=jnp.float32) * scale
        tok = idx[..., None] * SEL_BLOCK + jnp.arange(SEL_BLOCK)
        m_s = (tok <= t[None, None, :, None, None])[:, :, None]
        p_s = masked_softmax(s_s.reshape(B, G, HG, Q_BLOCK, n_sel * SEL_BLOCK),
                             m_s.reshape(B, G, 1, Q_BLOCK, n_sel * SEL_BLOCK))
        o_s = jnp.einsum('bghqm,bgqmd->bghqd', p_s, v_sel.reshape(B, G, Q_BLOCK, n_sel * SEL_BLOCK, DH))
        k_w = lax.dynamic_slice_in_dim(kw_pad, t0, WINDOW + Q_BLOCK, axis=2)
        v_w = lax.dynamic_slice_in_dim(vw_pad, t0, WINDOW + Q_BLOCK, axis=2)
        kpos = t0 - WINDOW + jnp.arange(WINDOW + Q_BLOCK)
        m_w = ((kpos[None, :] <= t[:, None]) & (kpos[None, :] > t[:, None] - WINDOW)
               & (kpos[None, :] >= 0))
        s_w = jnp.einsum('bghqd,bgkd->bghqk', q_b, k_w, preferred_element_type=jnp.float32) * scale
        p_w = masked_softmax(s_w, m_w)
        o_w = jnp.einsum('bghqk,bgkd->bghqd', p_w, v_w)
        out = g_b[..., 0:1] * o_c + g_b[..., 1:2] * o_s + g_b[..., 2:3] * o_w
        return out.astype(p.dtype)

    outs = lax.map(query_block, jnp.arange(S // Q_BLOCK))
    return outs.transpose(1, 0, 4, 2, 3, 5).reshape(B, S, NSA_Q_W)


def conformer_conv(p, conv_w, conv_vec, w_o):
    u = p[..., :CONV_CH] * jax.nn.sigmoid(p[..., CONV_CH:])
    u = lax.conv_general_dilated(u, conv_w[:, None, :].astype(u.dtype), (1,), [(CONV_K - 1, 0)],
                                 dimension_numbers=('NWC', 'WIO', 'NWC'),
                                 feature_group_count=CONV_CH)
    uf = u.astype(jnp.float32) + conv_vec[0]
    mu = jnp.mean(uf, axis=-1, keepdims=True)
    var = jnp.mean(jnp.square(uf - mu), axis=-1, keepdims=True)
    uf = (uf - mu) * lax.rsqrt(var + LN_EPS) * conv_vec[1] + conv_vec[2]
    uf = uf * jax.nn.sigmoid(uf)
    return uf.astype(p.dtype) @ w_o


def wkv7_scan(r, w, k, v, kk, a):
    B, S, H, N = r.shape

    def step(state, inp):
        r_t, w_t, k_t, v_t, kk_t, a_t = inp
        s_kk = jnp.einsum('bhvk,bhk->bhv', state, kk_t)
        state = (state * w_t[:, :, None, :]
                 - s_kk[..., None] * (kk_t * a_t)[:, :, None, :]
                 + v_t[..., None] * k_t[:, :, None, :])
        return state, jnp.einsum('bhvk,bhk->bhv', state, r_t)

    xs = tuple(jnp.swapaxes(z, 0, 1) for z in (r, w, k, v, kk, a))
    _, ys = lax.scan(step, jnp.zeros((B, H, N, N), jnp.float32), xs)
    return jnp.swapaxes(ys, 0, 1)


def rwkv7_mixer(p, mu, vec, w_up, a_up, g_up, w_o, v_first, vres_up, vres_b):
    B, S, _ = p.shape
    C, H, N = RWKV_W, RWKV_HEADS, RWKV_HEAD_DIM
    prev = jnp.concatenate([jnp.zeros_like(p[:, :1]), p[:, :-1]], axis=1)
    u = p + (prev - p) * mu
    r, k, v = u[..., :C], u[..., C:2 * C], u[..., 2 * C:3 * C]
    o = 3 * C
    wd = u[..., o:o + DECAY_LORA]; o += DECAY_LORA
    ad = u[..., o:o + ICLR_LORA]; o += ICLR_LORA
    gd = u[..., o:o + GATE_LORA]; o += GATE_LORA
    w_log = -jax.nn.softplus(-(vec[0] + jnp.tanh(wd) @ w_up).astype(jnp.float32)) - 0.5
    decay = jnp.exp(-jnp.exp(w_log))
    a = jax.nn.sigmoid((vec[1] + ad @ a_up).astype(jnp.float32))
    g = jax.nn.sigmoid(gd) @ g_up
    if v_first is not None:
        vd = u[..., o:o + VRES_LORA]
        v = v + (v_first - v) * jax.nn.sigmoid(vres_b + vd @ vres_up)
    hd = lambda z: z.reshape(B, S, H, N)
    kk = hd(k * vec[2]).astype(jnp.float32)
    kk = kk / jnp.maximum(jnp.sqrt(jnp.sum(kk * kk, axis=-1, keepdims=True)), 1e-12)
    k = k * (1.0 + (a - 1.0) * vec[3])
    rf, kf, vf = hd(r).astype(jnp.float32), hd(k).astype(jnp.float32), hd(v).astype(jnp.float32)
    y = wkv7_scan(rf, hd(decay), kf, vf, kk, hd(a))
    ym = jnp.mean(y, axis=-1, keepdims=True)
    yv = jnp.mean(jnp.square(y - ym), axis=-1, keepdims=True)
    y = ((y - ym) * lax.rsqrt(yv + GN_EPS)).reshape(B, S, C) * vec[5] + vec[6]
    r_k = vec[4].reshape(H, N).astype(jnp.float32)
    bonus = (jnp.sum(rf * kf * r_k, axis=-1, keepdims=True) * vf).reshape(B, S, C)
    y = (y + bonus) * g
    return y.astype(p.dtype) @ w_o, v


def swiglu(h, w_gu, w_down):
    gate, up = jnp.split(h @ w_gu, 2, axis=-1)
    return (jax.nn.silu(gate) * up) @ w_down


def moe_swiglu(h, router_w, router_b, w_gu, w_down):
    B, S, D = h.shape
    T = B * S
    ht = h.reshape(T, D)
    logits = (ht @ router_w).astype(jnp.float32) + router_b
    top_logit, top_e = lax.top_k(logits, TOP_K)
    top_w = jax.nn.softmax(top_logit, axis=-1)
    A = T * TOP_K
    flat_e = top_e.reshape(A)
    flat_tok = jnp.repeat(jnp.arange(T, dtype=jnp.int32), TOP_K)
    flat_w = top_w.reshape(A)
    order = jnp.argsort(flat_e)
    sorted_e = flat_e[order]
    counts = jnp.bincount(flat_e, length=N_EXPERTS)
    padded = (counts + MOE_BLOCK - 1) // MOE_BLOCK * MOE_BLOCK
    pad_end = jnp.cumsum(padded)
    pad_start = pad_end - padded
    start = jnp.cumsum(counts) - counts
    dest = pad_start[sorted_e] + jnp.arange(A) - start[sorted_e]
    n_blocks = -(-A // MOE_BLOCK) + N_EXPERTS
    rows = n_blocks * MOE_BLOCK
    row_tok = jnp.full((rows,), T, jnp.int32).at[dest].set(flat_tok[order])
    row_w = jnp.zeros((rows,), jnp.float32).at[dest].set(flat_w[order])
    block_e = jnp.minimum(jnp.searchsorted(pad_end, jnp.arange(n_blocks) * MOE_BLOCK, side='right'),
                          N_EXPERTS - 1)
    h_pad = jnp.concatenate([ht, jnp.zeros((1, D), ht.dtype)], axis=0)
    xs = h_pad[row_tok].reshape(n_blocks, MOE_BLOCK, D)

    def expert_block(args):
        xb, e = args
        gate, up = jnp.split(xb @ w_gu[e], 2, axis=-1)
        return (jax.nn.silu(gate) * up) @ w_down[e]

    ys = lax.map(expert_block, (xs, block_e)).reshape(rows, D)
    ys = ys * row_w[:, None].astype(ys.dtype)
    out = jax.ops.segment_sum(ys, row_tok, num_segments=T + 1)[:T]
    return out.reshape(B, S, D)


def setup_inputs(seed: int = 0) -> dict:
    key = jax.random.key(seed)
    keys = iter(jax.random.split(key, 48))
    L = DEPTH
    nd = (DEPTH + 1) // 2
    nm = DEPTH // 2
    f32 = jnp.float32

    def nrm(shape, scale):
        return jax.random.normal(next(keys), shape, f32) * scale

    def gain(shape):
        return 1.0 + nrm(shape, 0.02)

    def unif(shape, lo, hi):
        return jax.random.uniform(next(keys), shape, f32, minval=lo, maxval=hi)

    C = RWKV_W
    return {
        'x': nrm((BATCH, SEQ, D_MODEL), 1.0),
        'mix_norm_g': gain((L, D_MODEL)),
        'w_in_first': nrm((D_MODEL, N_IN_FIRST), D_MODEL ** -0.5),
        'w_in_rest': nrm((L - 1, D_MODEL, N_IN_REST), D_MODEL ** -0.5),
        'rwkv_mu_first': unif((RWKV_IN_FIRST,), 0.0, 1.0),
        'rwkv_mu_rest': unif((L - 1, RWKV_IN_REST), 0.0, 1.0),
        'nsa_qk_g': gain((L, 4, NSA_HEAD_DIM)),
        'nsa_cmp_pos': nrm((L, 2, CMP_LEN, NSA_HEAD_DIM), 0.1),
        'nsa_cmp_w1': nrm((L, 2, CMP_LEN * NSA_HEAD_DIM, CMP_HIDDEN), (CMP_LEN * NSA_HEAD_DIM) ** -0.5),
        'nsa_cmp_w2': nrm((L, 2, CMP_HIDDEN, NSA_HEAD_DIM), CMP_HIDDEN ** -0.5),
        'nsa_w_o': nrm((L, NSA_Q_W, D_MODEL), NSA_Q_W ** -0.5),
        'conv_w': nrm((L, CONV_K, CONV_CH), CONV_K ** -0.5),
        'conv_vec': jnp.stack([nrm((L, CONV_CH), 0.02), gain((L, CONV_CH)), nrm((L, CONV_CH), 0.02)], axis=1),
        'conv_w_o': nrm((L, CONV_CH, D_MODEL), CONV_CH ** -0.5),
        'rwkv_vec': jnp.stack([
            unif((L, C), -6.0, 1.0),
            nrm((L, C), 0.5),
            1.0 + nrm((L, C), 0.1),
            1.0 + nrm((L, C), 0.1),
            nrm((L, C), 0.1),
            gain((L, C)),
            nrm((L, C), 0.02),
        ], axis=1),
        'rwkv_w_up': nrm((L, DECAY_LORA, C), 0.1),
        'rwkv_a_up': nrm((L, ICLR_LORA, C), ICLR_LORA ** -0.5),
        'rwkv_g_up': nrm((L, GATE_LORA, C), GATE_LORA ** -0.5),
        'rwkv_vres_up': nrm((L - 1, VRES_LORA, C), VRES_LORA ** -0.5),
        'rwkv_vres_b': nrm((L - 1, C), 0.1),
        'rwkv_w_o': nrm((L, C, D_MODEL), C ** -0.5),
        'w_out': nrm((L, D_MODEL, D_MODEL), D_MODEL ** -0.5),
        'ffn_norm_g': gain((L, D_MODEL)),
        'dense_w_gu': nrm((nd, D_MODEL, 2 * DENSE_FF), D_MODEL ** -0.5),
        'dense_w_down': nrm((nd, DENSE_FF, D_MODEL), DENSE_FF ** -0.5),
        'router_w': nrm((nm, D_MODEL, N_EXPERTS), D_MODEL ** -0.5),
        'router_b': nrm((nm, N_EXPERTS), 0.01),
        'moe_w_gu': nrm((nm, N_EXPERTS, D_MODEL, 2 * EXPERT_FF), D_MODEL ** -0.5),
        'moe_w_down': nrm((nm, N_EXPERTS, EXPERT_FF, D_MODEL), EXPERT_FF ** -0.5),
    }


def reference(x, mix_norm_g, w_in_first, w_in_rest, rwkv_mu_first, rwkv_mu_rest, nsa_qk_g, nsa_cmp_pos,
              nsa_cmp_w1, nsa_cmp_w2, nsa_w_o, conv_w, conv_vec, conv_w_o, rwkv_vec, rwkv_w_up, rwkv_a_up,
              rwkv_g_up, rwkv_vres_up, rwkv_vres_b, rwkv_w_o, w_out, ffn_norm_g, dense_w_gu, dense_w_down,
              router_w, router_b, moe_w_gu, moe_w_down):
    B, S, D = x.shape
    v_first = None
    for layer in range(DEPTH):
        h = rms_norm(x, mix_norm_g[layer])
        if layer == 0:
            w_in, mu, vres_up, vres_b = w_in_first, rwkv_mu_first, None, None
        else:
            w_in, mu = w_in_rest[layer - 1], rwkv_mu_rest[layer - 1]
            vres_up, vres_b = rwkv_vres_up[layer - 1], rwkv_vres_b[layer - 1]
        proj = h @ w_in
        o_nsa = nsa_mixer(proj[..., :OFF_CONV], nsa_qk_g[layer], nsa_cmp_pos[layer], nsa_cmp_w1[layer],
                          nsa_cmp_w2[layer]) @ nsa_w_o[layer]
        o_conv = conformer_conv(proj[..., OFF_CONV:OFF_GATE], conv_w[layer], conv_vec[layer], conv_w_o[layer])
        o_rwkv, v_layer = rwkv7_mixer(proj[..., OFF_RWKV:], mu, rwkv_vec[layer], rwkv_w_up[layer],
                                      rwkv_a_up[layer], rwkv_g_up[layer], rwkv_w_o[layer], v_first,
                                      vres_up, vres_b)
        if layer == 0:
            v_first = v_layer
        gates = jax.nn.sigmoid(proj[..., OFF_GATE:OFF_RWKV].astype(jnp.float32)).reshape(B, S, N_BRANCH, D)
        mixed = gates[:, :, 0] * o_nsa + gates[:, :, 1] * o_conv + gates[:, :, 2] * o_rwkv
        x = x + mixed.astype(x.dtype) @ w_out[layer]
        h2 = rms_norm(x, ffn_norm_g[layer])
        if layer % 2 == 0:
            f = swiglu(h2, dense_w_gu[layer // 2], dense_w_down[layer // 2])
        else:
            f = moe_swiglu(h2, router_w[layer // 2], router_b[layer // 2], moe_w_gu[layer // 2],
                           moe_w_down[layer // 2])
        x = x + f.astype(x.dtype)
    return x
```

```python
import functools

import jax
import jax.numpy as jnp
from jax import lax
from jax.experimental import pallas as pl
from jax.experimental.pallas import tpu as pltpu

F32 = jnp.float32
BF16 = jnp.bfloat16
HI = lax.Precision.HIGHEST

NSA_HEADS = 8
NSA_KV_HEADS = 2
NSA_GROUP = NSA_HEADS // NSA_KV_HEADS
HEAD_DIM = 64
NSA_Q_W = NSA_HEADS * HEAD_DIM
NSA_KV_W = NSA_KV_HEADS * HEAD_DIM
CMP_STRIDE = 16
CMP_LEN = 32
SEL_BLOCK = 64
N_SEL = 8
WINDOW = 512
SEL_FORCE = 1e4
CONV_CH = 512
CONV_K = 31
RWKV_HEADS = 8
RWKV_W = RWKV_HEADS * HEAD_DIM
DECAY_LORA = 64
ICLR_LORA = 64
GATE_LORA = 160
VRES_LORA = 32
N_EXPERTS = 8
ROPE_THETA = 10000.0
RMS_EPS = 1e-6
LN_EPS = 1e-5
GN_EPS = 64e-5
NEG_INF = -1e30

NSA_IN = NSA_Q_W + 6 * NSA_KV_W + 3 * NSA_HEADS
NSA_IN_PAD = 1408
NSA_GATE_BLK = (NSA_Q_W + 6 * NSA_KV_W) // 128
RWKV_IN_PAD = 1920
LANE = 128
VMEM_LIMIT = 56 * 1024 * 1024

CHUNK = 64


def _dot(a, b, precision=None):
    return jnp.dot(a, b, preferred_element_type=F32, precision=precision)


def _dot_nt(a, b, precision=None):
    return lax.dot_general(a, b, (((1,), (1,)), ((), ())), preferred_element_type=F32,
                           precision=precision)


def _dot_tn(a, b, precision=None):
    return lax.dot_general(a, b, (((0,), (0,)), ((), ())), preferred_element_type=F32,
                           precision=precision)


def _sigmoid(x):
    return 1.0 / (1.0 + jnp.exp(-x))


def _params(*sem):
    return pltpu.CompilerParams(dimension_semantics=sem, vmem_limit_bytes=VMEM_LIMIT)


def _rms_mm_kernel(x_ref, g_ref, w_ref, o_ref, h_scr):
    @pl.when(pl.program_id(1) == 0)
    def _():
        x = x_ref[...]
        ms = jnp.mean(x * x, axis=-1, keepdims=True)
        h_scr[...] = (x * lax.rsqrt(ms + RMS_EPS) * g_ref[...]).astype(BF16)

    o_ref[...] = _dot(h_scr[...], w_ref[...])


def _rms_mm(x, g, w, tm, tn):
    T, D = x.shape
    N = w.shape[1]
    return pl.pallas_call(
        _rms_mm_kernel,
        grid=(T // tm, N // tn),
        in_specs=[pl.BlockSpec((tm, D), lambda i, j: (i, 0)),
                  pl.BlockSpec((1, D), lambda i, j: (0, 0)),
                  pl.BlockSpec((D, tn), lambda i, j: (0, j))],
        out_specs=pl.BlockSpec((tm, tn), lambda i, j: (i, j)),
        out_shape=jax.ShapeDtypeStruct((T, N), F32),
        scratch_shapes=[pltpu.VMEM((tm, D), BF16)],
        compiler_params=_params("parallel", "arbitrary"),
        name="rms_proj",
    )(x, g.reshape(1, D), w)


def _rope_dense(x, cos, sin_signed):
    w = x.shape[1]
    lane = lax.broadcasted_iota(jnp.int32, x.shape, 1)
    first_half = (lane % HEAD_DIM) < (HEAD_DIM // 2)
    rot = jnp.where(first_half, pltpu.roll(x, w - HEAD_DIM // 2, 1), pltpu.roll(x, HEAD_DIM // 2, 1))
    return x * cos + rot * sin_signed


def _head_rms(x, bd, g):
    ms = _dot(x * x, bd, precision=HI)
    return x * lax.rsqrt(ms + RMS_EPS) * g


def _nsa_prep_kernel(p_ref, cos_ref, sin_ref, gq_ref, gks_ref, gkw_ref, bd_ref,
                     q_ref, kc_ref, vc_ref, ks_ref, vs_ref, kw_ref, vw_ref):
    cos = cos_ref[...]
    sin = sin_ref[...]
    bd = bd_ref[...]
    q = p_ref[0, :, 0:NSA_Q_W]
    q = _rope_dense(_head_rms(q, bd, gq_ref[...]), cos, sin) * (HEAD_DIM ** -0.5)
    for h in range(NSA_HEADS):
        q_ref[0, h] = q[:, h * HEAD_DIM:(h + 1) * HEAD_DIM].astype(BF16)

    def slab(i):
        return p_ref[0, :, NSA_Q_W + i * NSA_KV_W:NSA_Q_W + (i + 1) * NSA_KV_W]

    cos_k = cos[:, :NSA_KV_W]
    sin_k = sin[:, :NSA_KV_W]
    bd_k = bd[:NSA_KV_W, :NSA_KV_W]
    ks = _rope_dense(_head_rms(slab(2), bd_k, gks_ref[...]), cos_k, sin_k)
    kw = _rope_dense(_head_rms(slab(4), bd_k, gkw_ref[...]), cos_k, sin_k)
    for val, ref in ((slab(0), kc_ref), (slab(1), vc_ref), (ks, ks_ref), (slab(3), vs_ref),
                     (kw, kw_ref), (slab(5), vw_ref)):
        for g in range(NSA_KV_HEADS):
            ref[0, g] = val[:, g * HEAD_DIM:(g + 1) * HEAD_DIM].astype(BF16)


def _nsa_prep(p, cos, sin, qk_g, ts):
    B, S, _ = p.shape
    bd = jnp.kron(jnp.eye(NSA_HEADS, dtype=F32), jnp.full((HEAD_DIM, HEAD_DIM), 1.0 / HEAD_DIM, F32))
    gq = jnp.tile(qk_g[0], NSA_HEADS).reshape(1, NSA_Q_W)
    gks = jnp.tile(qk_g[2], NSA_KV_HEADS).reshape(1, NSA_KV_W)
    gkw = jnp.tile(qk_g[3], NSA_KV_HEADS).reshape(1, NSA_KV_W)
    kv_shape = jax.ShapeDtypeStruct((B, NSA_KV_HEADS, S, HEAD_DIM), BF16)
    kv_spec = pl.BlockSpec((1, NSA_KV_HEADS, ts, HEAD_DIM), lambda b, i: (b, 0, i, 0))
    const = lambda shape: pl.BlockSpec(shape, lambda b, i: (0,) * len(shape))
    return pl.pallas_call(
        _nsa_prep_kernel,
        grid=(B, S // ts),
        in_specs=[pl.BlockSpec((1, ts, NSA_IN_PAD), lambda b, i: (b, i, 0)),
                  pl.BlockSpec((ts, NSA_Q_W), lambda b, i: (i, 0)),
                  pl.BlockSpec((ts, NSA_Q_W), lambda b, i: (i, 0)),
                  const((1, NSA_Q_W)), const((1, NSA_KV_W)), const((1, NSA_KV_W)),
                  const((NSA_Q_W, NSA_Q_W))],
        out_specs=[pl.BlockSpec((1, NSA_HEADS, ts, HEAD_DIM), lambda b, i: (b, 0, i, 0))] + [kv_spec] * 6,
        out_shape=[jax.ShapeDtypeStruct((B, NSA_HEADS, S, HEAD_DIM), BF16)] + [kv_shape] * 6,
        compiler_params=_params("parallel", "parallel"),
        name="nsa_prep",
    )(p, cos, sin, gq, gks, gkw, bd)


def _gelu_tanh(x):
    return 0.5 * x * (1.0 + jnp.tanh(0.7978845608028654 * (x + 0.044715 * x * x * x)))


def _compress_kernel(a_ref, posa_ref, posb_ref, w1a_ref, w1b_ref, w2_ref, g_ref, cos_ref, sin_ref,
                     perm_ref, o_ref, *, is_key):
    a = a_ref[0]
    ncp = a.shape[0]
    w1a = w1a_ref[...]
    w1b = w1b_ref[...]
    p1 = _dot(a, w1a)
    p2 = _dot(a, w1b)
    hp = _dot(posa_ref[...], w1a) + _dot(posb_ref[...], w1b)
    h = _gelu_tanh(p1 + pltpu.roll(p2, ncp - 1, 0) + hp[0:1])
    o = _dot(h.astype(BF16), w2_ref[...])
    if is_key:
        ms = jnp.mean(o * o, axis=-1, keepdims=True)
        o = o * lax.rsqrt(ms + RMS_EPS) * g_ref[...]
        o = o * cos_ref[...] + _dot(o, perm_ref[...], precision=HI) * sin_ref[...]
    row = lax.broadcasted_iota(jnp.int32, o.shape, 0)
    o_ref[0] = jnp.where(row < ncp - 1, o, 0.0).astype(BF16)


def _compress(t, pos_emb, w1, w2, g, cos_c, sin_c, is_key):
    BG, ncp, cw = t.shape
    hid = w1.shape[1]
    pos = pos_emb.reshape(1, CMP_LEN * HEAD_DIM)
    posa = jnp.tile(pos[:, :cw], (8, 1)).astype(BF16)
    posb = jnp.tile(pos[:, cw:], (8, 1)).astype(BF16)
    half = HEAD_DIM // 2
    perm = jnp.roll(jnp.eye(HEAD_DIM, dtype=F32), half, axis=0)
    const = lambda shape: pl.BlockSpec(shape, lambda b: (0,) * len(shape))
    return pl.pallas_call(
        functools.partial(_compress_kernel, is_key=is_key),
        grid=(BG,),
        in_specs=[pl.BlockSpec((1, ncp, cw), lambda b: (b, 0, 0)),
                  const((8, cw)), const((8, cw)), const((cw, hid)), const((cw, hid)),
                  const((hid, HEAD_DIM)), const((1, HEAD_DIM)),
                  const((ncp, HEAD_DIM)), const((ncp, HEAD_DIM)), const((HEAD_DIM, HEAD_DIM))],
        out_specs=pl.BlockSpec((1, ncp, HEAD_DIM), lambda b: (b, 0, 0)),
        out_shape=jax.ShapeDtypeStruct((BG, ncp, HEAD_DIM), BF16),
        compiler_params=_params("parallel"),
        name="nsa_compress_k" if is_key else "nsa_compress_v",
    )(t, posa, posb, w1[:cw].astype(BF16), w1[cw:].astype(BF16), w2.astype(BF16),
      g.reshape(1, HEAD_DIM), cos_c, sin_c, perm)


def _softmax_rows(s, valid):
    s = jnp.where(valid, s, NEG_INF)
    m = jnp.max(s, axis=-1, keepdims=True)
    e = jnp.where(valid, jnp.exp(s - m), 0.0)
    l = jnp.sum(e, axis=-1, keepdims=True)
    return e / jnp.maximum(l, 1e-30)


def _heads_to_lanes(o, tq):
    return jnp.concatenate([o[h * tq:(h + 1) * tq] for h in range(NSA_GROUP)], axis=1)


def _nsa_attn_kernel(q_ref, kc_ref, vc_ref, ks_ref, vs_ref, kw_ref, vw_ref, gl_ref, o_ref,
                     *, tq, tk, seq):
    g = pl.program_id(1)
    i = pl.program_id(2)
    rows = NSA_GROUP * tq
    ncp = seq // CMP_STRIDE
    ns = seq // SEL_BLOCK
    n_sel = min(N_SEL, ns)
    q = q_ref[0].reshape(rows, HEAD_DIM)
    t0 = i * tq
    t_q = t0 + lax.broadcasted_iota(jnp.int32, (tq, 1), 0)
    t_r = jnp.concatenate([t_q] * NSA_GROUP, axis=0)

    s = _dot_nt(q, kc_ref[0])
    c_end = lax.broadcasted_iota(jnp.int32, (1, ncp), 1) * CMP_STRIDE + (CMP_LEN - 1)
    p_c = _softmax_rows(s, c_end <= t_r)
    o_c = _dot(p_c.astype(BF16), vc_ref[0])

    p_sum = p_c[0:tq]
    for h in range(1, NSA_GROUP):
        p_sum = p_sum + p_c[h * tq:(h + 1) * tq]
    c0 = lax.broadcasted_iota(jnp.int32, (ncp, ns), 0) * CMP_STRIDE
    s0 = lax.broadcasted_iota(jnp.int32, (ncp, ns), 1) * SEL_BLOCK
    overlap = jnp.where((c0 < s0 + SEL_BLOCK) & (c0 + CMP_LEN > s0), 1.0, 0.0)
    imp = _dot(p_sum, overlap, precision=HI)
    j = lax.broadcasted_iota(jnp.int32, (tq, ns), 1)
    cur = t_q // SEL_BLOCK
    forced = (j == 0) | (j == cur) | (j == cur - 1)
    score = jnp.where(forced, SEL_FORCE, jnp.where(j <= cur, imp, -1.0))
    sel = jnp.zeros((tq, ns), F32)
    for _ in range(n_sel):
        mx = jnp.max(score, axis=-1, keepdims=True)
        first = jnp.min(jnp.where(score == mx, j, ns), axis=-1, keepdims=True)
        pick = j == first
        sel = jnp.where(pick, 1.0, sel)
        score = jnp.where(pick, -jnp.inf, score)
    sel = sel.astype(BF16)

    wlen = WINDOW + tq
    w0 = pl.multiple_of(jnp.maximum(t0 - WINDOW, 0), tq)
    s = _dot_nt(q, kw_ref[0, 0, pl.ds(w0, wlen), :])
    kpos = w0 + lax.broadcasted_iota(jnp.int32, (1, wlen), 1)
    p_w = _softmax_rows(s, (kpos <= t_r) & (kpos > t_r - WINDOW))
    o_w = _dot(p_w.astype(BF16), vw_ref[0, 0, pl.ds(w0, wlen), :])

    blk_per_tile = tk // SEL_BLOCK

    def sel_tile(jt, carry):
        m_run, l_run, acc = carry
        k0 = pl.multiple_of(jt * tk, tk)
        s = _dot_nt(q, ks_ref[0, 0, pl.ds(k0, tk), :])
        blk = lax.broadcasted_iota(jnp.int32, (ns, tk), 0)
        key = lax.broadcasted_iota(jnp.int32, (ns, tk), 1)
        expand = jnp.where(blk == jt * blk_per_tile + key // SEL_BLOCK, 1.0, 0.0).astype(BF16)
        hit = _dot(sel, expand)
        kpos = k0 + lax.broadcasted_iota(jnp.int32, (1, tk), 1)
        valid_q = (hit > 0.5) & (kpos <= t_q)
        valid = jnp.concatenate([valid_q] * NSA_GROUP, axis=0)
        s = jnp.where(valid, s, NEG_INF)
        m_new = jnp.maximum(m_run, jnp.max(s, axis=-1, keepdims=True))
        e = jnp.where(valid, jnp.exp(s - m_new), 0.0)
        alpha = jnp.exp(m_run - m_new)
        l_new = alpha * l_run + jnp.sum(e, axis=-1, keepdims=True)
        acc = alpha * acc + _dot(e.astype(BF16), vs_ref[0, 0, pl.ds(k0, tk), :])
        return m_new, l_new, acc

    n_tiles = (t0 + tq - 1) // tk + 1
    init = (jnp.full((rows, 1), NEG_INF, F32), jnp.zeros((rows, 1), F32), jnp.zeros((rows, HEAD_DIM), F32))
    _, l_s, acc_s = lax.fori_loop(0, n_tiles, sel_tile, init)
    o_s = acc_s / jnp.maximum(l_s, 1e-30)

    sig = _sigmoid(gl_ref[0])
    gw = NSA_GROUP * HEAD_DIM
    col = lax.broadcasted_iota(jnp.int32, (LANE, gw), 0)
    head = g * NSA_GROUP + lax.broadcasted_iota(jnp.int32, (LANE, gw), 1) // HEAD_DIM
    out = jnp.zeros((tq, gw), F32)
    for br, o in enumerate((o_c, o_s, o_w)):
        spread = jnp.where(col == head * 3 + br, 1.0, 0.0)
        out = out + _dot(sig, spread, precision=HI) * _heads_to_lanes(o, tq)
    o_ref[0] = out


def _nsa_attn(q, kc, vc, ks, vs, kw, vw, p_nsa, tq, tk):
    B, H, S, _ = q.shape
    G = NSA_KV_HEADS
    ncp = S // CMP_STRIDE
    cmp_spec = pl.BlockSpec((1, ncp, HEAD_DIM), lambda b, g, i: (b * G + g, 0, 0))
    seq_spec = pl.BlockSpec((1, 1, S, HEAD_DIM), lambda b, g, i: (b, g, 0, 0))
    return pl.pallas_call(
        functools.partial(_nsa_attn_kernel, tq=tq, tk=tk, seq=S),
        grid=(B, G, S // tq),
        in_specs=[pl.BlockSpec((1, NSA_GROUP, tq, HEAD_DIM), lambda b, g, i: (b, g, i, 0)),
                  cmp_spec, cmp_spec, seq_spec, seq_spec, seq_spec, seq_spec,
                  pl.BlockSpec((1, tq, LANE), lambda b, g, i: (b, i, NSA_GATE_BLK))],
        out_specs=pl.BlockSpec((1, tq, NSA_GROUP * HEAD_DIM), lambda b, g, i: (b, i, g)),
        out_shape=jax.ShapeDtypeStruct((B, S, NSA_Q_W), F32),
        compiler_params=_params("parallel", "parallel", "arbitrary"),
        name="nsa_attn",
    )(q, kc, vc, ks, vs, kw, vw, p_nsa)


def _rope_tables(S):
    half = HEAD_DIM // 2
    inv = ROPE_THETA ** (-jnp.arange(half, dtype=F32) / half)
    ang = jnp.arange(S, dtype=F32)[:, None] * inv[None, :]
    cos, sin = jnp.cos(ang), jnp.sin(ang)
    cos64 = jnp.concatenate([cos, cos], axis=1)
    sin64 = jnp.concatenate([-sin, sin], axis=1)
    return cos64, sin64


def _nsa_branch(p_nsa, qk_g, cmp_pos, cmp_w1, cmp_w2, B, S):
    cos64, sin64 = _rope_tables(S)
    cos = jnp.tile(cos64, (1, NSA_HEADS))
    sin = jnp.tile(sin64, (1, NSA_HEADS))
    q, kc, vc, ks, vs, kw, vw = _nsa_prep(p_nsa, cos, sin, qk_g, ts=min(256, S))
    ncp = S // CMP_STRIDE
    cmp_end = jnp.minimum(jnp.arange(ncp) * CMP_STRIDE + CMP_LEN - 1, S - 1)
    cw = CMP_STRIDE * HEAD_DIM
    kc = kc.reshape(B * NSA_KV_HEADS, ncp, cw)
    vc = vc.reshape(B * NSA_KV_HEADS, ncp, cw)
    k_cmp = _compress(kc, cmp_pos[0], cmp_w1[0], cmp_w2[0], qk_g[1], cos64[cmp_end], sin64[cmp_end], True)
    v_cmp = _compress(vc, cmp_pos[1], cmp_w1[1], cmp_w2[1], qk_g[1], cos64[cmp_end], sin64[cmp_end], False)
    return _nsa_attn(q, k_cmp, v_cmp, ks, vs, kw, vw, p_nsa, tq=128, tk=min(512, S))


CONV_HALO = 32


def _conv_kernel(cur_ref, halo_ref, w_ref, vec_ref, o_ref, buf, y_scr, *, ts):
    i = pl.program_id(1)

    def glu(p):
        return p[:, :CONV_CH] * _sigmoid(p[:, CONV_CH:])

    buf[0:CONV_HALO, :] = jnp.where(i > 0, glu(halo_ref[0]), 0.0)
    buf[CONV_HALO:, :] = glu(cur_ref[0])
    rt = min(ts, 128)
    shift = CONV_HALO - (CONV_K - 1)
    for c in range(CONV_CH // LANE):
        cs = slice(c * LANE, (c + 1) * LANE)
        for r in range(ts // rt):
            acc = jnp.zeros((rt, LANE), F32)
            for j in range(CONV_K):
                acc = acc + w_ref[j:j + 1, cs] * buf[r * rt + shift + j:r * rt + shift + j + rt, cs]
            y_scr[r * rt:(r + 1) * rt, cs] = acc
    y = y_scr[...] + vec_ref[0:1, :]
    mu = jnp.mean(y, axis=-1, keepdims=True)
    yc = y - mu
    var = jnp.mean(yc * yc, axis=-1, keepdims=True)
    y = yc * lax.rsqrt(var + LN_EPS) * vec_ref[1:2, :] + vec_ref[2:3, :]
    o_ref[0] = (y * _sigmoid(y)).astype(BF16)


def _conformer(p, conv_w, conv_vec, ts):
    B, S, _ = p.shape
    w = jnp.pad(conv_w, ((0, 32 - CONV_K), (0, 0)))
    vec = jnp.pad(conv_vec, ((0, 5), (0, 0)))
    hb = ts // CONV_HALO
    return pl.pallas_call(
        functools.partial(_conv_kernel, ts=ts),
        grid=(B, S // ts),
        in_specs=[pl.BlockSpec((1, ts, 2 * CONV_CH), lambda b, i: (b, i, 0)),
                  pl.BlockSpec((1, CONV_HALO, 2 * CONV_CH), lambda b, i: (b, jnp.maximum(i * hb - 1, 0), 0)),
                  pl.BlockSpec((32, CONV_CH), lambda b, i: (0, 0)),
                  pl.BlockSpec((8, CONV_CH), lambda b, i: (0, 0))],
        out_specs=pl.BlockSpec((1, ts, CONV_CH), lambda b, i: (b, i, 0)),
        out_shape=jax.ShapeDtypeStruct((B, S, CONV_CH), BF16),
        scratch_shapes=[pltpu.VMEM((CONV_HALO + ts, CONV_CH), F32), pltpu.VMEM((ts, CONV_CH), F32)],
        compiler_params=_params("parallel", "parallel"),
        name="conformer",
    )(p, p, w, vec)


RW_LORA_OFF = 3 * RWKV_W
RW_GATE_OFF = RW_LORA_OFF + DECAY_LORA + ICLR_LORA
RW_VRES_OFF = RWKV_IN_PAD - LANE


def _rwkv_prep_kernel(*refs, ts, has_vres):
    if has_vres:
        (cur_ref, halo_ref, mu_ref, vec_ref, wup_ref, aup_ref, gup_ref, bd_ref, vf_ref, vb_ref, vup_ref,
         r_ref, lw_ref, k_ref, v_ref, kk_ref, kka_ref, g_ref, vd_ref) = refs
    else:
        (cur_ref, halo_ref, mu_ref, vec_ref, wup_ref, aup_ref, gup_ref, bd_ref,
         r_ref, lw_ref, k_ref, v_ref, kk_ref, kka_ref, g_ref, vd_ref) = refs
    i = pl.program_id(1)
    p = cur_ref[0]
    last = jnp.where(i > 0, halo_ref[0, 7:8, :], 0.0)
    row = lax.broadcasted_iota(jnp.int32, p.shape, 0)
    prev = jnp.where(row == 0, last, pltpu.roll(p, 1, 0))
    u = p + (prev - p) * mu_ref[...]
    C = RWKV_W
    r, k, v = u[:, :C], u[:, C:2 * C], u[:, 2 * C:3 * C]
    lora = u[:, RW_LORA_OFF:RW_GATE_OFF]
    w_in = vec_ref[0:1, :] + _dot(jnp.tanh(lora).astype(BF16), wup_ref[...])
    sp = jnp.maximum(-w_in, 0.0) + jnp.log(1.0 + jnp.exp(-jnp.abs(w_in)))
    lw = -jnp.exp(-sp - 0.5)
    a = _sigmoid(vec_ref[1:2, :] + _dot(lora.astype(BF16), aup_ref[...]))
    g = _dot(_sigmoid(u[:, RW_GATE_OFF:]).astype(BF16), gup_ref[...])
    if has_vres:
        mix = _sigmoid(vb_ref[...] + _dot(u[:, RW_VRES_OFF:].astype(BF16), vup_ref[...]))
        v = v + (vf_ref[0] - v) * mix
    kk = k * vec_ref[2:3, :]
    ss = _dot(kk * kk, bd_ref[...], precision=HI)
    kk = kk / jnp.maximum(jnp.sqrt(ss), 1e-12)
    k = k * (1.0 + (a - 1.0) * vec_ref[3:4, :])
    g_ref[0] = g
    vd_ref[0] = v
    for val, ref in ((r, r_ref), (lw, lw_ref), (k, k_ref), (v, v_ref), (kk, kk_ref), (kk * a, kka_ref)):
        for h in range(RWKV_HEADS):
            ref[0, h] = val[:, h * HEAD_DIM:(h + 1) * HEAD_DIM]


def _pad_rows(w, n):
    return jnp.pad(w, ((0, n - w.shape[0]), (0, 0)))


def _rwkv_prep(p, mu, vec, w_up, a_up, g_up, v_first, vres_up, vres_b, ts):
    B, S, _ = p.shape
    C = RWKV_W
    has_vres = v_first is not None
    mu_p = jnp.pad(mu, (0, RWKV_IN_PAD - mu.shape[0])).reshape(1, RWKV_IN_PAD)
    n_lora = DECAY_LORA + ICLR_LORA
    wup = _pad_rows(w_up, n_lora).astype(BF16)
    aup = jnp.pad(a_up, ((DECAY_LORA, 0), (0, 0))).astype(BF16)
    gup = _pad_rows(g_up, RWKV_IN_PAD - RW_GATE_OFF).astype(BF16)
    bd = jnp.kron(jnp.eye(RWKV_HEADS, dtype=F32), jnp.ones((HEAD_DIM, HEAD_DIM), F32))
    const = lambda shape: pl.BlockSpec(shape, lambda b, i: (0,) * len(shape))
    dense = pl.BlockSpec((1, ts, C), lambda b, i: (b, i, 0))
    heads = pl.BlockSpec((1, RWKV_HEADS, ts, HEAD_DIM), lambda b, i: (b, 0, i, 0))
    in_specs = [pl.BlockSpec((1, ts, RWKV_IN_PAD), lambda b, i: (b, i, 0)),
                pl.BlockSpec((1, 8, RWKV_IN_PAD), lambda b, i: (b, jnp.maximum(i * (ts // 8) - 1, 0), 0)),
                const((1, RWKV_IN_PAD)), const((8, C)), const((n_lora, C)), const((n_lora, C)),
                const((RWKV_IN_PAD - RW_GATE_OFF, C)), const((C, C))]
    args = [p, p, mu_p, jnp.pad(vec, ((0, 1), (0, 0))), wup, aup, gup, bd]
    if has_vres:
        off = RWKV_IN_PAD - LANE
        lo = 3 * C + n_lora + GATE_LORA - off
        vup = jnp.pad(vres_up, ((lo, LANE - lo - VRES_LORA), (0, 0))).astype(BF16)
        in_specs += [dense, const((1, C)), const((LANE, C))]
        args += [v_first, vres_b.reshape(1, C), vup]
    hshape = jax.ShapeDtypeStruct((B, RWKV_HEADS, S, HEAD_DIM), F32)
    dshape = jax.ShapeDtypeStruct((B, S, C), F32)
    return pl.pallas_call(
        functools.partial(_rwkv_prep_kernel, ts=ts, has_vres=has_vres),
        grid=(B, S // ts),
        in_specs=in_specs,
        out_specs=[heads] * 6 + [dense, dense],
        out_shape=[hshape] * 6 + [dshape, dshape],
        compiler_params=_params("parallel", "parallel"),
        name="rwkv_prep",
    )(*args)


def _unit_lower_inverse(a):
    n = a.shape[0]
    eye = jnp.where(lax.broadcasted_iota(jnp.int32, (n, n), 0) == lax.broadcasted_iota(jnp.int32, (n, n), 1),
                    1.0, 0.0)
    x = eye - a
    pw = _dot(a, a, precision=HI)
    steps = n.bit_length() - 1
    for s in range(1, steps):
        x = x + _dot(x, pw, precision=HI)
        if s + 1 < steps:
            pw = _dot(pw, pw, precision=HI)
    return x


def _rwkv_scan_kernel(r_ref, lw_ref, k_ref, v_ref, kk_ref, kka_ref, g_ref, gn_ref, o_ref, state,
                      *, tt):
    @pl.when(pl.program_id(1) == 0)
    def _():
        state[...] = jnp.zeros_like(state)

    ri = lax.broadcasted_iota(jnp.int32, (CHUNK, CHUNK), 0)
    ci = lax.broadcasted_iota(jnp.int32, (CHUNK, CHUNK), 1)
    incl = ci <= ri
    strict = ci < ri
    tri = jnp.where(incl, 1.0, 0.0)

    def chunk(c, carry):
        rows = pl.ds(pl.multiple_of(c * CHUNK, CHUNK), CHUNK)
        outs = []
        for h in range(RWKV_HEADS):
            r = r_ref[0, h, rows, :]
            lw = lw_ref[0, h, rows, :]
            k = k_ref[0, h, rows, :]
            v = v_ref[0, h, rows, :]
            kk = kk_ref[0, h, rows, :]
            kka = kka_ref[0, h, rows, :]
            cs = _dot(tri, lw, precision=HI)
            g_in = jnp.exp(cs)
            g_inv = jnp.exp(-cs)
            kap = (kk * jnp.exp(cs - lw)).astype(BF16)
            bt = (kka * g_inv).astype(BF16)
            kt = (k * g_inv).astype(BF16)
            rt = (r * g_in).astype(BF16)
            vb = v.astype(BF16)
            a_kb = jnp.where(strict, _dot_nt(kap, bt), 0.0)
            a_kk = jnp.where(strict, _dot_nt(kap, kt), 0.0)
            a_rk = jnp.where(incl, _dot_nt(rt, kt), 0.0)
            a_rb = jnp.where(incl, _dot_nt(rt, bt), 0.0)
            t_inv = _unit_lower_inverse(a_kb)
            s0 = state[h]
            s0b = s0.astype(BF16)
            rhs = _dot_nt(kap, s0b) + _dot(a_kk.astype(BF16), vb)
            pm = _dot(t_inv, rhs, precision=HI)
            pb = pm.astype(BF16)
            y = _dot_nt(rt, s0b) + _dot(a_rk.astype(BF16), vb) - _dot(a_rb.astype(BF16), pb)
            state[h] = (s0 + _dot_tn(vb, kt) - _dot_tn(pb, bt)) * g_in[CHUNK - 1:CHUNK, :]
            ym = jnp.mean(y, axis=-1, keepdims=True)
            yc = y - ym
            yv = jnp.mean(yc * yc, axis=-1, keepdims=True)
            yn = yc * lax.rsqrt(yv + GN_EPS) * gn_ref[0, h] + gn_ref[1, h]
            bonus = jnp.sum(r * k * gn_ref[2, h], axis=-1, keepdims=True) * v
            outs.append(yn + bonus)
        out = jnp.concatenate(outs, axis=1) * g_ref[0, rows, :]
        o_ref[0, rows, :] = out.astype(BF16)
        return carry

    lax.fori_loop(0, tt // CHUNK, chunk, 0)


def _rwkv_scan(r, lw, k, v, kk, kka, g, vec, tt):
    B, H, S, N = r.shape
    gn = jnp.stack([vec[5], vec[6], vec[4]]).reshape(3, H, 1, N)
    heads = pl.BlockSpec((1, H, tt, N), lambda b, i: (b, 0, i, 0))
    dense = pl.BlockSpec((1, tt, H * N), lambda b, i: (b, i, 0))
    return pl.pallas_call(
        functools.partial(_rwkv_scan_kernel, tt=tt),
        grid=(B, S // tt),
        in_specs=[heads] * 6 + [dense, pl.BlockSpec((3, H, 1, N), lambda b, i: (0, 0, 0, 0))],
        out_specs=dense,
        out_shape=jax.ShapeDtypeStruct((B, S, H * N), BF16),
        scratch_shapes=[pltpu.VMEM((H, N, N), F32)],
        compiler_params=_params("parallel", "arbitrary"),
        name="rwkv_scan",
    )(r, lw, k, v, kk, kka, g, gn)


def _merge_kernel(x_ref, an_ref, ac_ref, ar_ref, gl_ref, wn_ref, wc_ref, wr_ref, wo_ref, g2_ref,
                  xo_ref, h_ref):
    D = x_ref.shape[1]
    o_n = _dot(an_ref[...].astype(BF16), wn_ref[...])
    mixed = _sigmoid(gl_ref[:, 0:D]) * o_n
    o_c = _dot(ac_ref[...], wc_ref[...])
    mixed = mixed + _sigmoid(gl_ref[:, D:2 * D]) * o_c
    o_r = _dot(ar_ref[...], wr_ref[...])
    mixed = mixed + _sigmoid(gl_ref[:, 2 * D:3 * D]) * o_r
    x = x_ref[...] + _dot(mixed.astype(BF16), wo_ref[...])
    xo_ref[...] = x
    ms = jnp.mean(x * x, axis=-1, keepdims=True)
    h_ref[...] = (x * lax.rsqrt(ms + RMS_EPS) * g2_ref[...]).astype(BF16)


def _merge(x, a_nsa, a_conv, a_rwkv, gl, w_n, w_c, w_r, w_o, g2, tm):
    T, D = x.shape
    row = lambda w: pl.BlockSpec((tm, w), lambda i: (i, 0))
    const = lambda shape: pl.BlockSpec(shape, lambda i: (0,) * len(shape))
    return pl.pallas_call(
        _merge_kernel,
        grid=(T // tm,),
        in_specs=[row(D), row(a_nsa.shape[1]), row(a_conv.shape[1]), row(a_rwkv.shape[1]), row(3 * D),
                  const(w_n.shape), const(w_c.shape), const(w_r.shape), const(w_o.shape), const((1, D))],
        out_specs=[row(D), row(D)],
        out_shape=[jax.ShapeDtypeStruct((T, D), F32), jax.ShapeDtypeStruct((T, D), BF16)],
        compiler_params=_params("parallel"),
        name="merge",
    )(x, a_nsa, a_conv, a_rwkv, gl, w_n.astype(BF16), w_c.astype(BF16), w_r.astype(BF16),
      w_o.astype(BF16), g2.reshape(1, D))


def _ffn_kernel(x_ref, h_ref, wg_ref, wu_ref, wd_ref, o_ref, acc):
    j = pl.program_id(1)

    @pl.when(j == 0)
    def _():
        acc[...] = x_ref[...]

    h = h_ref[...]
    gate = _dot(h, wg_ref[...])
    up = _dot(h, wu_ref[...])
    act = (gate * _sigmoid(gate) * up).astype(BF16)
    acc[...] += _dot(act, wd_ref[...])

    @pl.when(j == pl.num_programs(1) - 1)
    def _():
        o_ref[...] = acc[...]


def _ffn(x, h, w_gu, w_down, tm, tf):
    T, D = x.shape
    FF = w_down.shape[0]
    wg = w_gu[:, :FF].astype(BF16)
    wu = w_gu[:, FF:].astype(BF16)
    return pl.pallas_call(
        _ffn_kernel,
        grid=(T // tm, FF // tf),
        in_specs=[pl.BlockSpec((tm, D), lambda i, j: (i, 0)),
                  pl.BlockSpec((tm, D), lambda i, j: (i, 0)),
                  pl.BlockSpec((D, tf), lambda i, j: (0, j)),
                  pl.BlockSpec((D, tf), lambda i, j: (0, j)),
                  pl.BlockSpec((tf, D), lambda i, j: (j, 0))],
        out_specs=pl.BlockSpec((tm, D), lambda i, j: (i, 0)),
        out_shape=jax.ShapeDtypeStruct((T, D), F32),
        scratch_shapes=[pltpu.VMEM((tm, D), F32)],
        compiler_params=_params("parallel", "arbitrary"),
        name="ffn_dense",
    )(x, h, wg, wu, w_down.astype(BF16))


def _router_kernel(h_ref, w_ref, b_ref, gw_ref):
    logits = _dot(h_ref[...], w_ref[...]) + b_ref[...]
    lane = lax.broadcasted_iota(jnp.int32, logits.shape, 1)
    m1 = jnp.max(logits, axis=-1, keepdims=True)
    i1 = jnp.min(jnp.where(logits == m1, lane, LANE), axis=-1, keepdims=True)
    rest = jnp.where(lane == i1, -jnp.inf, logits)
    m2 = jnp.max(rest, axis=-1, keepdims=True)
    i2 = jnp.min(jnp.where(rest == m2, lane, LANE), axis=-1, keepdims=True)
    e2 = jnp.exp(m2 - m1)
    w1 = 1.0 / (1.0 + e2)
    w2 = e2 / (1.0 + e2)
    gw_ref[...] = jnp.where(lane == i1, w1, jnp.where(lane == i2, w2, 0.0))


def _router(h, router_w, router_b, tm):
    T, D = h.shape
    w = _pad_cols(router_w, LANE).astype(BF16)
    b = jnp.pad(router_b, (0, LANE - N_EXPERTS), constant_values=-jnp.inf).reshape(1, LANE)
    return pl.pallas_call(
        _router_kernel,
        grid=(T // tm,),
        in_specs=[pl.BlockSpec((tm, D), lambda i: (i, 0)),
                  pl.BlockSpec((D, LANE), lambda i: (0, 0)),
                  pl.BlockSpec((1, LANE), lambda i: (0, 0))],
        out_specs=pl.BlockSpec((tm, LANE), lambda i: (i, 0)),
        out_shape=jax.ShapeDtypeStruct((T, LANE), F32),
        compiler_params=_params("parallel"),
        name="moe_router",
    )(h, w, b)


def _moe_kernel(x_ref, h_ref, gw_ref, wg_ref, wu_ref, wd_ref, o_ref, acc):
    e = pl.program_id(1)
    j = pl.program_id(2)

    @pl.when((e == 0) & (j == 0))
    def _():
        acc[...] = x_ref[...]

    gw = gw_ref[...]
    lane = lax.broadcasted_iota(jnp.int32, gw.shape, 1)
    w_e = jnp.sum(jnp.where(lane == e, gw, 0.0), axis=-1, keepdims=True)
    h = h_ref[...]
    gate = _dot(h, wg_ref[0])
    up = _dot(h, wu_ref[0])
    act = (gate * _sigmoid(gate) * up).astype(BF16)
    acc[...] += w_e * _dot(act, wd_ref[0])

    @pl.when((e == pl.num_programs(1) - 1) & (j == pl.num_programs(2) - 1))
    def _():
        o_ref[...] = acc[...]


def _moe(x, h, gw, w_gu, w_down, tm, tf):
    T, D = x.shape
    E, FF, _ = w_down.shape
    wg = w_gu[:, :, :FF].astype(BF16)
    wu = w_gu[:, :, FF:].astype(BF16)
    return pl.pallas_call(
        _moe_kernel,
        grid=(T // tm, E, FF // tf),
        in_specs=[pl.BlockSpec((tm, D), lambda i, e, j: (i, 0)),
                  pl.BlockSpec((tm, D), lambda i, e, j: (i, 0)),
                  pl.BlockSpec((tm, LANE), lambda i, e, j: (i, 0)),
                  pl.BlockSpec((1, D, tf), lambda i, e, j: (e, 0, j)),
                  pl.BlockSpec((1, D, tf), lambda i, e, j: (e, 0, j)),
                  pl.BlockSpec((1, tf, D), lambda i, e, j: (e, j, 0))],
        out_specs=pl.BlockSpec((tm, D), lambda i, e, j: (i, 0)),
        out_shape=jax.ShapeDtypeStruct((T, D), F32),
        scratch_shapes=[pltpu.VMEM((tm, D), F32)],
        compiler_params=_params("parallel", "arbitrary", "arbitrary"),
        name="moe_experts",
    )(x, h, gw, wg, wu, w_down.astype(BF16))


def _pad_cols(w, n):
    return jnp.pad(w, ((0, 0), (0, n - w.shape[1])))


def kernel(x, mix_norm_g, w_in_first, w_in_rest, rwkv_mu_first, rwkv_mu_rest, nsa_qk_g, nsa_cmp_pos, nsa_cmp_w1, nsa_cmp_w2, nsa_w_o, conv_w, conv_vec, conv_w_o, rwkv_vec, rwkv_w_up, rwkv_a_up, rwkv_g_up, rwkv_vres_up, rwkv_vres_b, rwkv_w_o, w_out, ffn_norm_g, dense_w_gu, dense_w_down, router_w, router_b, moe_w_gu, moe_w_down):
    B, S, D = x.shape
    T = B * S
    depth = mix_norm_g.shape[0]
    off_conv = NSA_IN
    off_gate = off_conv + 2 * CONV_CH
    off_rwkv = off_gate + 3 * D
    tm = min(512, T)
    xt = x.reshape(T, D)
    v_first = None
    for layer in range(depth):
        if layer == 0:
            w_in, mu, vres_up, vres_b = w_in_first, rwkv_mu_first, None, None
        else:
            w_in, mu = w_in_rest[layer - 1], rwkv_mu_rest[layer - 1]
            vres_up, vres_b = rwkv_vres_up[layer - 1], rwkv_vres_b[layer - 1]
        g1 = mix_norm_g[layer]
        w_nsa = _pad_cols(w_in[:, :off_conv], NSA_IN_PAD).astype(BF16)
        w_conv = w_in[:, off_conv:off_gate].astype(BF16)
        w_gate = w_in[:, off_gate:off_rwkv].astype(BF16)
        w_rwkv = _pad_cols(w_in[:, off_rwkv:], RWKV_IN_PAD).astype(BF16)
        p_nsa = _rms_mm(xt, g1, w_nsa, tm, NSA_IN_PAD).reshape(B, S, NSA_IN_PAD)
        p_conv = _rms_mm(xt, g1, w_conv, tm, 2 * CONV_CH).reshape(B, S, 2 * CONV_CH)
        p_gate = _rms_mm(xt, g1, w_gate, tm, D)
        p_rwkv = _rms_mm(xt, g1, w_rwkv, tm, RWKV_IN_PAD).reshape(B, S, RWKV_IN_PAD)

        a_nsa = _nsa_branch(p_nsa, nsa_qk_g[layer], nsa_cmp_pos[layer], nsa_cmp_w1[layer],
                            nsa_cmp_w2[layer], B, S)
        a_conv = _conformer(p_conv, conv_w[layer], conv_vec[layer], ts=min(512, S))
        r, lw, k, v, kk, kka, g, v_dense = _rwkv_prep(
            p_rwkv, mu, rwkv_vec[layer], rwkv_w_up[layer], rwkv_a_up[layer], rwkv_g_up[layer],
            v_first, vres_up, vres_b, ts=min(256, S))
        if layer == 0:
            v_first = v_dense
        a_rwkv = _rwkv_scan(r, lw, k, v, kk, kka, g, rwkv_vec[layer], tt=min(512, S))

        xt, h2 = _merge(xt, a_nsa.reshape(T, NSA_Q_W), a_conv.reshape(T, CONV_CH),
                        a_rwkv.reshape(T, RWKV_W), p_gate, nsa_w_o[layer], conv_w_o[layer],
                        rwkv_w_o[layer], w_out[layer], ffn_norm_g[layer], tm)
        if layer % 2 == 0:
            xt = _ffn(xt, h2, dense_w_gu[layer // 2], dense_w_down[layer // 2], min(1024, T), 256)
        else:
            gw = _router(h2, router_w[layer // 2], router_b[layer // 2], tm)
            xt = _moe(xt, h2, gw, moe_w_gu[layer // 2], moe_w_down[layer // 2], min(1024, T), 512)
    return xt.reshape(B, S, D)
```

```python
import functools

import jax
import jax.numpy as jnp
from jax import lax
from jax.experimental import pallas as pl
from jax.experimental.pallas import tpu as pltpu

F32 = jnp.float32
BF16 = jnp.bfloat16
HI = lax.Precision.HIGHEST

NSA_HEADS = 8
NSA_KV_HEADS = 2
NSA_GROUP = NSA_HEADS // NSA_KV_HEADS
HEAD_DIM = 64
NSA_Q_W = NSA_HEADS * HEAD_DIM
NSA_KV_W = NSA_KV_HEADS * HEAD_DIM
CMP_STRIDE = 16
CMP_LEN = 32
SEL_BLOCK = 64
N_SEL = 8
WINDOW = 512
SEL_FORCE = 1e4
CONV_CH = 512
CONV_K = 31
RWKV_HEADS = 8
RWKV_W = RWKV_HEADS * HEAD_DIM
DECAY_LORA = 64
ICLR_LORA = 64
GATE_LORA = 160
VRES_LORA = 32
N_EXPERTS = 8
ROPE_THETA = 10000.0
RMS_EPS = 1e-6
LN_EPS = 1e-5
GN_EPS = 64e-5
NEG_INF = -1e30
LOG2E = 1.4426950408889634

NSA_IN = NSA_Q_W + 6 * NSA_KV_W + 3 * NSA_HEADS
NSA_IN_PAD = 1408
NSA_GATE_BLK = (NSA_Q_W + 6 * NSA_KV_W) // 128
RWKV_IN_PAD = 1920
LANE = 128
VMEM_LIMIT = 56 * 1024 * 1024

CHUNK = 64


def _dot(a, b, precision=None):
    return jnp.dot(a, b, preferred_element_type=F32, precision=precision)


def _dot_nt(a, b, precision=None):
    return lax.dot_general(a, b, (((1,), (1,)), ((), ())), preferred_element_type=F32,
                           precision=precision)


def _dot_tn(a, b, precision=None):
    return lax.dot_general(a, b, (((0,), (0,)), ((), ())), preferred_element_type=F32,
                           precision=precision)


def _sigmoid(x):
    return 1.0 / (1.0 + jnp.exp(-x))


def _params(*sem):
    return pltpu.CompilerParams(dimension_semantics=sem, vmem_limit_bytes=VMEM_LIMIT)


def _rms_mm_kernel(x_ref, g_ref, w_ref, o_ref, h_scr):
    @pl.when(pl.program_id(1) == 0)
    def _():
        x = x_ref[...]
        ms = jnp.mean(x * x, axis=-1, keepdims=True)
        h_scr[...] = (x * lax.rsqrt(ms + RMS_EPS) * g_ref[...]).astype(BF16)

    o_ref[...] = _dot(h_scr[...], w_ref[...])


def _rms_mm(x, g, w, tm, tn):
    T, D = x.shape
    N = w.shape[1]
    return pl.pallas_call(
        _rms_mm_kernel,
        grid=(T // tm, N // tn),
        in_specs=[pl.BlockSpec((tm, D), lambda i, j: (i, 0)),
                  pl.BlockSpec((1, D), lambda i, j: (0, 0)),
                  pl.BlockSpec((D, tn), lambda i, j: (0, j))],
        out_specs=pl.BlockSpec((tm, tn), lambda i, j: (i, j)),
        out_shape=jax.ShapeDtypeStruct((T, N), F32),
        scratch_shapes=[pltpu.VMEM((tm, D), BF16)],
        compiler_params=_params("parallel", "arbitrary"),
        name="rms_proj",
    )(x, g.reshape(1, D), w)


def _rope_dense(x, cos, sin_signed):
    w = x.shape[1]
    lane = lax.broadcasted_iota(jnp.int32, x.shape, 1)
    first_half = (lane % HEAD_DIM) < (HEAD_DIM // 2)
    rot = jnp.where(first_half, pltpu.roll(x, w - HEAD_DIM // 2, 1), pltpu.roll(x, HEAD_DIM // 2, 1))
    return x * cos + rot * sin_signed


def _head_rms(x, bd, g):
    ms = _dot(x * x, bd, precision=HI)
    return x * lax.rsqrt(ms + RMS_EPS) * g


def _nsa_prep_kernel(p_ref, cos_ref, sin_ref, gq_ref, gks_ref, gkw_ref, bd_ref,
                     q_ref, kc_ref, vc_ref, ks_ref, vs_ref, kw_ref, vw_ref):
    cos = cos_ref[...]
    sin = sin_ref[...]
    bd = bd_ref[...]
    q = p_ref[0, :, 0:NSA_Q_W]
    q = _rope_dense(_head_rms(q, bd, gq_ref[...]), cos, sin) * (HEAD_DIM ** -0.5 * LOG2E)
    for h in range(NSA_HEADS):
        q_ref[0, h] = q[:, h * HEAD_DIM:(h + 1) * HEAD_DIM].astype(BF16)

    def slab(i):
        return p_ref[0, :, NSA_Q_W + i * NSA_KV_W:NSA_Q_W + (i + 1) * NSA_KV_W]

    cos_k = cos[:, :NSA_KV_W]
    sin_k = sin[:, :NSA_KV_W]
    bd_k = bd[:NSA_KV_W, :NSA_KV_W]
    ks = _rope_dense(_head_rms(slab(2), bd_k, gks_ref[...]), cos_k, sin_k)
    kw = _rope_dense(_head_rms(slab(4), bd_k, gkw_ref[...]), cos_k, sin_k)
    for val, ref in ((slab(0), kc_ref), (slab(1), vc_ref), (ks, ks_ref), (slab(3), vs_ref),
                     (kw, kw_ref), (slab(5), vw_ref)):
        for g in range(NSA_KV_HEADS):
            ref[0, g] = val[:, g * HEAD_DIM:(g + 1) * HEAD_DIM].astype(BF16)


def _nsa_prep(p, cos, sin, qk_g, ts):
    B, S, _ = p.shape
    bd = jnp.kron(jnp.eye(NSA_HEADS, dtype=F32), jnp.full((HEAD_DIM, HEAD_DIM), 1.0 / HEAD_DIM, F32))
    gq = jnp.tile(qk_g[0], NSA_HEADS).reshape(1, NSA_Q_W)
    gks = jnp.tile(qk_g[2], NSA_KV_HEADS).reshape(1, NSA_KV_W)
    gkw = jnp.tile(qk_g[3], NSA_KV_HEADS).reshape(1, NSA_KV_W)
    kv_shape = jax.ShapeDtypeStruct((B, NSA_KV_HEADS, S, HEAD_DIM), BF16)
    kv_spec = pl.BlockSpec((1, NSA_KV_HEADS, ts, HEAD_DIM), lambda b, i: (b, 0, i, 0))
    const = lambda shape: pl.BlockSpec(shape, lambda b, i: (0,) * len(shape))
    return pl.pallas_call(
        _nsa_prep_kernel,
        grid=(B, S // ts),
        in_specs=[pl.BlockSpec((1, ts, NSA_IN_PAD), lambda b, i: (b, i, 0)),
                  pl.BlockSpec((ts, NSA_Q_W), lambda b, i: (i, 0)),
                  pl.BlockSpec((ts, NSA_Q_W), lambda b, i: (i, 0)),
                  const((1, NSA_Q_W)), const((1, NSA_KV_W)), const((1, NSA_KV_W)),
                  const((NSA_Q_W, NSA_Q_W))],
        out_specs=[pl.BlockSpec((1, NSA_HEADS, ts, HEAD_DIM), lambda b, i: (b, 0, i, 0))] + [kv_spec] * 6,
        out_shape=[jax.ShapeDtypeStruct((B, NSA_HEADS, S, HEAD_DIM), BF16)] + [kv_shape] * 6,
        compiler_params=_params("parallel", "parallel"),
        name="nsa_prep",
    )(p, cos, sin, gq, gks, gkw, bd)


def _gelu_tanh(x):
    return 0.5 * x * (1.0 + jnp.tanh(0.7978845608028654 * (x + 0.044715 * x * x * x)))


def _compress_kernel(a_ref, posa_ref, posb_ref, w1a_ref, w1b_ref, w2_ref, g_ref, cos_ref, sin_ref,
                     perm_ref, o_ref, *, is_key):
    a = a_ref[0]
    ncp = a.shape[0]
    w1a = w1a_ref[...]
    w1b = w1b_ref[...]
    p1 = _dot(a, w1a)
    p2 = _dot(a, w1b)
    hp = _dot(posa_ref[...], w1a) + _dot(posb_ref[...], w1b)
    h = _gelu_tanh(p1 + pltpu.roll(p2, ncp - 1, 0) + hp[0:1])
    o = _dot(h.astype(BF16), w2_ref[...])
    if is_key:
        ms = jnp.mean(o * o, axis=-1, keepdims=True)
        o = o * lax.rsqrt(ms + RMS_EPS) * g_ref[...]
        o = o * cos_ref[...] + _dot(o, perm_ref[...], precision=HI) * sin_ref[...]
    row = lax.broadcasted_iota(jnp.int32, o.shape, 0)
    o_ref[0] = jnp.where(row < ncp - 1, o, 0.0).astype(BF16)


def _compress(t, pos_emb, w1, w2, g, cos_c, sin_c, is_key):
    BG, ncp, cw = t.shape
    hid = w1.shape[1]
    pos = pos_emb.reshape(1, CMP_LEN * HEAD_DIM)
    posa = jnp.tile(pos[:, :cw], (8, 1)).astype(BF16)
    posb = jnp.tile(pos[:, cw:], (8, 1)).astype(BF16)
    half = HEAD_DIM // 2
    perm = jnp.roll(jnp.eye(HEAD_DIM, dtype=F32), half, axis=0)
    const = lambda shape: pl.BlockSpec(shape, lambda b: (0,) * len(shape))
    return pl.pallas_call(
        functools.partial(_compress_kernel, is_key=is_key),
        grid=(BG,),
        in_specs=[pl.BlockSpec((1, ncp, cw), lambda b: (b, 0, 0)),
                  const((8, cw)), const((8, cw)), const((cw, hid)), const((cw, hid)),
                  const((hid, HEAD_DIM)), const((1, HEAD_DIM)),
                  const((ncp, HEAD_DIM)), const((ncp, HEAD_DIM)), const((HEAD_DIM, HEAD_DIM))],
        out_specs=pl.BlockSpec((1, ncp, HEAD_DIM), lambda b: (b, 0, 0)),
        out_shape=jax.ShapeDtypeStruct((BG, ncp, HEAD_DIM), BF16),
        compiler_params=_params("parallel"),
        name="nsa_compress_k" if is_key else "nsa_compress_v",
    )(t, posa, posb, w1[:cw].astype(BF16), w1[cw:].astype(BF16), w2.astype(BF16),
      g.reshape(1, HEAD_DIM), cos_c, sin_c, perm)


def _exp_cols(s):
    m = jnp.max(s, axis=0, keepdims=True)
    e = jnp.exp2(s - m)
    return e, jnp.sum(e, axis=0, keepdims=True)


def _heads_to_rows(o, tq):
    return jnp.concatenate([o[:, h * tq:(h + 1) * tq] for h in range(NSA_GROUP)], axis=0)


def _nsa_attn_kernel(q_ref, kc_ref, vc_ref, ks_ref, vs_ref, kw_ref, vw_ref, gl_ref, o_ref, selb_ref, s_ref,
                     *, tq, tk, seq):
    g = pl.program_id(1)
    i = pl.program_id(2)
    rows = NSA_GROUP * tq
    ncp = seq // CMP_STRIDE
    ns = seq // SEL_BLOCK
    n_sel = min(N_SEL, ns)
    q = q_ref[0].reshape(rows, HEAD_DIM)
    t0 = i * tq
    t_q = t0 + lax.broadcasted_iota(jnp.int32, (1, tq), 1)

    def per_head(x):
        return jnp.concatenate([x] * NSA_GROUP, axis=1)

    c_end = lax.broadcasted_iota(jnp.int32, (ncp, 1), 0) * CMP_STRIDE + (CMP_LEN - 1)
    bias = jnp.where(c_end <= t_q, 0.0, NEG_INF)
    e, l = _exp_cols(_dot_nt(kc_ref[0], q) + per_head(bias))
    p_c = e * (per_head(jnp.where(t_q >= CMP_LEN - 1, 1.0, 0.0)) / l)
    o_c = _dot_tn(vc_ref[0], p_c.astype(BF16))

    p_sum = p_c[:, 0:tq]
    for h in range(1, NSA_GROUP):
        p_sum = p_sum + p_c[:, h * tq:(h + 1) * tq]
    s0 = lax.broadcasted_iota(jnp.int32, (ns, ncp), 0) * SEL_BLOCK
    c0 = lax.broadcasted_iota(jnp.int32, (ns, ncp), 1) * CMP_STRIDE
    overlap = jnp.where((c0 < s0 + SEL_BLOCK) & (c0 + CMP_LEN > s0), 1.0, 0.0).astype(BF16)
    imp = _dot_split_rhs(overlap, p_sum, 2)
    j = lax.broadcasted_iota(jnp.int32, (ns, tq), 0).astype(F32)
    cur = (t_q // SEL_BLOCK).astype(F32)
    forced = (j == 0.0) | (j == cur) | (j == cur - 1.0)
    score = jnp.where(forced, SEL_FORCE, jnp.where(j <= cur, imp, -1.0))
    for _ in range(n_sel):
        mx = jnp.max(score, axis=0, keepdims=True)
        first = jnp.min(jnp.where(score == mx, j, float(ns)), axis=0, keepdims=True)
        score = jnp.where(j == first, -jnp.inf, score)
    selb_ref[...] = jnp.where(score == -jnp.inf, 0.0, NEG_INF)

    wlen = WINDOW + tq
    w0 = pl.multiple_of(jnp.maximum(t0 - WINDOW, 0), tq)
    kpos = w0 + lax.broadcasted_iota(jnp.int32, (wlen, 1), 0)
    bias = jnp.where((kpos <= t_q) & (kpos > t_q - WINDOW), 0.0, NEG_INF)
    e, l = _exp_cols(_dot_nt(kw_ref[0, 0, pl.ds(w0, wlen), :], q) + per_head(bias))
    o_w = _dot_tn(vw_ref[0, 0, pl.ds(w0, wlen), :], e.astype(BF16)) / l

    key_off = lax.broadcasted_iota(jnp.int32, (SEL_BLOCK, 1), 0)

    last_tile = seq // tk - 1

    def scores(jt, slot):
        jt = jnp.minimum(jt, last_tile)
        k0 = pl.multiple_of(jt * tk, tk)
        bias = []
        for b in range(tk // SEL_BLOCK):
            picked = selb_ref[pl.ds(jt * (tk // SEL_BLOCK) + b, 1), :]
            causal = k0 + b * SEL_BLOCK + key_off <= t_q
            bias.append(jnp.where(causal, picked, NEG_INF))
        bias = jnp.concatenate(bias, axis=0)
        s_ref[slot] = _dot_nt(ks_ref[0, 0, pl.ds(k0, tk), :], q) + per_head(bias)

    def absorb(jt, slot, carry):
        m_run, l_run, acc = carry
        k0 = pl.multiple_of(jt * tk, tk)
        s = s_ref[slot]
        m_new = jnp.maximum(m_run, jnp.max(s, axis=0, keepdims=True))
        e = jnp.exp2(s - m_new)
        alpha = jnp.exp2(m_run - m_new)
        l_new = alpha * l_run + jnp.sum(e, axis=0, keepdims=True)
        acc = alpha * acc + _dot_tn(vs_ref[0, 0, pl.ds(k0, tk), :], e.astype(BF16))
        return m_new, l_new, acc

    def sel_pair(it, carry):
        scores(2 * it + 1, 1)
        carry = absorb(2 * it, 0, carry)
        scores(2 * it + 2, 0)
        return absorb(2 * it + 1, 1, carry)

    n_tiles = (t0 + tq - 1) // tk + 1
    init = (jnp.full((1, rows), NEG_INF, F32), jnp.zeros((1, rows), F32), jnp.zeros((HEAD_DIM, rows), F32))
    scores(0, 0)
    _, l_s, acc_s = lax.fori_loop(0, (n_tiles + 1) // 2, sel_pair, init)
    o_s = acc_s / l_s

    gw = NSA_GROUP * HEAD_DIM
    n = lax.broadcasted_iota(jnp.int32, (3 * gw, LANE), 0)
    col = lax.broadcasted_iota(jnp.int32, (3 * gw, LANE), 1)
    head = g * NSA_GROUP + (n % gw) // HEAD_DIM
    spread = jnp.where(col == head * 3 + n // gw, 1.0, 0.0).astype(BF16)
    gates = None
    for piece in _split_bf16(_sigmoid(gl_ref[0]), 3):
        part = _dot_nt(spread, piece)
        gates = part if gates is None else gates + part
    out = gates[0:gw] * _heads_to_rows(o_c, tq)
    out = out + gates[gw:2 * gw] * _heads_to_rows(o_s, tq)
    out = out + gates[2 * gw:] * _heads_to_rows(o_w, tq)
    o_ref[0] = out.T


def _nsa_attn(q, kc, vc, ks, vs, kw, vw, p_nsa, tq, tk):
    B, H, S, _ = q.shape
    G = NSA_KV_HEADS
    ncp = S // CMP_STRIDE
    assert S % (2 * tk) == 0
    cmp_spec = pl.BlockSpec((1, ncp, HEAD_DIM), lambda b, g, i: (b * G + g, 0, 0))
    seq_spec = pl.BlockSpec((1, 1, S, HEAD_DIM), lambda b, g, i: (b, g, 0, 0))
    return pl.pallas_call(
        functools.partial(_nsa_attn_kernel, tq=tq, tk=tk, seq=S),
        grid=(B, G, S // tq),
        in_specs=[pl.BlockSpec((1, NSA_GROUP, tq, HEAD_DIM), lambda b, g, i: (b, g, i, 0)),
                  cmp_spec, cmp_spec, seq_spec, seq_spec, seq_spec, seq_spec,
                  pl.BlockSpec((1, tq, LANE), lambda b, g, i: (b, i, NSA_GATE_BLK))],
        out_specs=pl.BlockSpec((1, tq, NSA_GROUP * HEAD_DIM), lambda b, g, i: (b, i, g)),
        out_shape=jax.ShapeDtypeStruct((B, S, NSA_Q_W), F32),
        scratch_shapes=[pltpu.VMEM((S // SEL_BLOCK, tq), F32),
                        pltpu.VMEM((2, tk, NSA_GROUP * tq), F32)],
        compiler_params=_params("parallel", "parallel", "arbitrary"),
        name="nsa_attn",
    )(q, kc, vc, ks, vs, kw, vw, p_nsa)


def _rope_tables(S):
    half = HEAD_DIM // 2
    inv = ROPE_THETA ** (-jnp.arange(half, dtype=F32) / half)
    ang = jnp.arange(S, dtype=F32)[:, None] * inv[None, :]
    cos, sin = jnp.cos(ang), jnp.sin(ang)
    cos64 = jnp.concatenate([cos, cos], axis=1)
    sin64 = jnp.concatenate([-sin, sin], axis=1)
    return cos64, sin64


def _nsa_branch(p_nsa, qk_g, cmp_pos, cmp_w1, cmp_w2, B, S):
    cos64, sin64 = _rope_tables(S)
    cos = jnp.tile(cos64, (1, NSA_HEADS))
    sin = jnp.tile(sin64, (1, NSA_HEADS))
    q, kc, vc, ks, vs, kw, vw = _nsa_prep(p_nsa, cos, sin, qk_g, ts=min(256, S))
    ncp = S // CMP_STRIDE
    cmp_end = jnp.minimum(jnp.arange(ncp) * CMP_STRIDE + CMP_LEN - 1, S - 1)
    cw = CMP_STRIDE * HEAD_DIM
    kc = kc.reshape(B * NSA_KV_HEADS, ncp, cw)
    vc = vc.reshape(B * NSA_KV_HEADS, ncp, cw)
    k_cmp = _compress(kc, cmp_pos[0], cmp_w1[0], cmp_w2[0], qk_g[1], cos64[cmp_end], sin64[cmp_end], True)
    v_cmp = _compress(vc, cmp_pos[1], cmp_w1[1], cmp_w2[1], qk_g[1], cos64[cmp_end], sin64[cmp_end], False)
    return _nsa_attn(q, k_cmp, v_cmp, ks, vs, kw, vw, p_nsa, tq=128, tk=256)


CONV_HALO = 32


def _conv_kernel(cur_ref, halo_ref, w_ref, vec_ref, o_ref, buf, y_scr, *, ts):
    i = pl.program_id(1)

    def glu(p):
        return p[:, :CONV_CH] * _sigmoid(p[:, CONV_CH:])

    buf[0:CONV_HALO, :] = jnp.where(i > 0, glu(halo_ref[0]), 0.0)
    buf[CONV_HALO:, :] = glu(cur_ref[0])
    rt = min(ts, 128)
    shift = CONV_HALO - (CONV_K - 1)
    for c in range(CONV_CH // LANE):
        cs = slice(c * LANE, (c + 1) * LANE)
        for r in range(ts // rt):
            acc = jnp.zeros((rt, LANE), F32)
            for j in range(CONV_K):
                acc = acc + w_ref[j:j + 1, cs] * buf[r * rt + shift + j:r * rt + shift + j + rt, cs]
            y_scr[r * rt:(r + 1) * rt, cs] = acc
    y = y_scr[...] + vec_ref[0:1, :]
    mu = jnp.mean(y, axis=-1, keepdims=True)
    yc = y - mu
    var = jnp.mean(yc * yc, axis=-1, keepdims=True)
    y = yc * lax.rsqrt(var + LN_EPS) * vec_ref[1:2, :] + vec_ref[2:3, :]
    o_ref[0] = (y * _sigmoid(y)).astype(BF16)


def _conformer(p, conv_w, conv_vec, ts):
    B, S, _ = p.shape
    w = jnp.pad(conv_w, ((0, 32 - CONV_K), (0, 0)))
    vec = jnp.pad(conv_vec, ((0, 5), (0, 0)))
    hb = ts // CONV_HALO
    return pl.pallas_call(
        functools.partial(_conv_kernel, ts=ts),
        grid=(B, S // ts),
        in_specs=[pl.BlockSpec((1, ts, 2 * CONV_CH), lambda b, i: (b, i, 0)),
                  pl.BlockSpec((1, CONV_HALO, 2 * CONV_CH), lambda b, i: (b, jnp.maximum(i * hb - 1, 0), 0)),
                  pl.BlockSpec((32, CONV_CH), lambda b, i: (0, 0)),
                  pl.BlockSpec((8, CONV_CH), lambda b, i: (0, 0))],
        out_specs=pl.BlockSpec((1, ts, CONV_CH), lambda b, i: (b, i, 0)),
        out_shape=jax.ShapeDtypeStruct((B, S, CONV_CH), BF16),
        scratch_shapes=[pltpu.VMEM((CONV_HALO + ts, CONV_CH), F32), pltpu.VMEM((ts, CONV_CH), F32)],
        compiler_params=_params("parallel", "parallel"),
        name="conformer",
    )(p, p, w, vec)


RW_LORA_OFF = 3 * RWKV_W
RW_GATE_OFF = RW_LORA_OFF + DECAY_LORA + ICLR_LORA
RW_VRES_OFF = RWKV_IN_PAD - LANE


def _rwkv_prep_kernel(*refs, ts, has_vres):
    if has_vres:
        (cur_ref, halo_ref, mu_ref, vec_ref, wup_ref, aup_ref, gup_ref, bd_ref, vf_ref, vb_ref, vup_ref,
         r_ref, lw_ref, k_ref, v_ref, kk_ref, kka_ref, g_ref) = refs
    else:
        (cur_ref, halo_ref, mu_ref, vec_ref, wup_ref, aup_ref, gup_ref, bd_ref,
         r_ref, lw_ref, k_ref, v_ref, kk_ref, kka_ref, g_ref) = refs
    i = pl.program_id(1)
    p = cur_ref[0]
    last = jnp.where(i > 0, halo_ref[0, 7:8, :], 0.0)
    row = lax.broadcasted_iota(jnp.int32, p.shape, 0)
    prev = jnp.where(row == 0, last, pltpu.roll(p, 1, 0))
    u = p + (prev - p) * mu_ref[...]
    C = RWKV_W
    r, k, v = u[:, :C], u[:, C:2 * C], u[:, 2 * C:3 * C]
    lora = u[:, RW_LORA_OFF:RW_GATE_OFF]
    w_in = vec_ref[0:1, :] + _dot(jnp.tanh(lora).astype(BF16), wup_ref[...])
    sp = jnp.maximum(-w_in, 0.0) + jnp.log(1.0 + jnp.exp(-jnp.abs(w_in)))
    lw = -jnp.exp(-sp - 0.5)
    a = _sigmoid(vec_ref[1:2, :] + _dot(lora.astype(BF16), aup_ref[...]))
    g = _dot(_sigmoid(u[:, RW_GATE_OFF:]).astype(BF16), gup_ref[...])
    if has_vres:
        mix = _sigmoid(vb_ref[...] + _dot(u[:, RW_VRES_OFF:].astype(BF16), vup_ref[...]))
        v = v + (vf_ref[0] - v) * mix
    kk = k * vec_ref[2:3, :]
    ss = _dot(kk * kk, bd_ref[...], precision=HI)
    kk = kk / jnp.maximum(jnp.sqrt(ss), 1e-12)
    k = k * (1.0 + (a - 1.0) * vec_ref[3:4, :])
    for val, ref in ((r, r_ref), (lw, lw_ref), (k, k_ref), (v, v_ref), (kk, kk_ref), (kk * a, kka_ref),
                     (g, g_ref)):
        ref[0] = val


def _pad_rows(w, n):
    return jnp.pad(w, ((0, n - w.shape[0]), (0, 0)))


def _rwkv_prep(p, mu, vec, w_up, a_up, g_up, v_first, vres_up, vres_b, ts):
    B, S, _ = p.shape
    C = RWKV_W
    has_vres = v_first is not None
    mu_p = jnp.pad(mu, (0, RWKV_IN_PAD - mu.shape[0])).reshape(1, RWKV_IN_PAD)
    n_lora = DECAY_LORA + ICLR_LORA
    wup = _pad_rows(w_up, n_lora).astype(BF16)
    aup = jnp.pad(a_up, ((DECAY_LORA, 0), (0, 0))).astype(BF16)
    gup = _pad_rows(g_up, RWKV_IN_PAD - RW_GATE_OFF).astype(BF16)
    bd = jnp.kron(jnp.eye(RWKV_HEADS, dtype=F32), jnp.ones((HEAD_DIM, HEAD_DIM), F32))
    const = lambda shape: pl.BlockSpec(shape, lambda b, i: (0,) * len(shape))
    dense = pl.BlockSpec((1, ts, C), lambda b, i: (b, i, 0))
    in_specs = [pl.BlockSpec((1, ts, RWKV_IN_PAD), lambda b, i: (b, i, 0)),
                pl.BlockSpec((1, 8, RWKV_IN_PAD), lambda b, i: (b, jnp.maximum(i * (ts // 8) - 1, 0), 0)),
                const((1, RWKV_IN_PAD)), const((8, C)), const((n_lora, C)), const((n_lora, C)),
                const((RWKV_IN_PAD - RW_GATE_OFF, C)), const((C, C))]
    args = [p, p, mu_p, jnp.pad(vec, ((0, 1), (0, 0))), wup, aup, gup, bd]
    if has_vres:
        off = RWKV_IN_PAD - LANE
        lo = 3 * C + n_lora + GATE_LORA - off
        vup = jnp.pad(vres_up, ((lo, LANE - lo - VRES_LORA), (0, 0))).astype(BF16)
        in_specs += [dense, const((1, C)), const((LANE, C))]
        args += [v_first, vres_b.reshape(1, C), vup]
    return pl.pallas_call(
        functools.partial(_rwkv_prep_kernel, ts=ts, has_vres=has_vres),
        grid=(B, S // ts),
        in_specs=in_specs,
        out_specs=[dense] * 7,
        out_shape=[jax.ShapeDtypeStruct((B, S, C), F32)] * 7,
        compiler_params=_params("parallel", "parallel"),
        name="rwkv_prep",
    )(*args)


GROUP_HEADS = 4
GW = GROUP_HEADS * HEAD_DIM
SR = GROUP_HEADS * CHUNK


def _split_bf16(x, n):
    parts = []
    for _ in range(n - 1):
        h = x.astype(BF16)
        parts.append(h)
        x = x - h.astype(F32)
    parts.append(x.astype(BF16))
    return parts


def _dot_split_rhs(a, x, n):
    parts = _split_bf16(x, n)
    out = _dot(a, parts[0])
    for p in parts[1:]:
        out = out + _dot(a, p)
    return out


def _dot_split_lhs(x, b, n):
    parts = _split_bf16(x, n)
    out = _dot(parts[0], b)
    for p in parts[1:]:
        out = out + _dot(p, b)
    return out


def _unit_lower_inverse(a, eye, in16, in32):
    b = lambda m: m.astype(BF16)
    d = b(jnp.where(in16, a, 0.0))
    x = eye - d
    p = b(_dot(d, d))
    x = x + _dot(b(x), p)
    p = b(_dot(p, p))
    x = x + _dot(b(x), p)
    p = b(_dot(p, p))
    x = x + _dot(b(x), p)
    for lower in (jnp.where(in32 & jnp.logical_not(in16), a, 0.0), jnp.where(in32, 0.0, a)):
        xb = b(x)
        x = x - _dot(b(_dot(xb, b(lower))), xb)
    return x


def _rwkv_scan_kernel(r_ref, lw_ref, k_ref, v_ref, kk_ref, kka_ref, g_ref, gn_ref, o_ref, state,
                      *, tt):
    @pl.when(pl.program_id(1) == 0)
    def _():
        state[...] = jnp.zeros_like(state)

    ri = lax.broadcasted_iota(jnp.int32, (SR, GW), 0)
    ci = lax.broadcasted_iota(jnp.int32, (SR, GW), 1)
    same_head = (ri // CHUNK) == (ci // HEAD_DIM)
    strict = (ci % CHUNK) < (ri % CHUNK)
    incl = (ci % CHUNK) <= (ri % CHUNK)
    in16 = (ri // 16) == (ci // 16)
    in32 = (ri // 32) == (ci // 32)
    eye = jnp.where(ri == ci, 1.0, 0.0)
    head_ones = jnp.where(same_head, 1.0, 0.0).astype(BF16)
    tr = lax.broadcasted_iota(jnp.int32, (CHUNK, CHUNK), 0)
    tc = lax.broadcasted_iota(jnp.int32, (CHUNK, CHUNK), 1)
    tri = jnp.where(tc <= tr, 1.0, 0.0).astype(BF16)
    b = lambda m: m.astype(BF16)

    def stack(x):
        return b(jnp.where(same_head, jnp.concatenate([x] * GROUP_HEADS, axis=0), 0.0))

    def chunk(c, carry):
        rows = pl.ds(pl.multiple_of(c * CHUNK, CHUNK), CHUNK)
        for gi in range(RWKV_HEADS // GROUP_HEADS):
            lanes = slice(gi * GW, (gi + 1) * GW)
            r = r_ref[0, rows, lanes]
            lw = lw_ref[0, rows, lanes]
            k = k_ref[0, rows, lanes]
            v = v_ref[0, rows, lanes]
            kk = kk_ref[0, rows, lanes]
            kka = kka_ref[0, rows, lanes]
            cs = _dot_split_rhs(tri, lw, 3)
            g_in = jnp.exp(cs)
            g_inv = jnp.exp(-cs)
            kap = stack(kk * jnp.exp(cs - lw))
            bt = stack(kka * g_inv)
            kt = stack(k * g_inv)
            rt = stack(r * g_in)
            vs = stack(v)
            a_kb = jnp.where(strict, _dot_nt(kap, bt), 0.0)
            a_kk = b(jnp.where(strict, _dot_nt(kap, kt), 0.0))
            a_rk = b(jnp.where(incl, _dot_nt(rt, kt), 0.0))
            a_rb = b(jnp.where(incl, _dot_nt(rt, bt), 0.0))
            t_inv = _unit_lower_inverse(a_kb, eye, in16, in32)
            s0 = state[gi]
            s0b = b(s0)
            rhs = _dot_nt(kap, s0b) + _dot(a_kk, vs)
            pb = b(_dot(b(t_inv), b(rhs)))
            y = _dot_nt(rt, s0b) + _dot(a_rk, vs) - _dot(a_rb, pb)
            state[gi] = (s0 + _dot_tn(vs, kt) - _dot_tn(pb, bt)) * g_in[CHUNK - 1:CHUNK, :]
            yd = y[0:CHUNK]
            for h in range(1, GROUP_HEADS):
                yd = yd + y[h * CHUNK:(h + 1) * CHUNK]
            ym = _dot_split_lhs(yd, head_ones, 2) * (1.0 / HEAD_DIM)
            yc = yd - ym
            yv = _dot_split_lhs(yc * yc, head_ones, 2) * (1.0 / HEAD_DIM)
            yn = yc * lax.rsqrt(yv + GN_EPS) * gn_ref[0:1, lanes] + gn_ref[1:2, lanes]
            bonus = _dot_split_lhs(r * k * gn_ref[2:3, lanes], head_ones, 2) * v
            o_ref[0, rows, lanes] = b((yn + bonus) * g_ref[0, rows, lanes])
        return carry

    lax.fori_loop(0, tt // CHUNK, chunk, 0)


def _rwkv_scan(r, lw, k, v, kk, kka, g, vec, tt):
    B, S, C = r.shape
    gn = jnp.pad(jnp.stack([vec[5], vec[6], vec[4]]), ((0, 5), (0, 0)))
    dense = pl.BlockSpec((1, tt, C), lambda b, i: (b, i, 0))
    return pl.pallas_call(
        functools.partial(_rwkv_scan_kernel, tt=tt),
        grid=(B, S // tt),
        in_specs=[dense] * 7 + [pl.BlockSpec((8, C), lambda b, i: (0, 0))],
        out_specs=dense,
        out_shape=jax.ShapeDtypeStruct((B, S, C), BF16),
        scratch_shapes=[pltpu.VMEM((C // GW, GW, GW), F32)],
        compiler_params=_params("parallel", "arbitrary"),
        name="rwkv_scan",
    )(r, lw, k, v, kk, kka, g, gn)


def _merge_kernel(x_ref, an_ref, ac_ref, ar_ref, gl_ref, wn_ref, wc_ref, wr_ref, wo_ref, g2_ref,
                  xo_ref, h_ref):
    D = x_ref.shape[1]
    o_n = _dot(an_ref[...].astype(BF16), wn_ref[...])
    mixed = _sigmoid(gl_ref[:, 0:D]) * o_n
    o_c = _dot(ac_ref[...], wc_ref[...])
    mixed = mixed + _sigmoid(gl_ref[:, D:2 * D]) * o_c
    o_r = _dot(ar_ref[...], wr_ref[...])
    mixed = mixed + _sigmoid(gl_ref[:, 2 * D:3 * D]) * o_r
    x = x_ref[...] + _dot(mixed.astype(BF16), wo_ref[...])
    xo_ref[...] = x
    ms = jnp.mean(x * x, axis=-1, keepdims=True)
    h_ref[...] = (x * lax.rsqrt(ms + RMS_EPS) * g2_ref[...]).astype(BF16)


def _merge(x, a_nsa, a_conv, a_rwkv, gl, w_n, w_c, w_r, w_o, g2, tm):
    T, D = x.shape
    row = lambda w: pl.BlockSpec((tm, w), lambda i: (i, 0))
    const = lambda shape: pl.BlockSpec(shape, lambda i: (0,) * len(shape))
    return pl.pallas_call(
        _merge_kernel,
        grid=(T // tm,),
        in_specs=[row(D), row(a_nsa.shape[1]), row(a_conv.shape[1]), row(a_rwkv.shape[1]), row(3 * D),
                  const(w_n.shape), const(w_c.shape), const(w_r.shape), const(w_o.shape), const((1, D))],
        out_specs=[row(D), row(D)],
        out_shape=[jax.ShapeDtypeStruct((T, D), F32), jax.ShapeDtypeStruct((T, D), BF16)],
        compiler_params=_params("parallel"),
        name="merge",
    )(x, a_nsa, a_conv, a_rwkv, gl, w_n.astype(BF16), w_c.astype(BF16), w_r.astype(BF16),
      w_o.astype(BF16), g2.reshape(1, D))


def _ffn_kernel(x_ref, h_ref, wg_ref, wu_ref, wd_ref, o_ref, acc):
    j = pl.program_id(1)

    @pl.when(j == 0)
    def _():
        acc[...] = x_ref[...]

    h = h_ref[...]
    gate = _dot(h, wg_ref[...])
    up = _dot(h, wu_ref[...])
    act = (gate * _sigmoid(gate) * up).astype(BF16)
    acc[...] += _dot(act, wd_ref[...])

    @pl.when(j == pl.num_programs(1) - 1)
    def _():
        o_ref[...] = acc[...]


def _ffn(x, h, w_gu, w_down, tm, tf):
    T, D = x.shape
    FF = w_down.shape[0]
    wg = w_gu[:, :FF].astype(BF16)
    wu = w_gu[:, FF:].astype(BF16)
    return pl.pallas_call(
        _ffn_kernel,
        grid=(T // tm, FF // tf),
        in_specs=[pl.BlockSpec((tm, D), lambda i, j: (i, 0)),
                  pl.BlockSpec((tm, D), lambda i, j: (i, 0)),
                  pl.BlockSpec((D, tf), lambda i, j: (0, j)),
                  pl.BlockSpec((D, tf), lambda i, j: (0, j)),
                  pl.BlockSpec((tf, D), lambda i, j: (j, 0))],
        out_specs=pl.BlockSpec((tm, D), lambda i, j: (i, 0)),
        out_shape=jax.ShapeDtypeStruct((T, D), F32),
        scratch_shapes=[pltpu.VMEM((tm, D), F32)],
        compiler_params=_params("parallel", "arbitrary"),
        name="ffn_dense",
    )(x, h, wg, wu, w_down.astype(BF16))


def _router_kernel(h_ref, w_ref, b_ref, gw_ref):
    logits = _dot(h_ref[...], w_ref[...]) + b_ref[...]
    lane = lax.broadcasted_iota(jnp.int32, logits.shape, 1)
    m1 = jnp.max(logits, axis=-1, keepdims=True)
    i1 = jnp.min(jnp.where(logits == m1, lane, LANE), axis=-1, keepdims=True)
    rest = jnp.where(lane == i1, -jnp.inf, logits)
    m2 = jnp.max(rest, axis=-1, keepdims=True)
    i2 = jnp.min(jnp.where(rest == m2, lane, LANE), axis=-1, keepdims=True)
    e2 = jnp.exp(m2 - m1)
    w1 = 1.0 / (1.0 + e2)
    w2 = e2 / (1.0 + e2)
    gw_ref[...] = jnp.where(lane == i1, w1, jnp.where(lane == i2, w2, 0.0))


def _router(h, router_w, router_b, tm):
    T, D = h.shape
    w = _pad_cols(router_w, LANE).astype(BF16)
    b = jnp.pad(router_b, (0, LANE - N_EXPERTS), constant_values=-jnp.inf).reshape(1, LANE)
    return pl.pallas_call(
        _router_kernel,
        grid=(T // tm,),
        in_specs=[pl.BlockSpec((tm, D), lambda i: (i, 0)),
                  pl.BlockSpec((D, LANE), lambda i: (0, 0)),
                  pl.BlockSpec((1, LANE), lambda i: (0, 0))],
        out_specs=pl.BlockSpec((tm, LANE), lambda i: (i, 0)),
        out_shape=jax.ShapeDtypeStruct((T, LANE), F32),
        compiler_params=_params("parallel"),
        name="moe_router",
    )(h, w, b)


def _moe_kernel(x_ref, h_ref, gw_ref, wg_ref, wu_ref, wd_ref, o_ref, acc):
    e = pl.program_id(1)
    j = pl.program_id(2)

    @pl.when((e == 0) & (j == 0))
    def _():
        acc[...] = x_ref[...]

    gw = gw_ref[...]
    lane = lax.broadcasted_iota(jnp.int32, gw.shape, 1)
    w_e = jnp.sum(jnp.where(lane == e, gw, 0.0), axis=-1, keepdims=True)
    h = h_ref[...]
    gate = _dot(h, wg_ref[0])
    up = _dot(h, wu_ref[0])
    act = (gate * _sigmoid(gate) * up).astype(BF16)
    acc[...] += w_e * _dot(act, wd_ref[0])

    @pl.when((e == pl.num_programs(1) - 1) & (j == pl.num_programs(2) - 1))
    def _():
        o_ref[...] = acc[...]


def _moe(x, h, gw, w_gu, w_down, tm, tf):
    T, D = x.shape
    E, FF, _ = w_down.shape
    wg = w_gu[:, :, :FF].astype(BF16)
    wu = w_gu[:, :, FF:].astype(BF16)
    return pl.pallas_call(
        _moe_kernel,
        grid=(T // tm, E, FF // tf),
        in_specs=[pl.BlockSpec((tm, D), lambda i, e, j: (i, 0)),
                  pl.BlockSpec((tm, D), lambda i, e, j: (i, 0)),
                  pl.BlockSpec((tm, LANE), lambda i, e, j: (i, 0)),
                  pl.BlockSpec((1, D, tf), lambda i, e, j: (e, 0, j)),
                  pl.BlockSpec((1, D, tf), lambda i, e, j: (e, 0, j)),
                  pl.BlockSpec((1, tf, D), lambda i, e, j: (e, j, 0))],
        out_specs=pl.BlockSpec((tm, D), lambda i, e, j: (i, 0)),
        out_shape=jax.ShapeDtypeStruct((T, D), F32),
        scratch_shapes=[pltpu.VMEM((tm, D), F32)],
        compiler_params=_params("parallel", "arbitrary", "arbitrary"),
        name="moe_experts",
    )(x, h, gw, wg, wu, w_down.astype(BF16))


def _pad_cols(w, n):
    return jnp.pad(w, ((0, 0), (0, n - w.shape[1])))


def kernel(x, mix_norm_g, w_in_first, w_in_rest, rwkv_mu_first, rwkv_mu_rest, nsa_qk_g, nsa_cmp_pos, nsa_cmp_w1, nsa_cmp_w2, nsa_w_o, conv_w, conv_vec, conv_w_o, rwkv_vec, rwkv_w_up, rwkv_a_up, rwkv_g_up, rwkv_vres_up, rwkv_vres_b, rwkv_w_o, w_out, ffn_norm_g, dense_w_gu, dense_w_down, router_w, router_b, moe_w_gu, moe_w_down):
    B, S, D = x.shape
    T = B * S
    depth = mix_norm_g.shape[0]
    off_conv = NSA_IN
    off_gate = off_conv + 2 * CONV_CH
    off_rwkv = off_gate + 3 * D
    tm = min(512, T)
    xt = x.reshape(T, D)
    v_first = None
    for layer in range(depth):
        if layer == 0:
            w_in, mu, vres_up, vres_b = w_in_first, rwkv_mu_first, None, None
        else:
            w_in, mu = w_in_rest[layer - 1], rwkv_mu_rest[layer - 1]
            vres_up, vres_b = rwkv_vres_up[layer - 1], rwkv_vres_b[layer - 1]
        g1 = mix_norm_g[layer]
        w_nsa = _pad_cols(w_in[:, :off_conv], NSA_IN_PAD).astype(BF16)
        w_conv = w_in[:, off_conv:off_gate].astype(BF16)
        w_gate = w_in[:, off_gate:off_rwkv].astype(BF16)
        w_rwkv = _pad_cols(w_in[:, off_rwkv:], RWKV_IN_PAD).astype(BF16)
        p_nsa = _rms_mm(xt, g1, w_nsa, tm, NSA_IN_PAD).reshape(B, S, NSA_IN_PAD)
        p_conv = _rms_mm(xt, g1, w_conv, tm, 2 * CONV_CH).reshape(B, S, 2 * CONV_CH)
        p_gate = _rms_mm(xt, g1, w_gate, tm, D)
        p_rwkv = _rms_mm(xt, g1, w_rwkv, tm, RWKV_IN_PAD).reshape(B, S, RWKV_IN_PAD)

        a_nsa = _nsa_branch(p_nsa, nsa_qk_g[layer], nsa_cmp_pos[layer], nsa_cmp_w1[layer],
                            nsa_cmp_w2[layer], B, S)
        a_conv = _conformer(p_conv, conv_w[layer], conv_vec[layer], ts=min(512, S))
        r, lw, k, v, kk, kka, g = _rwkv_prep(
            p_rwkv, mu, rwkv_vec[layer], rwkv_w_up[layer], rwkv_a_up[layer], rwkv_g_up[layer],
            v_first, vres_up, vres_b, ts=min(256, S))
        if layer == 0:
            v_first = v
        a_rwkv = _rwkv_scan(r, lw, k, v, kk, kka, g, rwkv_vec[layer], tt=min(512, S))

        xt, h2 = _merge(xt, a_nsa.reshape(T, NSA_Q_W), a_conv.reshape(T, CONV_CH),
                        a_rwkv.reshape(T, RWKV_W), p_gate, nsa_w_o[layer], conv_w_o[layer],
                        rwkv_w_o[layer], w_out[layer], ffn_norm_g[layer], tm)
        if layer % 2 == 0:
            xt = _ffn(xt, h2, dense_w_gu[layer // 2], dense_w_down[layer // 2], min(1024, T), 256)
        else:
            gw = _router(h2, router_w[layer // 2], router_b[layer // 2], tm)
            xt = _moe(xt, h2, gw, moe_w_gu[layer // 2], moe_w_down[layer // 2], min(1024, T), 512)
    return xt.reshape(B, S, D)
```

```python
import functools

import jax
import jax.numpy as jnp
from jax import lax
from jax.experimental import pallas as pl
from jax.experimental.pallas import tpu as pltpu

F32 = jnp.float32
BF16 = jnp.bfloat16
HI = lax.Precision.HIGHEST

NSA_HEADS = 8
NSA_KV_HEADS = 2
NSA_GROUP = NSA_HEADS // NSA_KV_HEADS
HEAD_DIM = 64
NSA_Q_W = NSA_HEADS * HEAD_DIM
NSA_KV_W = NSA_KV_HEADS * HEAD_DIM
CMP_STRIDE = 16
CMP_LEN = 32
SEL_BLOCK = 64
N_SEL = 8
WINDOW = 512
SEL_FORCE = 1e4
CONV_CH = 512
CONV_K = 31
RWKV_HEADS = 8
RWKV_W = RWKV_HEADS * HEAD_DIM
DECAY_LORA = 64
ICLR_LORA = 64
GATE_LORA = 160
VRES_LORA = 32
N_EXPERTS = 8
ROPE_THETA = 10000.0
RMS_EPS = 1e-6
LN_EPS = 1e-5
GN_EPS = 64e-5
NEG_INF = -1e30
LOG2E = 1.4426950408889634

NSA_IN = NSA_Q_W + 6 * NSA_KV_W + 3 * NSA_HEADS
NSA_IN_PAD = 1408
NSA_GATE_BLK = (NSA_Q_W + 6 * NSA_KV_W) // 128
RWKV_IN_PAD = 1920
LANE = 128
VMEM_LIMIT = 56 * 1024 * 1024

CHUNK = 64


def _dot(a, b, precision=None):
    return jnp.dot(a, b, preferred_element_type=F32, precision=precision)


def _dot_nt(a, b, precision=None):
    return lax.dot_general(a, b, (((1,), (1,)), ((), ())), preferred_element_type=F32,
                           precision=precision)


def _dot_tn(a, b, precision=None):
    return lax.dot_general(a, b, (((0,), (0,)), ((), ())), preferred_element_type=F32,
                           precision=precision)


def _sigmoid(x):
    return 1.0 / (1.0 + jnp.exp(-x))


def _params(*sem):
    return pltpu.CompilerParams(dimension_semantics=sem, vmem_limit_bytes=VMEM_LIMIT)


def _rms_mm_kernel(x_ref, g_ref, w_ref, o_ref, h_scr):
    @pl.when(pl.program_id(1) == 0)
    def _():
        x = x_ref[...]
        ms = jnp.mean(x * x, axis=-1, keepdims=True)
        h_scr[...] = (x * lax.rsqrt(ms + RMS_EPS) * g_ref[...]).astype(BF16)

    o_ref[...] = _dot(h_scr[...], w_ref[...])


def _rms_mm(x, g, w, tm, tn):
    T, D = x.shape
    N = w.shape[1]
    return pl.pallas_call(
        _rms_mm_kernel,
        grid=(T // tm, N // tn),
        in_specs=[pl.BlockSpec((tm, D), lambda i, j: (i, 0)),
                  pl.BlockSpec((1, D), lambda i, j: (0, 0)),
                  pl.BlockSpec((D, tn), lambda i, j: (0, j))],
        out_specs=pl.BlockSpec((tm, tn), lambda i, j: (i, j)),
        out_shape=jax.ShapeDtypeStruct((T, N), F32),
        scratch_shapes=[pltpu.VMEM((tm, D), BF16)],
        compiler_params=_params("parallel", "arbitrary"),
        name="rms_proj",
    )(x, g.reshape(1, D), w)


def _rope_dense(x, cos, sin_signed):
    w = x.shape[1]
    lane = lax.broadcasted_iota(jnp.int32, x.shape, 1)
    first_half = (lane % HEAD_DIM) < (HEAD_DIM // 2)
    rot = jnp.where(first_half, pltpu.roll(x, w - HEAD_DIM // 2, 1), pltpu.roll(x, HEAD_DIM // 2, 1))
    return x * cos + rot * sin_signed


def _head_rms(x, bd, g):
    ms = _dot(x * x, bd, precision=HI)
    return x * lax.rsqrt(ms + RMS_EPS) * g


def _nsa_prep_kernel(p_ref, cos_ref, sin_ref, gq_ref, gks_ref, gkw_ref, bd_ref,
                     q_ref, kc_ref, vc_ref, ks_ref, vs_ref, kw_ref, vw_ref):
    cos = cos_ref[...]
    sin = sin_ref[...]
    bd = bd_ref[...]
    q = p_ref[0, :, 0:NSA_Q_W]
    q = _rope_dense(_head_rms(q, bd, gq_ref[...]), cos, sin) * (HEAD_DIM ** -0.5 * LOG2E)
    for h in range(NSA_HEADS):
        q_ref[0, h] = q[:, h * HEAD_DIM:(h + 1) * HEAD_DIM].astype(BF16)

    def slab(i):
        return p_ref[0, :, NSA_Q_W + i * NSA_KV_W:NSA_Q_W + (i + 1) * NSA_KV_W]

    cos_k = cos[:, :NSA_KV_W]
    sin_k = sin[:, :NSA_KV_W]
    bd_k = bd[:NSA_KV_W, :NSA_KV_W]
    ks = _rope_dense(_head_rms(slab(2), bd_k, gks_ref[...]), cos_k, sin_k)
    kw = _rope_dense(_head_rms(slab(4), bd_k, gkw_ref[...]), cos_k, sin_k)
    for val, ref in ((slab(0), kc_ref), (slab(1), vc_ref), (ks, ks_ref), (slab(3), vs_ref),
                     (kw, kw_ref), (slab(5), vw_ref)):
        for g in range(NSA_KV_HEADS):
            ref[0, g] = val[:, g * HEAD_DIM:(g + 1) * HEAD_DIM].astype(BF16)


def _nsa_prep(p, cos, sin, qk_g, ts):
    B, S, _ = p.shape
    bd = jnp.kron(jnp.eye(NSA_HEADS, dtype=F32), jnp.full((HEAD_DIM, HEAD_DIM), 1.0 / HEAD_DIM, F32))
    gq = jnp.tile(qk_g[0], NSA_HEADS).reshape(1, NSA_Q_W)
    gks = jnp.tile(qk_g[2], NSA_KV_HEADS).reshape(1, NSA_KV_W)
    gkw = jnp.tile(qk_g[3], NSA_KV_HEADS).reshape(1, NSA_KV_W)
    kv_shape = jax.ShapeDtypeStruct((B, NSA_KV_HEADS, S, HEAD_DIM), BF16)
    kv_spec = pl.BlockSpec((1, NSA_KV_HEADS, ts, HEAD_DIM), lambda b, i: (b, 0, i, 0))
    const = lambda shape: pl.BlockSpec(shape, lambda b, i: (0,) * len(shape))
    return pl.pallas_call(
        _nsa_prep_kernel,
        grid=(B, S // ts),
        in_specs=[pl.BlockSpec((1, ts, NSA_IN_PAD), lambda b, i: (b, i, 0)),
                  pl.BlockSpec((ts, NSA_Q_W), lambda b, i: (i, 0)),
                  pl.BlockSpec((ts, NSA_Q_W), lambda b, i: (i, 0)),
                  const((1, NSA_Q_W)), const((1, NSA_KV_W)), const((1, NSA_KV_W)),
                  const((NSA_Q_W, NSA_Q_W))],
        out_specs=[pl.BlockSpec((1, NSA_HEADS, ts, HEAD_DIM), lambda b, i: (b, 0, i, 0))] + [kv_spec] * 6,
        out_shape=[jax.ShapeDtypeStruct((B, NSA_HEADS, S, HEAD_DIM), BF16)] + [kv_shape] * 6,
        compiler_params=_params("parallel", "parallel"),
        name="nsa_prep",
    )(p, cos, sin, gq, gks, gkw, bd)


def _gelu_tanh(x):
    return 0.5 * x * (1.0 + jnp.tanh(0.7978845608028654 * (x + 0.044715 * x * x * x)))


def _compress_kernel(a_ref, posa_ref, posb_ref, w1a_ref, w1b_ref, w2_ref, g_ref, cos_ref, sin_ref,
                     perm_ref, o_ref, *, is_key):
    a = a_ref[0]
    ncp = a.shape[0]
    w1a = w1a_ref[...]
    w1b = w1b_ref[...]
    p1 = _dot(a, w1a)
    p2 = _dot(a, w1b)
    hp = _dot(posa_ref[...], w1a) + _dot(posb_ref[...], w1b)
    h = _gelu_tanh(p1 + pltpu.roll(p2, ncp - 1, 0) + hp[0:1])
    o = _dot(h.astype(BF16), w2_ref[...])
    if is_key:
        ms = jnp.mean(o * o, axis=-1, keepdims=True)
        o = o * lax.rsqrt(ms + RMS_EPS) * g_ref[...]
        o = o * cos_ref[...] + _dot(o, perm_ref[...], precision=HI) * sin_ref[...]
    row = lax.broadcasted_iota(jnp.int32, o.shape, 0)
    o_ref[0] = jnp.where(row < ncp - 1, o, 0.0).astype(BF16)


def _compress(t, pos_emb, w1, w2, g, cos_c, sin_c, is_key):
    BG, ncp, cw = t.shape
    hid = w1.shape[1]
    pos = pos_emb.reshape(1, CMP_LEN * HEAD_DIM)
    posa = jnp.tile(pos[:, :cw], (8, 1)).astype(BF16)
    posb = jnp.tile(pos[:, cw:], (8, 1)).astype(BF16)
    half = HEAD_DIM // 2
    perm = jnp.roll(jnp.eye(HEAD_DIM, dtype=F32), half, axis=0)
    const = lambda shape: pl.BlockSpec(shape, lambda b: (0,) * len(shape))
    return pl.pallas_call(
        functools.partial(_compress_kernel, is_key=is_key),
        grid=(BG,),
        in_specs=[pl.BlockSpec((1, ncp, cw), lambda b: (b, 0, 0)),
                  const((8, cw)), const((8, cw)), const((cw, hid)), const((cw, hid)),
                  const((hid, HEAD_DIM)), const((1, HEAD_DIM)),
                  const((ncp, HEAD_DIM)), const((ncp, HEAD_DIM)), const((HEAD_DIM, HEAD_DIM))],
        out_specs=pl.BlockSpec((1, ncp, HEAD_DIM), lambda b: (b, 0, 0)),
        out_shape=jax.ShapeDtypeStruct((BG, ncp, HEAD_DIM), BF16),
        compiler_params=_params("parallel"),
        name="nsa_compress_k" if is_key else "nsa_compress_v",
    )(t, posa, posb, w1[:cw].astype(BF16), w1[cw:].astype(BF16), w2.astype(BF16),
      g.reshape(1, HEAD_DIM), cos_c, sin_c, perm)


def _exp_cols(s):
    m = jnp.max(s, axis=0, keepdims=True)
    e = jnp.exp2(s - m)
    return e, jnp.sum(e, axis=0, keepdims=True)


def _heads_to_rows(o, tq):
    return jnp.concatenate([o[:, h * tq:(h + 1) * tq] for h in range(NSA_GROUP)], axis=0)


def _nsa_attn_kernel(q_ref, kc_ref, vc_ref, ks_ref, vs_ref, kw_ref, vw_ref, gl_ref, o_ref, selb_ref, s_ref,
                     *, tq, tk, seq):
    g = pl.program_id(1)
    i = pl.program_id(2)
    rows = NSA_GROUP * tq
    ncp = seq // CMP_STRIDE
    ns = seq // SEL_BLOCK
    n_sel = min(N_SEL, ns)
    q = q_ref[0].reshape(rows, HEAD_DIM)
    t0 = i * tq
    t_q = t0 + lax.broadcasted_iota(jnp.int32, (1, tq), 1)

    def per_head(x):
        return jnp.concatenate([x] * NSA_GROUP, axis=1)

    c_end = lax.broadcasted_iota(jnp.int32, (ncp, 1), 0) * CMP_STRIDE + (CMP_LEN - 1)
    bias = jnp.where(c_end <= t_q, 0.0, NEG_INF)
    e, l = _exp_cols(_dot_nt(kc_ref[0], q) + per_head(bias))
    p_c = e * (per_head(jnp.where(t_q >= CMP_LEN - 1, 1.0, 0.0)) / l)
    o_c = _dot_tn(vc_ref[0], p_c.astype(BF16))

    p_sum = p_c[:, 0:tq]
    for h in range(1, NSA_GROUP):
        p_sum = p_sum + p_c[:, h * tq:(h + 1) * tq]
    s0 = lax.broadcasted_iota(jnp.int32, (ns, ncp), 0) * SEL_BLOCK
    c0 = lax.broadcasted_iota(jnp.int32, (ns, ncp), 1) * CMP_STRIDE
    overlap = jnp.where((c0 < s0 + SEL_BLOCK) & (c0 + CMP_LEN > s0), 1.0, 0.0).astype(BF16)
    imp = _dot_split_rhs(overlap, p_sum, 2)
    j = lax.broadcasted_iota(jnp.int32, (ns, tq), 0).astype(F32)
    cur = (t_q // SEL_BLOCK).astype(F32)
    forced = (j == 0.0) | (j == cur) | (j == cur - 1.0)
    score = jnp.where(forced, SEL_FORCE, jnp.where(j <= cur, imp, -1.0))
    for _ in range(n_sel):
        mx = jnp.max(score, axis=0, keepdims=True)
        first = jnp.min(jnp.where(score == mx, j, float(ns)), axis=0, keepdims=True)
        score = jnp.where(j == first, -jnp.inf, score)
    selb_ref[...] = jnp.where(score == -jnp.inf, 0.0, NEG_INF)

    wlen = WINDOW + tq
    w0 = pl.multiple_of(jnp.maximum(t0 - WINDOW, 0), tq)
    kpos = w0 + lax.broadcasted_iota(jnp.int32, (wlen, 1), 0)
    bias = jnp.where((kpos <= t_q) & (kpos > t_q - WINDOW), 0.0, NEG_INF)
    e, l = _exp_cols(_dot_nt(kw_ref[0, 0, pl.ds(w0, wlen), :], q) + per_head(bias))
    o_w = _dot_tn(vw_ref[0, 0, pl.ds(w0, wlen), :], e.astype(BF16)) / l

    key_off = lax.broadcasted_iota(jnp.int32, (SEL_BLOCK, 1), 0)

    last_tile = seq // tk - 1

    def scores(jt, slot):
        jt = jnp.minimum(jt, last_tile)
        k0 = pl.multiple_of(jt * tk, tk)
        bias = []
        for b in range(tk // SEL_BLOCK):
            picked = selb_ref[pl.ds(jt * (tk // SEL_BLOCK) + b, 1), :]
            causal = k0 + b * SEL_BLOCK + key_off <= t_q
            bias.append(jnp.where(causal, picked, NEG_INF))
        bias = jnp.concatenate(bias, axis=0)
        s_ref[slot] = _dot_nt(ks_ref[0, 0, pl.ds(k0, tk), :], q) + per_head(bias)

    def absorb(jt, slot, carry):
        m_run, l_run, acc = carry
        k0 = pl.multiple_of(jt * tk, tk)
        s = s_ref[slot]
        m_new = jnp.maximum(m_run, jnp.max(s, axis=0, keepdims=True))
        e = jnp.exp2(s - m_new)
        alpha = jnp.exp2(m_run - m_new)
        l_new = alpha * l_run + jnp.sum(e, axis=0, keepdims=True)
        acc = alpha * acc + _dot_tn(vs_ref[0, 0, pl.ds(k0, tk), :], e.astype(BF16))
        return m_new, l_new, acc

    def sel_pair(it, carry):
        scores(2 * it + 1, 1)
        carry = absorb(2 * it, 0, carry)
        scores(2 * it + 2, 0)
        return absorb(2 * it + 1, 1, carry)

    n_tiles = (t0 + tq - 1) // tk + 1
    init = (jnp.full((1, rows), NEG_INF, F32), jnp.zeros((1, rows), F32), jnp.zeros((HEAD_DIM, rows), F32))
    scores(0, 0)
    _, l_s, acc_s = lax.fori_loop(0, (n_tiles + 1) // 2, sel_pair, init)
    o_s = acc_s / l_s

    gw = NSA_GROUP * HEAD_DIM
    n = lax.broadcasted_iota(jnp.int32, (3 * gw, LANE), 0)
    col = lax.broadcasted_iota(jnp.int32, (3 * gw, LANE), 1)
    head = g * NSA_GROUP + (n % gw) // HEAD_DIM
    spread = jnp.where(col == head * 3 + n // gw, 1.0, 0.0).astype(BF16)
    gates = None
    for piece in _split_bf16(_sigmoid(gl_ref[0]), 3):
        part = _dot_nt(spread, piece)
        gates = part if gates is None else gates + part
    out = gates[0:gw] * _heads_to_rows(o_c, tq)
    out = out + gates[gw:2 * gw] * _heads_to_rows(o_s, tq)
    out = out + gates[2 * gw:] * _heads_to_rows(o_w, tq)
    o_ref[0] = out.T


def _nsa_attn(q, kc, vc, ks, vs, kw, vw, p_nsa, tq, tk):
    B, H, S, _ = q.shape
    G = NSA_KV_HEADS
    ncp = S // CMP_STRIDE
    assert S % (2 * tk) == 0
    cmp_spec = pl.BlockSpec((1, ncp, HEAD_DIM), lambda b, g, i: (b * G + g, 0, 0))
    seq_spec = pl.BlockSpec((1, 1, S, HEAD_DIM), lambda b, g, i: (b, g, 0, 0))
    return pl.pallas_call(
        functools.partial(_nsa_attn_kernel, tq=tq, tk=tk, seq=S),
        grid=(B, G, S // tq),
        in_specs=[pl.BlockSpec((1, NSA_GROUP, tq, HEAD_DIM), lambda b, g, i: (b, g, i, 0)),
                  cmp_spec, cmp_spec, seq_spec, seq_spec, seq_spec, seq_spec,
                  pl.BlockSpec((1, tq, LANE), lambda b, g, i: (b, i, NSA_GATE_BLK))],
        out_specs=pl.BlockSpec((1, tq, NSA_GROUP * HEAD_DIM), lambda b, g, i: (b, i, g)),
        out_shape=jax.ShapeDtypeStruct((B, S, NSA_Q_W), F32),
        scratch_shapes=[pltpu.VMEM((S // SEL_BLOCK, tq), F32),
                        pltpu.VMEM((2, tk, NSA_GROUP * tq), F32)],
        compiler_params=_params("parallel", "parallel", "arbitrary"),
        name="nsa_attn",
    )(q, kc, vc, ks, vs, kw, vw, p_nsa)


def _rope_tables(S):
    half = HEAD_DIM // 2
    inv = ROPE_THETA ** (-jnp.arange(half, dtype=F32) / half)
    ang = jnp.arange(S, dtype=F32)[:, None] * inv[None, :]
    cos, sin = jnp.cos(ang), jnp.sin(ang)
    cos64 = jnp.concatenate([cos, cos], axis=1)
    sin64 = jnp.concatenate([-sin, sin], axis=1)
    return cos64, sin64


def _nsa_branch(p_nsa, qk_g, cmp_pos, cmp_w1, cmp_w2, B, S):
    cos64, sin64 = _rope_tables(S)
    cos = jnp.tile(cos64, (1, NSA_HEADS))
    sin = jnp.tile(sin64, (1, NSA_HEADS))
    q, kc, vc, ks, vs, kw, vw = _nsa_prep(p_nsa, cos, sin, qk_g, ts=min(256, S))
    ncp = S // CMP_STRIDE
    cmp_end = jnp.minimum(jnp.arange(ncp) * CMP_STRIDE + CMP_LEN - 1, S - 1)
    cw = CMP_STRIDE * HEAD_DIM
    kc = kc.reshape(B * NSA_KV_HEADS, ncp, cw)
    vc = vc.reshape(B * NSA_KV_HEADS, ncp, cw)
    k_cmp = _compress(kc, cmp_pos[0], cmp_w1[0], cmp_w2[0], qk_g[1], cos64[cmp_end], sin64[cmp_end], True)
    v_cmp = _compress(vc, cmp_pos[1], cmp_w1[1], cmp_w2[1], qk_g[1], cos64[cmp_end], sin64[cmp_end], False)
    return _nsa_attn(q, k_cmp, v_cmp, ks, vs, kw, vw, p_nsa, tq=128, tk=256)


CONV_HALO = 32


def _conv_kernel(cur_ref, halo_ref, w_ref, vec_ref, o_ref, buf, y_scr, *, ts):
    i = pl.program_id(1)

    def glu(p):
        return p[:, :CONV_CH] * _sigmoid(p[:, CONV_CH:])

    buf[0:CONV_HALO, :] = jnp.where(i > 0, glu(halo_ref[0]), 0.0)
    buf[CONV_HALO:, :] = glu(cur_ref[0])
    rt = min(ts, 128)
    shift = CONV_HALO - (CONV_K - 1)
    for c in range(CONV_CH // LANE):
        cs = slice(c * LANE, (c + 1) * LANE)
        for r in range(ts // rt):
            acc = jnp.zeros((rt, LANE), F32)
            for j in range(CONV_K):
                acc = acc + w_ref[j:j + 1, cs] * buf[r * rt + shift + j:r * rt + shift + j + rt, cs]
            y_scr[r * rt:(r + 1) * rt, cs] = acc
    y = y_scr[...] + vec_ref[0:1, :]
    mu = jnp.mean(y, axis=-1, keepdims=True)
    yc = y - mu
    var = jnp.mean(yc * yc, axis=-1, keepdims=True)
    y = yc * lax.rsqrt(var + LN_EPS) * vec_ref[1:2, :] + vec_ref[2:3, :]
    o_ref[0] = (y * _sigmoid(y)).astype(BF16)


def _conformer(p, conv_w, conv_vec, ts):
    B, S, _ = p.shape
    w = jnp.pad(conv_w, ((0, 32 - CONV_K), (0, 0)))
    vec = jnp.pad(conv_vec, ((0, 5), (0, 0)))
    hb = ts // CONV_HALO
    return pl.pallas_call(
        functools.partial(_conv_kernel, ts=ts),
        grid=(B, S // ts),
        in_specs=[pl.BlockSpec((1, ts, 2 * CONV_CH), lambda b, i: (b, i, 0)),
                  pl.BlockSpec((1, CONV_HALO, 2 * CONV_CH), lambda b, i: (b, jnp.maximum(i * hb - 1, 0), 0)),
                  pl.BlockSpec((32, CONV_CH), lambda b, i: (0, 0)),
                  pl.BlockSpec((8, CONV_CH), lambda b, i: (0, 0))],
        out_specs=pl.BlockSpec((1, ts, CONV_CH), lambda b, i: (b, i, 0)),
        out_shape=jax.ShapeDtypeStruct((B, S, CONV_CH), BF16),
        scratch_shapes=[pltpu.VMEM((CONV_HALO + ts, CONV_CH), F32), pltpu.VMEM((ts, CONV_CH), F32)],
        compiler_params=_params("parallel", "parallel"),
        name="conformer",
    )(p, p, w, vec)


RW_LORA_OFF = 3 * RWKV_W
RW_GATE_OFF = RW_LORA_OFF + DECAY_LORA + ICLR_LORA
RW_VRES_OFF = RWKV_IN_PAD - LANE


def _rwkv_prep_kernel(*refs, ts, has_vres):
    if has_vres:
        (cur_ref, halo_ref, mu_ref, vec_ref, wup_ref, aup_ref, gup_ref, bd_ref, vf_ref, vb_ref, vup_ref,
         r_ref, lw_ref, k_ref, v_ref, kk_ref, kka_ref, g_ref) = refs
    else:
        (cur_ref, halo_ref, mu_ref, vec_ref, wup_ref, aup_ref, gup_ref, bd_ref,
         r_ref, lw_ref, k_ref, v_ref, kk_ref, kka_ref, g_ref) = refs
    i = pl.program_id(1)
    p = cur_ref[0]
    last = jnp.where(i > 0, halo_ref[0, 7:8, :], 0.0)
    row = lax.broadcasted_iota(jnp.int32, p.shape, 0)
    prev = jnp.where(row == 0, last, pltpu.roll(p, 1, 0))
    u = p + (prev - p) * mu_ref[...]
    C = RWKV_W
    r, k, v = u[:, :C], u[:, C:2 * C], u[:, 2 * C:3 * C]
    lora = u[:, RW_LORA_OFF:RW_GATE_OFF]
    w_in = vec_ref[0:1, :] + _dot(jnp.tanh(lora).astype(BF16), wup_ref[...])
    sp = jnp.maximum(-w_in, 0.0) + jnp.log(1.0 + jnp.exp(-jnp.abs(w_in)))
    lw = -jnp.exp(-sp - 0.5)
    a = _sigmoid(vec_ref[1:2, :] + _dot(lora.astype(BF16), aup_ref[...]))
    g = _dot(_sigmoid(u[:, RW_GATE_OFF:]).astype(BF16), gup_ref[...])
    if has_vres:
        mix = _sigmoid(vb_ref[...] + _dot(u[:, RW_VRES_OFF:].astype(BF16), vup_ref[...]))
        v = v + (vf_ref[0] - v) * mix
    kk = k * vec_ref[2:3, :]
    ss = _dot(kk * kk, bd_ref[...], precision=HI)
    kk = kk / jnp.maximum(jnp.sqrt(ss), 1e-12)
    k = k * (1.0 + (a - 1.0) * vec_ref[3:4, :])
    for val, ref in ((r, r_ref), (lw, lw_ref), (k, k_ref), (v, v_ref), (kk, kk_ref), (kk * a, kka_ref),
                     (g, g_ref)):
        ref[0] = val


def _pad_rows(w, n):
    return jnp.pad(w, ((0, n - w.shape[0]), (0, 0)))


def _rwkv_prep(p, mu, vec, w_up, a_up, g_up, v_first, vres_up, vres_b, ts):
    B, S, _ = p.shape
    C = RWKV_W
    has_vres = v_first is not None
    mu_p = jnp.pad(mu, (0, RWKV_IN_PAD - mu.shape[0])).reshape(1, RWKV_IN_PAD)
    n_lora = DECAY_LORA + ICLR_LORA
    wup = _pad_rows(w_up, n_lora).astype(BF16)
    aup = jnp.pad(a_up, ((DECAY_LORA, 0), (0, 0))).astype(BF16)
    gup = _pad_rows(g_up, RWKV_IN_PAD - RW_GATE_OFF).astype(BF16)
    bd = jnp.kron(jnp.eye(RWKV_HEADS, dtype=F32), jnp.ones((HEAD_DIM, HEAD_DIM), F32))
    const = lambda shape: pl.BlockSpec(shape, lambda b, i: (0,) * len(shape))
    dense = pl.BlockSpec((1, ts, C), lambda b, i: (b, i, 0))
    in_specs = [pl.BlockSpec((1, ts, RWKV_IN_PAD), lambda b, i: (b, i, 0)),
                pl.BlockSpec((1, 8, RWKV_IN_PAD), lambda b, i: (b, jnp.maximum(i * (ts // 8) - 1, 0), 0)),
                const((1, RWKV_IN_PAD)), const((8, C)), const((n_lora, C)), const((n_lora, C)),
                const((RWKV_IN_PAD - RW_GATE_OFF, C)), const((C, C))]
    args = [p, p, mu_p, jnp.pad(vec, ((0, 1), (0, 0))), wup, aup, gup, bd]
    if has_vres:
        off = RWKV_IN_PAD - LANE
        lo = 3 * C + n_lora + GATE_LORA - off
        vup = jnp.pad(vres_up, ((lo, LANE - lo - VRES_LORA), (0, 0))).astype(BF16)
        in_specs += [dense, const((1, C)), const((LANE, C))]
        args += [v_first, vres_b.reshape(1, C), vup]
    return pl.pallas_call(
        functools.partial(_rwkv_prep_kernel, ts=ts, has_vres=has_vres),
        grid=(B, S // ts),
        in_specs=in_specs,
        out_specs=[dense] * 7,
        out_shape=[jax.ShapeDtypeStruct((B, S, C), F32)] * 7,
        compiler_params=_params("parallel", "parallel"),
        name="rwkv_prep",
    )(*args)


GROUP_HEADS = 4
GW = GROUP_HEADS * HEAD_DIM
SR = GROUP_HEADS * CHUNK


def _split_bf16(x, n):
    parts = []
    for _ in range(n - 1):
        h = x.astype(BF16)
        parts.append(h)
        x = x - h.astype(F32)
    parts.append(x.astype(BF16))
    return parts


def _dot_split_rhs(a, x, n):
    parts = _split_bf16(x, n)
    out = _dot(a, parts[0])
    for p in parts[1:]:
        out = out + _dot(a, p)
    return out


def _dot_split_lhs(x, b, n):
    parts = _split_bf16(x, n)
    out = _dot(parts[0], b)
    for p in parts[1:]:
        out = out + _dot(p, b)
    return out


def _unit_lower_inverses(mats, eye, in16, in32):
    b = lambda m: m.astype(BF16)
    ds = [b(jnp.where(in16, a, 0.0)) for a in mats]
    xs = [eye - d for d in ds]
    ps = [b(_dot(d, d)) for d in ds]
    for step in range(3):
        xs = [x + _dot(b(x), p) for x, p in zip(xs, ps)]
        if step < 2:
            ps = [b(_dot(p, p)) for p in ps]
    for lowers in ([jnp.where(in32 & jnp.logical_not(in16), a, 0.0) for a in mats],
                   [jnp.where(in32, 0.0, a) for a in mats]):
        xbs = [b(x) for x in xs]
        mids = [b(_dot(xb, b(lo))) for xb, lo in zip(xbs, lowers)]
        xs = [x - _dot(mid, xb) for x, mid, xb in zip(xs, mids, xbs)]
    return xs


def _rwkv_scan_kernel(r_ref, lw_ref, k_ref, v_ref, kk_ref, kka_ref, g_ref, gn_ref, o_ref, state,
                      kap_s, bt_s, rt_s, arb_s, tinv_s, u_s, y0_s, sv_s, gl_s, y_s, *, tt):
    @pl.when(pl.program_id(1) == 0)
    def _():
        state[...] = jnp.zeros_like(state)

    ri = lax.broadcasted_iota(jnp.int32, (SR, GW), 0)
    ci = lax.broadcasted_iota(jnp.int32, (SR, GW), 1)
    same_head = (ri // CHUNK) == (ci // HEAD_DIM)
    strict = (ci % CHUNK) < (ri % CHUNK)
    incl = (ci % CHUNK) <= (ri % CHUNK)
    in16 = (ri // 16) == (ci // 16)
    in32 = (ri // 32) == (ci // 32)
    eye = jnp.where(ri == ci, 1.0, 0.0)
    head_ones = jnp.where(same_head, 1.0, 0.0).astype(BF16)
    tr = lax.broadcasted_iota(jnp.int32, (CHUNK, CHUNK), 0)
    tc = lax.broadcasted_iota(jnp.int32, (CHUNK, CHUNK), 1)
    tri = jnp.where(tc <= tr, 1.0, 0.0).astype(BF16)
    b = lambda m: m.astype(BF16)

    def stack(x):
        return b(jnp.where(same_head, jnp.concatenate([x] * GROUP_HEADS, axis=0), 0.0))

    n_groups = RWKV_HEADS // GROUP_HEADS

    def chunk_rows(c):
        return pl.ds(pl.multiple_of(c * CHUNK, CHUNK), CHUNK)

    def prepare_pair(it, carry):
        items = [(2 * it + u, gi) for u in range(2) for gi in range(n_groups)]
        n = range(len(items))
        kap, bt, kt, rt, vs, gl = [], [], [], [], [], []
        for c, gi in items:
            rows = chunk_rows(c)
            lanes = slice(gi * GW, (gi + 1) * GW)
            lw = lw_ref[0, rows, lanes]
            cs = _dot_split_rhs(tri, lw, 3)
            g_in = jnp.exp(cs)
            g_inv = jnp.exp(-cs)
            kap.append(stack(kk_ref[0, rows, lanes] * jnp.exp(cs - lw)))
            bt.append(stack(kka_ref[0, rows, lanes] * g_inv))
            kt.append(stack(k_ref[0, rows, lanes] * g_inv))
            rt.append(stack(r_ref[0, rows, lanes] * g_in))
            vs.append(stack(v_ref[0, rows, lanes]))
            gl.append(jnp.broadcast_to(g_in[CHUNK - 1:CHUNK, :], (8, GW)))
        a_kb = [jnp.where(strict, _dot_nt(kap[i], bt[i]), 0.0) for i in n]
        a_kk = [b(jnp.where(strict, _dot_nt(kap[i], kt[i]), 0.0)) for i in n]
        a_rk = [b(jnp.where(incl, _dot_nt(rt[i], kt[i]), 0.0)) for i in n]
        a_rb = [b(jnp.where(incl, _dot_nt(rt[i], bt[i]), 0.0)) for i in n]
        u = [_dot(a_kk[i], vs[i]) for i in n]
        y0 = [_dot(a_rk[i], vs[i]) for i in n]
        sv = [_dot_tn(vs[i], kt[i]) for i in n]
        t_inv = _unit_lower_inverses(a_kb, eye, in16, in32)
        for i, (c, gi) in enumerate(items):
            slot = c * n_groups + gi
            kap_s[slot] = kap[i]
            bt_s[slot] = bt[i]
            rt_s[slot] = rt[i]
            arb_s[slot] = a_rb[i]
            tinv_s[slot] = b(t_inv[i])
            u_s[slot] = u[i]
            y0_s[slot] = y0[i]
            sv_s[slot] = sv[i]
            gl_s[slot] = gl[i]
        return carry

    lax.fori_loop(0, tt // (2 * CHUNK), prepare_pair, 0)

    def advance(c, carry):
        gs = range(n_groups)
        slot = [c * n_groups + gi for gi in gs]
        s0 = [state[gi] for gi in gs]
        s0b = [b(s) for s in s0]
        rhs = [b(_dot_nt(kap_s[slot[gi]], s0b[gi]) + u_s[slot[gi]]) for gi in gs]
        pb = [b(_dot(tinv_s[slot[gi]], rhs[gi])) for gi in gs]
        for gi in gs:
            state[gi] = (s0[gi] + sv_s[slot[gi]] - _dot_tn(pb[gi], bt_s[slot[gi]])) * gl_s[slot[gi]][0:1, :]
        for gi in gs:
            y = _dot_nt(rt_s[slot[gi]], s0b[gi]) + y0_s[slot[gi]] - _dot(arb_s[slot[gi]], pb[gi])
            yd = y[0:CHUNK]
            for h in range(1, GROUP_HEADS):
                yd = yd + y[h * CHUNK:(h + 1) * CHUNK]
            y_s[chunk_rows(c), gi * GW:(gi + 1) * GW] = yd
        return carry

    lax.fori_loop(0, tt // CHUNK, advance, 0)

    def finish_pair(it, carry):
        items = [(chunk_rows(2 * it + u), slice(gi * GW, (gi + 1) * GW)) for u in range(2) for gi in range(n_groups)]
        yd = [y_s[rows, lanes] for rows, lanes in items]
        ym = [_dot_split_lhs(y, head_ones, 2) * (1.0 / HEAD_DIM) for y in yd]
        yc = [y - m for y, m in zip(yd, ym)]
        yv = [_dot_split_lhs(c * c, head_ones, 2) * (1.0 / HEAD_DIM) for c in yc]
        rk = [r_ref[0, rows, lanes] * k_ref[0, rows, lanes] * gn_ref[2:3, lanes] for rows, lanes in items]
        bonus = [_dot_split_lhs(x, head_ones, 2) for x in rk]
        for i, (rows, lanes) in enumerate(items):
            yn = yc[i] * lax.rsqrt(yv[i] + GN_EPS) * gn_ref[0:1, lanes] + gn_ref[1:2, lanes]
            o_ref[0, rows, lanes] = b((yn + bonus[i] * v_ref[0, rows, lanes]) * g_ref[0, rows, lanes])
        return carry

    lax.fori_loop(0, tt // (2 * CHUNK), finish_pair, 0)


def _rwkv_scan(r, lw, k, v, kk, kka, g, vec, tt):
    B, S, C = r.shape
    gn = jnp.pad(jnp.stack([vec[5], vec[6], vec[4]]), ((0, 5), (0, 0)))
    dense = pl.BlockSpec((1, tt, C), lambda b, i: (b, i, 0))
    slots = (tt // CHUNK) * (C // GW)
    assert tt % (2 * CHUNK) == 0
    stacked = lambda dtype: pltpu.VMEM((slots, SR, GW), dtype)
    return pl.pallas_call(
        functools.partial(_rwkv_scan_kernel, tt=tt),
        grid=(B, S // tt),
        in_specs=[dense] * 7 + [pl.BlockSpec((8, C), lambda b, i: (0, 0))],
        out_specs=dense,
        out_shape=jax.ShapeDtypeStruct((B, S, C), BF16),
        scratch_shapes=[pltpu.VMEM((C // GW, GW, GW), F32)] + [stacked(BF16)] * 5 + [stacked(F32)] * 3
                       + [pltpu.VMEM((slots, 8, GW), F32), pltpu.VMEM((tt, C), F32)],
        compiler_params=_params("parallel", "arbitrary"),
        name="rwkv_scan",
    )(r, lw, k, v, kk, kka, g, gn)


def _merge_kernel(x_ref, an_ref, ac_ref, ar_ref, gl_ref, wn_ref, wc_ref, wr_ref, wo_ref, g2_ref,
                  xo_ref, h_ref):
    D = x_ref.shape[1]
    o_n = _dot(an_ref[...].astype(BF16), wn_ref[...])
    mixed = _sigmoid(gl_ref[:, 0:D]) * o_n
    o_c = _dot(ac_ref[...], wc_ref[...])
    mixed = mixed + _sigmoid(gl_ref[:, D:2 * D]) * o_c
    o_r = _dot(ar_ref[...], wr_ref[...])
    mixed = mixed + _sigmoid(gl_ref[:, 2 * D:3 * D]) * o_r
    x = x_ref[...] + _dot(mixed.astype(BF16), wo_ref[...])
    xo_ref[...] = x
    ms = jnp.mean(x * x, axis=-1, keepdims=True)
    h_ref[...] = (x * lax.rsqrt(ms + RMS_EPS) * g2_ref[...]).astype(BF16)


def _merge(x, a_nsa, a_conv, a_rwkv, gl, w_n, w_c, w_r, w_o, g2, tm):
    T, D = x.shape
    row = lambda w: pl.BlockSpec((tm, w), lambda i: (i, 0))
    const = lambda shape: pl.BlockSpec(shape, lambda i: (0,) * len(shape))
    return pl.pallas_call(
        _merge_kernel,
        grid=(T // tm,),
        in_specs=[row(D), row(a_nsa.shape[1]), row(a_conv.shape[1]), row(a_rwkv.shape[1]), row(3 * D),
                  const(w_n.shape), const(w_c.shape), const(w_r.shape), const(w_o.shape), const((1, D))],
        out_specs=[row(D), row(D)],
        out_shape=[jax.ShapeDtypeStruct((T, D), F32), jax.ShapeDtypeStruct((T, D), BF16)],
        compiler_params=_params("parallel"),
        name="merge",
    )(x, a_nsa, a_conv, a_rwkv, gl, w_n.astype(BF16), w_c.astype(BF16), w_r.astype(BF16),
      w_o.astype(BF16), g2.reshape(1, D))


def _ffn_kernel(x_ref, h_ref, wg_ref, wu_ref, wd_ref, o_ref, acc):
    j = pl.program_id(1)

    @pl.when(j == 0)
    def _():
        acc[...] = x_ref[...]

    h = h_ref[...]
    gate = _dot(h, wg_ref[...])
    up = _dot(h, wu_ref[...])
    act = (gate * _sigmoid(gate) * up).astype(BF16)
    acc[...] += _dot(act, wd_ref[...])

    @pl.when(j == pl.num_programs(1) - 1)
    def _():
        o_ref[...] = acc[...]


def _ffn(x, h, w_gu, w_down, tm, tf):
    T, D = x.shape
    FF = w_down.shape[0]
    wg = w_gu[:, :FF].astype(BF16)
    wu = w_gu[:, FF:].astype(BF16)
    return pl.pallas_call(
        _ffn_kernel,
        grid=(T // tm, FF // tf),
        in_specs=[pl.BlockSpec((tm, D), lambda i, j: (i, 0)),
                  pl.BlockSpec((tm, D), lambda i, j: (i, 0)),
                  pl.BlockSpec((D, tf), lambda i, j: (0, j)),
                  pl.BlockSpec((D, tf), lambda i, j: (0, j)),
                  pl.BlockSpec((tf, D), lambda i, j: (j, 0))],
        out_specs=pl.BlockSpec((tm, D), lambda i, j: (i, 0)),
        out_shape=jax.ShapeDtypeStruct((T, D), F32),
        scratch_shapes=[pltpu.VMEM((tm, D), F32)],
        compiler_params=_params("parallel", "arbitrary"),
        name="ffn_dense",
    )(x, h, wg, wu, w_down.astype(BF16))


MOE_ROWS = 256
RANK_BLK = 256


def _router_kernel(h_ref, w_ref, b_ref, gw_ref, pos_ref, cnt_ref):
    logits = _dot(h_ref[...], w_ref[...]) + b_ref[...]
    tt = logits.shape[0]
    lane = lax.broadcasted_iota(jnp.int32, logits.shape, 1)
    m1 = jnp.max(logits, axis=-1, keepdims=True)
    i1 = jnp.min(jnp.where(logits == m1, lane, LANE), axis=-1, keepdims=True)
    rest = jnp.where(lane == i1, -jnp.inf, logits)
    m2 = jnp.max(rest, axis=-1, keepdims=True)
    i2 = jnp.min(jnp.where(rest == m2, lane, LANE), axis=-1, keepdims=True)
    e2 = jnp.exp(m2 - m1)
    w1 = 1.0 / (1.0 + e2)
    w2 = e2 / (1.0 + e2)
    gw_ref[...] = jnp.where(lane == i1, w1, jnp.where(lane == i2, w2, 0.0))
    routed = (lane == i1) | (lane == i2)
    r = lax.broadcasted_iota(jnp.int32, (RANK_BLK, RANK_BLK), 0)
    c = lax.broadcasted_iota(jnp.int32, (RANK_BLK, RANK_BLK), 1)
    before = jnp.where(c < r, 1.0, 0.0).astype(BF16)
    run = jnp.zeros((1, LANE), F32)
    ranks = []
    for blk in range(tt // RANK_BLK):
        hit = jnp.where(routed[blk * RANK_BLK:(blk + 1) * RANK_BLK], 1.0, 0.0)
        ranks.append(_dot(before, hit.astype(BF16)) + run)
        run = run + jnp.sum(hit, axis=0, keepdims=True)
    pos = jnp.where(routed, jnp.concatenate(ranks, axis=0), -1.0)
    pos_ref[0] = pos.T[0:N_EXPERTS]
    cnt_ref[0] = jnp.broadcast_to(run, (8, LANE))


def _router(h, router_w, router_b, tt):
    T, D = h.shape
    nt = T // tt
    w = _pad_cols(router_w, LANE).astype(BF16)
    b = jnp.pad(router_b, (0, LANE - N_EXPERTS), constant_values=-jnp.inf).reshape(1, LANE)
    return pl.pallas_call(
        _router_kernel,
        grid=(nt,),
        in_specs=[pl.BlockSpec((tt, D), lambda i: (i, 0)),
                  pl.BlockSpec((D, LANE), lambda i: (0, 0)),
                  pl.BlockSpec((1, LANE), lambda i: (0, 0))],
        out_specs=[pl.BlockSpec((tt, LANE), lambda i: (i, 0)),
                   pl.BlockSpec((1, N_EXPERTS, tt), lambda i: (i, 0, 0)),
                   pl.BlockSpec((1, 8, LANE), lambda i: (i, 0, 0))],
        out_shape=[jax.ShapeDtypeStruct((T, LANE), F32),
                   jax.ShapeDtypeStruct((nt, N_EXPERTS, tt), F32),
                   jax.ShapeDtypeStruct((nt, 8, LANE), F32)],
        compiler_params=_params("parallel"),
        name="moe_router",
    )(h, w, b)


def _moe_kernel(nsb_ref, x_ref, h_ref, gw_ref, pos_ref, wg_ref, wu_ref, wd_ref, o_ref, xc, yc):
    i = pl.program_id(0)
    e = pl.program_id(1)
    j = pl.program_id(2)
    n_sb = nsb_ref[i * pl.num_programs(1) + e]
    tt = h_ref.shape[0]

    @pl.when((e == 0) & (j == 0))
    def _():
        o_ref[...] = x_ref[...]

    pos = pos_ref[0, pl.ds(e, 1), :]
    slot = lax.broadcasted_iota(jnp.int32, (MOE_ROWS, 1), 0).astype(F32)

    def select(sb):
        return jnp.where(pos == slot + (sb * MOE_ROWS).astype(F32), 1.0, 0.0).astype(BF16)

    def rows(sb):
        return pl.ds(pl.multiple_of(sb * MOE_ROWS, MOE_ROWS), MOE_ROWS)

    @pl.when(j == 0)
    def _():
        def gather(sb, carry):
            xc[rows(sb), :] = _dot(select(sb), h_ref[...]).astype(BF16)
            yc[rows(sb), :] = jnp.zeros((MOE_ROWS, yc.shape[1]), F32)
            return carry
        lax.fori_loop(0, n_sb, gather, 0)

    def expert(sb, carry):
        xb = xc[rows(sb), :]
        gate = _dot(xb, wg_ref[0])
        up = _dot(xb, wu_ref[0])
        act = (gate * _sigmoid(gate) * up).astype(BF16)
        yc[rows(sb), :] += _dot(act, wd_ref[0])
        return carry
    lax.fori_loop(0, n_sb, expert, 0)

    @pl.when(j == pl.num_programs(2) - 1)
    def _():
        gw = gw_ref[...]
        lane = lax.broadcasted_iota(jnp.int32, gw.shape, 1)
        w_e = jnp.sum(jnp.where(lane == e, gw, 0.0), axis=-1, keepdims=True)

        def scatter(sb, carry):
            o_ref[...] += w_e * _dot_tn(select(sb), yc[rows(sb), :].astype(BF16))
            return carry
        lax.fori_loop(0, n_sb, scatter, 0)


def _moe(x, h, gw, pos, counts, w_gu, w_down, tt, tf):
    T, D = x.shape
    E, FF, _ = w_down.shape
    wg = w_gu[:, :, :FF].astype(BF16)
    wu = w_gu[:, :, FF:].astype(BF16)
    n_sb = ((counts[:, 0, :E].astype(jnp.int32) + MOE_ROWS - 1) // MOE_ROWS).reshape(-1)
    once = dict(pipeline_mode=pl.Buffered(1))
    return pl.pallas_call(
        _moe_kernel,
        grid_spec=pltpu.PrefetchScalarGridSpec(
            num_scalar_prefetch=1,
            grid=(T // tt, E, FF // tf),
            in_specs=[pl.BlockSpec((tt, D), lambda i, e, j, n: (i, 0), **once),
                      pl.BlockSpec((tt, D), lambda i, e, j, n: (i, 0), **once),
                      pl.BlockSpec((tt, LANE), lambda i, e, j, n: (i, 0)),
                      pl.BlockSpec((1, E, tt), lambda i, e, j, n: (i, 0, 0)),
                      pl.BlockSpec((1, D, tf), lambda i, e, j, n: (e, 0, j)),
                      pl.BlockSpec((1, D, tf), lambda i, e, j, n: (e, 0, j)),
                      pl.BlockSpec((1, tf, D), lambda i, e, j, n: (e, j, 0))],
            out_specs=pl.BlockSpec((tt, D), lambda i, e, j, n: (i, 0), **once),
            scratch_shapes=[pltpu.VMEM((tt, D), BF16), pltpu.VMEM((tt, D), F32)]),
        out_shape=jax.ShapeDtypeStruct((T, D), F32),
        compiler_params=_params("parallel", "arbitrary", "arbitrary"),
        name="moe_experts",
    )(n_sb, x, h, gw, pos, wg, wu, w_down.astype(BF16))


def _pad_cols(w, n):
    return jnp.pad(w, ((0, 0), (0, n - w.shape[1])))


def kernel(x, mix_norm_g, w_in_first, w_in_rest, rwkv_mu_first, rwkv_mu_rest, nsa_qk_g, nsa_cmp_pos, nsa_cmp_w1, nsa_cmp_w2, nsa_w_o, conv_w, conv_vec, conv_w_o, rwkv_vec, rwkv_w_up, rwkv_a_up, rwkv_g_up, rwkv_vres_up, rwkv_vres_b, rwkv_w_o, w_out, ffn_norm_g, dense_w_gu, dense_w_down, router_w, router_b, moe_w_gu, moe_w_down):
    B, S, D = x.shape
    T = B * S
    depth = mix_norm_g.shape[0]
    off_conv = NSA_IN
    off_gate = off_conv + 2 * CONV_CH
    off_rwkv = off_gate + 3 * D
    tm = min(512, T)
    xt = x.reshape(T, D)
    v_first = None
    for layer in range(depth):
        if layer == 0:
            w_in, mu, vres_up, vres_b = w_in_first, rwkv_mu_first, None, None
        else:
            w_in, mu = w_in_rest[layer - 1], rwkv_mu_rest[layer - 1]
            vres_up, vres_b = rwkv_vres_up[layer - 1], rwkv_vres_b[layer - 1]
        g1 = mix_norm_g[layer]
        w_nsa = _pad_cols(w_in[:, :off_conv], NSA_IN_PAD).astype(BF16)
        w_conv = w_in[:, off_conv:off_gate].astype(BF16)
        w_gate = w_in[:, off_gate:off_rwkv].astype(BF16)
        w_rwkv = _pad_cols(w_in[:, off_rwkv:], RWKV_IN_PAD).astype(BF16)
        p_nsa = _rms_mm(xt, g1, w_nsa, tm, NSA_IN_PAD).reshape(B, S, NSA_IN_PAD)
        p_conv = _rms_mm(xt, g1, w_conv, tm, 2 * CONV_CH).reshape(B, S, 2 * CONV_CH)
        p_gate = _rms_mm(xt, g1, w_gate, tm, D)
        p_rwkv = _rms_mm(xt, g1, w_rwkv, tm, RWKV_IN_PAD).reshape(B, S, RWKV_IN_PAD)

        a_nsa = _nsa_branch(p_nsa, nsa_qk_g[layer], nsa_cmp_pos[layer], nsa_cmp_w1[layer],
                            nsa_cmp_w2[layer], B, S)
        a_conv = _conformer(p_conv, conv_w[layer], conv_vec[layer], ts=min(512, S))
        r, lw, k, v, kk, kka, g = _rwkv_prep(
            p_rwkv, mu, rwkv_vec[layer], rwkv_w_up[layer], rwkv_a_up[layer], rwkv_g_up[layer],
            v_first, vres_up, vres_b, ts=min(256, S))
        if layer == 0:
            v_first = v
        a_rwkv = _rwkv_scan(r, lw, k, v, kk, kka, g, rwkv_vec[layer], tt=min(512, S))

        xt, h2 = _merge(xt, a_nsa.reshape(T, NSA_Q_W), a_conv.reshape(T, CONV_CH),
                        a_rwkv.reshape(T, RWKV_W), p_gate, nsa_w_o[layer], conv_w_o[layer],
                        rwkv_w_o[layer], w_out[layer], ffn_norm_g[layer], tm)
        if layer % 2 == 0:
            xt = _ffn(xt, h2, dense_w_gu[layer // 2], dense_w_down[layer // 2], min(1024, T), 256)
        else:
            tt = min(2048, T)
            gw, pos, counts = _router(h2, router_w[layer // 2], router_b[layer // 2], tt)
            xt = _moe(xt, h2, gw, pos, counts, moe_w_gu[layer // 2], moe_w_down[layer // 2], tt, 896)
    return xt.reshape(B, S, D)
```

```python
import functools

import jax
import jax.numpy as jnp
from jax import lax
from jax.experimental import pallas as pl
from jax.experimental.pallas import tpu as pltpu

F32 = jnp.float32
BF16 = jnp.bfloat16
HI = lax.Precision.HIGHEST

NSA_HEADS = 8
NSA_KV_HEADS = 2
NSA_GROUP = NSA_HEADS // NSA_KV_HEADS
HEAD_DIM = 64
NSA_Q_W = NSA_HEADS * HEAD_DIM
NSA_KV_W = NSA_KV_HEADS * HEAD_DIM
CMP_STRIDE = 16
CMP_LEN = 32
SEL_BLOCK = 64
N_SEL = 8
WINDOW = 512
SEL_FORCE = 1e4
CONV_CH = 512
CONV_K = 31
RWKV_HEADS = 8
RWKV_W = RWKV_HEADS * HEAD_DIM
DECAY_LORA = 64
ICLR_LORA = 64
GATE_LORA = 160
VRES_LORA = 32
N_EXPERTS = 8
ROPE_THETA = 10000.0
RMS_EPS = 1e-6
LN_EPS = 1e-5
GN_EPS = 64e-5
NEG_INF = -1e30
LOG2E = 1.4426950408889634

NSA_IN = NSA_Q_W + 6 * NSA_KV_W + 3 * NSA_HEADS
NSA_IN_PAD = 1408
NSA_GATE_BLK = (NSA_Q_W + 6 * NSA_KV_W) // 128
RWKV_IN_PAD = 1920
LANE = 128
VMEM_LIMIT = 56 * 1024 * 1024

CHUNK = 64


def _dot(a, b, precision=None):
    return jnp.dot(a, b, preferred_element_type=F32, precision=precision)


def _dot_nt(a, b, precision=None):
    return lax.dot_general(a, b, (((1,), (1,)), ((), ())), preferred_element_type=F32,
                           precision=precision)


def _dot_tn(a, b, precision=None):
    return lax.dot_general(a, b, (((0,), (0,)), ((), ())), preferred_element_type=F32,
                           precision=precision)


def _sigmoid(x):
    return 1.0 / (1.0 + jnp.exp(-x))


def _params(*sem):
    return pltpu.CompilerParams(dimension_semantics=sem, vmem_limit_bytes=VMEM_LIMIT)


def _rms_proj_kernel(x_ref, g_ref, *refs):
    n = len(refs) // 2
    x = x_ref[...]
    ms = jnp.mean(x * x, axis=-1, keepdims=True)
    h = (x * lax.rsqrt(ms + RMS_EPS) * g_ref[...]).astype(BF16)
    for w_ref, o_ref in zip(refs[:n], refs[n:]):
        o_ref[...] = _dot(h, w_ref[...])


def _rms_proj(x, g, ws, tm):
    T, D = x.shape
    once = dict(pipeline_mode=pl.Buffered(1))
    return pl.pallas_call(
        _rms_proj_kernel,
        grid=(T // tm,),
        in_specs=[pl.BlockSpec((tm, D), lambda i: (i, 0)),
                  pl.BlockSpec((1, D), lambda i: (0, 0))]
                 + [pl.BlockSpec(w.shape, lambda i: (0, 0), **once) for w in ws],
        out_specs=[pl.BlockSpec((tm, w.shape[1]), lambda i: (i, 0)) for w in ws],
        out_shape=[jax.ShapeDtypeStruct((T, w.shape[1]), F32) for w in ws],
        compiler_params=_params("parallel"),
        name="rms_proj",
    )(x, g.reshape(1, D), *ws)


def _rope_dense(x, cos, sin_signed):
    w = x.shape[1]
    lane = lax.broadcasted_iota(jnp.int32, x.shape, 1)
    first_half = (lane % HEAD_DIM) < (HEAD_DIM // 2)
    rot = jnp.where(first_half, pltpu.roll(x, w - HEAD_DIM // 2, 1), pltpu.roll(x, HEAD_DIM // 2, 1))
    return x * cos + rot * sin_signed


def _head_rms(x, bd, g):
    ms = _dot_split_lhs(x * x, bd, 3)
    return x * lax.rsqrt(ms + RMS_EPS) * g


def _nsa_prep_kernel(p_ref, cos_ref, sin_ref, gq_ref, gks_ref, gkw_ref, bd_ref,
                     q_ref, kc_ref, vc_ref, ks_ref, vs_ref, kw_ref, vw_ref):
    cos = cos_ref[...]
    sin = sin_ref[...]
    bd = bd_ref[...]
    q = p_ref[0, :, 0:NSA_Q_W]
    q = _rope_dense(_head_rms(q, bd, gq_ref[...]), cos, sin) * (HEAD_DIM ** -0.5 * LOG2E)
    for h in range(NSA_HEADS):
        q_ref[0, h] = q[:, h * HEAD_DIM:(h + 1) * HEAD_DIM].astype(BF16)

    def slab(i):
        return p_ref[0, :, NSA_Q_W + i * NSA_KV_W:NSA_Q_W + (i + 1) * NSA_KV_W]

    cos_k = cos[:, :NSA_KV_W]
    sin_k = sin[:, :NSA_KV_W]
    bd_k = bd[:NSA_KV_W, :NSA_KV_W]
    ks = _rope_dense(_head_rms(slab(2), bd_k, gks_ref[...]), cos_k, sin_k)
    kw = _rope_dense(_head_rms(slab(4), bd_k, gkw_ref[...]), cos_k, sin_k)
    for val, ref in ((slab(0), kc_ref), (slab(1), vc_ref), (ks, ks_ref), (slab(3), vs_ref),
                     (kw, kw_ref), (slab(5), vw_ref)):
        for g in range(NSA_KV_HEADS):
            ref[0, g] = val[:, g * HEAD_DIM:(g + 1) * HEAD_DIM].astype(BF16)


def _nsa_prep(p, cos, sin, qk_g, ts):
    B, S, _ = p.shape
    bd = jnp.kron(jnp.eye(NSA_HEADS, dtype=F32), jnp.full((HEAD_DIM, HEAD_DIM), 1.0 / HEAD_DIM, F32)).astype(BF16)
    gq = jnp.tile(qk_g[0], NSA_HEADS).reshape(1, NSA_Q_W)
    gks = jnp.tile(qk_g[2], NSA_KV_HEADS).reshape(1, NSA_KV_W)
    gkw = jnp.tile(qk_g[3], NSA_KV_HEADS).reshape(1, NSA_KV_W)
    kv_shape = jax.ShapeDtypeStruct((B, NSA_KV_HEADS, S, HEAD_DIM), BF16)
    kv_spec = pl.BlockSpec((1, NSA_KV_HEADS, ts, HEAD_DIM), lambda b, i: (b, 0, i, 0))
    const = lambda shape: pl.BlockSpec(shape, lambda b, i: (0,) * len(shape))
    return pl.pallas_call(
        _nsa_prep_kernel,
        grid=(B, S // ts),
        in_specs=[pl.BlockSpec((1, ts, NSA_IN_PAD), lambda b, i: (b, i, 0)),
                  pl.BlockSpec((ts, NSA_Q_W), lambda b, i: (i, 0)),
                  pl.BlockSpec((ts, NSA_Q_W), lambda b, i: (i, 0)),
                  const((1, NSA_Q_W)), const((1, NSA_KV_W)), const((1, NSA_KV_W)),
                  const((NSA_Q_W, NSA_Q_W))],
        out_specs=[pl.BlockSpec((1, NSA_HEADS, ts, HEAD_DIM), lambda b, i: (b, 0, i, 0))] + [kv_spec] * 6,
        out_shape=[jax.ShapeDtypeStruct((B, NSA_HEADS, S, HEAD_DIM), BF16)] + [kv_shape] * 6,
        compiler_params=_params("parallel", "parallel"),
        name="nsa_prep",
    )(p, cos, sin, gq, gks, gkw, bd)


def _gelu_tanh(x):
    return 0.5 * x * (1.0 + jnp.tanh(0.7978845608028654 * (x + 0.044715 * x * x * x)))


def _compress_kernel(a_ref, posa_ref, posb_ref, w1a_ref, w1b_ref, w2_ref, g_ref, cos_ref, sin_ref,
                     perm_ref, o_ref, *, is_key):
    a = a_ref[0]
    ncp = a.shape[0]
    w1a = w1a_ref[...]
    w1b = w1b_ref[...]
    p1 = _dot(a, w1a)
    p2 = _dot(a, w1b)
    hp = _dot(posa_ref[...], w1a) + _dot(posb_ref[...], w1b)
    h = _gelu_tanh(p1 + pltpu.roll(p2, ncp - 1, 0) + hp[0:1])
    o = _dot(h.astype(BF16), w2_ref[...])
    if is_key:
        ms = jnp.mean(o * o, axis=-1, keepdims=True)
        o = o * lax.rsqrt(ms + RMS_EPS) * g_ref[...]
        o = o * cos_ref[...] + _dot(o, perm_ref[...], precision=HI) * sin_ref[...]
    row = lax.broadcasted_iota(jnp.int32, o.shape, 0)
    o_ref[0] = jnp.where(row < ncp - 1, o, 0.0).astype(BF16)


def _compress(t, pos_emb, w1, w2, g, cos_c, sin_c, is_key):
    BG, ncp, cw = t.shape
    hid = w1.shape[1]
    pos = pos_emb.reshape(1, CMP_LEN * HEAD_DIM)
    posa = jnp.tile(pos[:, :cw], (8, 1)).astype(BF16)
    posb = jnp.tile(pos[:, cw:], (8, 1)).astype(BF16)
    half = HEAD_DIM // 2
    perm = jnp.roll(jnp.eye(HEAD_DIM, dtype=F32), half, axis=0)
    const = lambda shape: pl.BlockSpec(shape, lambda b: (0,) * len(shape))
    return pl.pallas_call(
        functools.partial(_compress_kernel, is_key=is_key),
        grid=(BG,),
        in_specs=[pl.BlockSpec((1, ncp, cw), lambda b: (b, 0, 0)),
                  const((8, cw)), const((8, cw)), const((cw, hid)), const((cw, hid)),
                  const((hid, HEAD_DIM)), const((1, HEAD_DIM)),
                  const((ncp, HEAD_DIM)), const((ncp, HEAD_DIM)), const((HEAD_DIM, HEAD_DIM))],
        out_specs=pl.BlockSpec((1, ncp, HEAD_DIM), lambda b: (b, 0, 0)),
        out_shape=jax.ShapeDtypeStruct((BG, ncp, HEAD_DIM), BF16),
        compiler_params=_params("parallel"),
        name="nsa_compress_k" if is_key else "nsa_compress_v",
    )(t, posa, posb, w1[:cw].astype(BF16), w1[cw:].astype(BF16), w2.astype(BF16),
      g.reshape(1, HEAD_DIM), cos_c, sin_c, perm)


def _exp_cols(s):
    m = jnp.max(s, axis=0, keepdims=True)
    e = jnp.exp2(s - m)
    return e, jnp.sum(e, axis=0, keepdims=True)


def _heads_to_rows(o, tq):
    return jnp.concatenate([o[:, h * tq:(h + 1) * tq] for h in range(NSA_GROUP)], axis=0)


def _nsa_attn_kernel(q_ref, kc_ref, vc_ref, ks_ref, vs_ref, kw_ref, vw_ref, gl_ref, o_ref, selb_ref, s_ref,
                     *, tq, tk, seq):
    g = pl.program_id(1)
    i = pl.program_id(2)
    rows = NSA_GROUP * tq
    ncp = seq // CMP_STRIDE
    ns = seq // SEL_BLOCK
    n_sel = min(N_SEL, ns)
    q = q_ref[0].reshape(rows, HEAD_DIM)
    t0 = i * tq
    t_q = t0 + lax.broadcasted_iota(jnp.int32, (1, tq), 1)

    def per_head(x):
        return jnp.concatenate([x] * NSA_GROUP, axis=1)

    c_end = lax.broadcasted_iota(jnp.int32, (ncp, 1), 0) * CMP_STRIDE + (CMP_LEN - 1)
    bias = jnp.where(c_end <= t_q, 0.0, NEG_INF)
    e, l = _exp_cols(_dot_nt(kc_ref[0], q) + per_head(bias))
    p_c = e * (per_head(jnp.where(t_q >= CMP_LEN - 1, 1.0, 0.0)) / l)
    o_c = _dot_tn(vc_ref[0], p_c.astype(BF16))

    p_sum = p_c[:, 0:tq]
    for h in range(1, NSA_GROUP):
        p_sum = p_sum + p_c[:, h * tq:(h + 1) * tq]
    s0 = lax.broadcasted_iota(jnp.int32, (ns, ncp), 0) * SEL_BLOCK
    c0 = lax.broadcasted_iota(jnp.int32, (ns, ncp), 1) * CMP_STRIDE
    overlap = jnp.where((c0 < s0 + SEL_BLOCK) & (c0 + CMP_LEN > s0), 1.0, 0.0).astype(BF16)
    imp = _dot_split_rhs(overlap, p_sum, 2)
    j = lax.broadcasted_iota(jnp.int32, (ns, tq), 0).astype(F32)
    cur = (t_q // SEL_BLOCK).astype(F32)
    forced = (j == 0.0) | (j == cur) | (j == cur - 1.0)
    score = jnp.where(forced, SEL_FORCE, jnp.where(j <= cur, imp, -1.0))
    for _ in range(n_sel):
        mx = jnp.max(score, axis=0, keepdims=True)
        first = jnp.min(jnp.where(score == mx, j, float(ns)), axis=0, keepdims=True)
        score = jnp.where(j == first, -jnp.inf, score)
    selb_ref[...] = jnp.where(score == -jnp.inf, 0.0, NEG_INF)

    wlen = WINDOW + tq
    w0 = pl.multiple_of(jnp.maximum(t0 - WINDOW, 0), tq)
    kpos = w0 + lax.broadcasted_iota(jnp.int32, (wlen, 1), 0)
    bias = jnp.where((kpos <= t_q) & (kpos > t_q - WINDOW), 0.0, NEG_INF)
    e, l = _exp_cols(_dot_nt(kw_ref[0, 0, pl.ds(w0, wlen), :], q) + per_head(bias))
    o_w = _dot_tn(vw_ref[0, 0, pl.ds(w0, wlen), :], e.astype(BF16)) / l

    key_off = lax.broadcasted_iota(jnp.int32, (SEL_BLOCK, 1), 0)

    last_tile = seq // tk - 1

    def scores(jt, slot):
        jt = jnp.minimum(jt, last_tile)
        k0 = pl.multiple_of(jt * tk, tk)
        bias = []
        for b in range(tk // SEL_BLOCK):
            picked = selb_ref[pl.ds(jt * (tk // SEL_BLOCK) + b, 1), :]
            causal = k0 + b * SEL_BLOCK + key_off <= t_q
            bias.append(jnp.where(causal, picked, NEG_INF))
        bias = jnp.concatenate(bias, axis=0)
        s_ref[slot] = _dot_nt(ks_ref[0, 0, pl.ds(k0, tk), :], q) + per_head(bias)

    def absorb(jt, slot, carry):
        m_run, l_run, acc = carry
        k0 = pl.multiple_of(jt * tk, tk)
        s = s_ref[slot]
        m_new = jnp.maximum(m_run, jnp.max(s, axis=0, keepdims=True))
        e = jnp.exp2(s - m_new)
        alpha = jnp.exp2(m_run - m_new)
        l_new = alpha * l_run + jnp.sum(e, axis=0, keepdims=True)
        acc = alpha * acc + _dot_tn(vs_ref[0, 0, pl.ds(k0, tk), :], e.astype(BF16))
        return m_new, l_new, acc

    def sel_pair(it, carry):
        scores(2 * it + 1, 1)
        carry = absorb(2 * it, 0, carry)
        scores(2 * it + 2, 0)
        return absorb(2 * it + 1, 1, carry)

    n_tiles = (t0 + tq - 1) // tk + 1
    init = (jnp.full((1, rows), NEG_INF, F32), jnp.zeros((1, rows), F32), jnp.zeros((HEAD_DIM, rows), F32))
    scores(0, 0)
    _, l_s, acc_s = lax.fori_loop(0, (n_tiles + 1) // 2, sel_pair, init)
    o_s = acc_s / l_s

    gw = NSA_GROUP * HEAD_DIM
    n = lax.broadcasted_iota(jnp.int32, (3 * gw, LANE), 0)
    col = lax.broadcasted_iota(jnp.int32, (3 * gw, LANE), 1)
    head = g * NSA_GROUP + (n % gw) // HEAD_DIM
    spread = jnp.where(col == head * 3 + n // gw, 1.0, 0.0).astype(BF16)
    gates = None
    for piece in _split_bf16(_sigmoid(gl_ref[0]), 3):
        part = _dot_nt(spread, piece)
        gates = part if gates is None else gates + part
    out = gates[0:gw] * _heads_to_rows(o_c, tq)
    out = out + gates[gw:2 * gw] * _heads_to_rows(o_s, tq)
    out = out + gates[2 * gw:] * _heads_to_rows(o_w, tq)
    o_ref[0] = out.T


def _nsa_attn(q, kc, vc, ks, vs, kw, vw, p_nsa, tq, tk):
    B, H, S, _ = q.shape
    G = NSA_KV_HEADS
    ncp = S // CMP_STRIDE
    assert S % (2 * tk) == 0
    cmp_spec = pl.BlockSpec((1, ncp, HEAD_DIM), lambda b, g, i: (b * G + g, 0, 0))
    seq_spec = pl.BlockSpec((1, 1, S, HEAD_DIM), lambda b, g, i: (b, g, 0, 0))
    return pl.pallas_call(
        functools.partial(_nsa_attn_kernel, tq=tq, tk=tk, seq=S),
        grid=(B, G, S // tq),
        in_specs=[pl.BlockSpec((1, NSA_GROUP, tq, HEAD_DIM), lambda b, g, i: (b, g, i, 0)),
                  cmp_spec, cmp_spec, seq_spec, seq_spec, seq_spec, seq_spec,
                  pl.BlockSpec((1, tq, LANE), lambda b, g, i: (b, i, NSA_GATE_BLK))],
        out_specs=pl.BlockSpec((1, tq, NSA_GROUP * HEAD_DIM), lambda b, g, i: (b, i, g)),
        out_shape=jax.ShapeDtypeStruct((B, S, NSA_Q_W), F32),
        scratch_shapes=[pltpu.VMEM((S // SEL_BLOCK, tq), F32),
                        pltpu.VMEM((2, tk, NSA_GROUP * tq), F32)],
        compiler_params=_params("parallel", "parallel", "arbitrary"),
        name="nsa_attn",
    )(q, kc, vc, ks, vs, kw, vw, p_nsa)


def _rope_tables(S):
    half = HEAD_DIM // 2
    inv = ROPE_THETA ** (-jnp.arange(half, dtype=F32) / half)
    ang = jnp.arange(S, dtype=F32)[:, None] * inv[None, :]
    cos, sin = jnp.cos(ang), jnp.sin(ang)
    cos64 = jnp.concatenate([cos, cos], axis=1)
    sin64 = jnp.concatenate([-sin, sin], axis=1)
    return cos64, sin64


def _nsa_branch(p_nsa, qk_g, cmp_pos, cmp_w1, cmp_w2, B, S):
    cos64, sin64 = _rope_tables(S)
    cos = jnp.tile(cos64, (1, NSA_HEADS))
    sin = jnp.tile(sin64, (1, NSA_HEADS))
    q, kc, vc, ks, vs, kw, vw = _nsa_prep(p_nsa, cos, sin, qk_g, ts=min(256, S))
    ncp = S // CMP_STRIDE
    cmp_end = jnp.minimum(jnp.arange(ncp) * CMP_STRIDE + CMP_LEN - 1, S - 1)
    cw = CMP_STRIDE * HEAD_DIM
    kc = kc.reshape(B * NSA_KV_HEADS, ncp, cw)
    vc = vc.reshape(B * NSA_KV_HEADS, ncp, cw)
    k_cmp = _compress(kc, cmp_pos[0], cmp_w1[0], cmp_w2[0], qk_g[1], cos64[cmp_end], sin64[cmp_end], True)
    v_cmp = _compress(vc, cmp_pos[1], cmp_w1[1], cmp_w2[1], qk_g[1], cos64[cmp_end], sin64[cmp_end], False)
    return _nsa_attn(q, k_cmp, v_cmp, ks, vs, kw, vw, p_nsa, tq=256, tk=256)


CONV_HALO = 32


def _conv_kernel(cur_ref, halo_ref, w_ref, vec_ref, o_ref, buf, y_scr, *, ts):
    i = pl.program_id(1)

    def glu(p):
        return p[:, :CONV_CH] * _sigmoid(p[:, CONV_CH:])

    buf[0:CONV_HALO, :] = jnp.where(i > 0, glu(halo_ref[0]), 0.0)
    buf[CONV_HALO:, :] = glu(cur_ref[0])
    rt = min(ts, 128)
    shift = CONV_HALO - (CONV_K - 1)
    for c in range(CONV_CH // LANE):
        cs = slice(c * LANE, (c + 1) * LANE)
        for r in range(ts // rt):
            acc = jnp.zeros((rt, LANE), F32)
            for j in range(CONV_K):
                acc = acc + w_ref[j:j + 1, cs] * buf[r * rt + shift + j:r * rt + shift + j + rt, cs]
            y_scr[r * rt:(r + 1) * rt, cs] = acc
    y = y_scr[...] + vec_ref[0:1, :]
    mu = jnp.mean(y, axis=-1, keepdims=True)
    yc = y - mu
    var = jnp.mean(yc * yc, axis=-1, keepdims=True)
    y = yc * lax.rsqrt(var + LN_EPS) * vec_ref[1:2, :] + vec_ref[2:3, :]
    o_ref[0] = (y * _sigmoid(y)).astype(BF16)


def _conformer(p, conv_w, conv_vec, ts):
    B, S, _ = p.shape
    w = jnp.pad(conv_w, ((0, 32 - CONV_K), (0, 0)))
    vec = jnp.pad(conv_vec, ((0, 5), (0, 0)))
    hb = ts // CONV_HALO
    return pl.pallas_call(
        functools.partial(_conv_kernel, ts=ts),
        grid=(B, S // ts),
        in_specs=[pl.BlockSpec((1, ts, 2 * CONV_CH), lambda b, i: (b, i, 0)),
                  pl.BlockSpec((1, CONV_HALO, 2 * CONV_CH), lambda b, i: (b, jnp.maximum(i * hb - 1, 0), 0)),
                  pl.BlockSpec((32, CONV_CH), lambda b, i: (0, 0)),
                  pl.BlockSpec((8, CONV_CH), lambda b, i: (0, 0))],
        out_specs=pl.BlockSpec((1, ts, CONV_CH), lambda b, i: (b, i, 0)),
        out_shape=jax.ShapeDtypeStruct((B, S, CONV_CH), BF16),
        scratch_shapes=[pltpu.VMEM((CONV_HALO + ts, CONV_CH), F32), pltpu.VMEM((ts, CONV_CH), F32)],
        compiler_params=_params("parallel", "parallel"),
        name="conformer",
    )(p, p, w, vec)


RW_LORA_OFF = 3 * RWKV_W
RW_GATE_OFF = RW_LORA_OFF + DECAY_LORA + ICLR_LORA
RW_VRES_OFF = RWKV_IN_PAD - LANE


def _rwkv_prep_kernel(*refs, ts, has_vres):
    if has_vres:
        (cur_ref, halo_ref, mu_ref, vec_ref, wup_ref, aup_ref, gup_ref, bd_ref, vf_ref, vb_ref, vup_ref,
         r_ref, lw_ref, k_ref, v_ref, kk_ref, kka_ref, g_ref) = refs
    else:
        (cur_ref, halo_ref, mu_ref, vec_ref, wup_ref, aup_ref, gup_ref, bd_ref,
         r_ref, lw_ref, k_ref, v_ref, kk_ref, kka_ref, g_ref) = refs
    i = pl.program_id(1)
    p = cur_ref[0]
    last = jnp.where(i > 0, halo_ref[0, 7:8, :], 0.0)
    row = lax.broadcasted_iota(jnp.int32, p.shape, 0)
    prev = jnp.where(row == 0, last, pltpu.roll(p, 1, 0))
    u = p + (prev - p) * mu_ref[...]
    C = RWKV_W
    r, k, v = u[:, :C], u[:, C:2 * C], u[:, 2 * C:3 * C]
    lora = u[:, RW_LORA_OFF:RW_GATE_OFF]
    w_in = vec_ref[0:1, :] + _dot(jnp.tanh(lora).astype(BF16), wup_ref[...])
    sp = jnp.maximum(-w_in, 0.0) + jnp.log(1.0 + jnp.exp(-jnp.abs(w_in)))
    lw = -jnp.exp(-sp - 0.5)
    a = _sigmoid(vec_ref[1:2, :] + _dot(lora.astype(BF16), aup_ref[...]))
    g = _dot(_sigmoid(u[:, RW_GATE_OFF:]).astype(BF16), gup_ref[...])
    if has_vres:
        mix = _sigmoid(vb_ref[...] + _dot(u[:, RW_VRES_OFF:].astype(BF16), vup_ref[...]))
        v = v + (vf_ref[0] - v) * mix
    kk = k * vec_ref[2:3, :]
    ss = _dot_split_lhs(kk * kk, bd_ref[...], 3)
    kk = kk / jnp.maximum(jnp.sqrt(ss), 1e-12)
    k = k * (1.0 + (a - 1.0) * vec_ref[3:4, :])
    for val, ref in ((r, r_ref), (lw, lw_ref), (k, k_ref), (v, v_ref), (kk, kk_ref), (kk * a, kka_ref),
                     (g, g_ref)):
        ref[0] = val


def _pad_rows(w, n):
    return jnp.pad(w, ((0, n - w.shape[0]), (0, 0)))


def _rwkv_prep(p, mu, vec, w_up, a_up, g_up, v_first, vres_up, vres_b, ts):
    B, S, _ = p.shape
    C = RWKV_W
    has_vres = v_first is not None
    mu_p = jnp.pad(mu, (0, RWKV_IN_PAD - mu.shape[0])).reshape(1, RWKV_IN_PAD)
    n_lora = DECAY_LORA + ICLR_LORA
    wup = _pad_rows(w_up, n_lora).astype(BF16)
    aup = jnp.pad(a_up, ((DECAY_LORA, 0), (0, 0))).astype(BF16)
    gup = _pad_rows(g_up, RWKV_IN_PAD - RW_GATE_OFF).astype(BF16)
    bd = jnp.kron(jnp.eye(RWKV_HEADS, dtype=F32), jnp.ones((HEAD_DIM, HEAD_DIM), F32)).astype(BF16)
    const = lambda shape: pl.BlockSpec(shape, lambda b, i: (0,) * len(shape))
    dense = pl.BlockSpec((1, ts, C), lambda b, i: (b, i, 0))
    in_specs = [pl.BlockSpec((1, ts, RWKV_IN_PAD), lambda b, i: (b, i, 0)),
                pl.BlockSpec((1, 8, RWKV_IN_PAD), lambda b, i: (b, jnp.maximum(i * (ts // 8) - 1, 0), 0)),
                const((1, RWKV_IN_PAD)), const((8, C)), const((n_lora, C)), const((n_lora, C)),
                const((RWKV_IN_PAD - RW_GATE_OFF, C)), const((C, C))]
    args = [p, p, mu_p, jnp.pad(vec, ((0, 1), (0, 0))), wup, aup, gup, bd]
    if has_vres:
        off = RWKV_IN_PAD - LANE
        lo = 3 * C + n_lora + GATE_LORA - off
        vup = jnp.pad(vres_up, ((lo, LANE - lo - VRES_LORA), (0, 0))).astype(BF16)
        in_specs += [dense, const((1, C)), const((LANE, C))]
        args += [v_first, vres_b.reshape(1, C), vup]
    return pl.pallas_call(
        functools.partial(_rwkv_prep_kernel, ts=ts, has_vres=has_vres),
        grid=(B, S // ts),
        in_specs=in_specs,
        out_specs=[dense] * 7,
        out_shape=[jax.ShapeDtypeStruct((B, S, C), F32)] * 7,
        compiler_params=_params("parallel", "parallel"),
        name="rwkv_prep",
    )(*args)


GROUP_HEADS = 4
GW = GROUP_HEADS * HEAD_DIM
SR = GROUP_HEADS * CHUNK


def _split_bf16(x, n):
    parts = []
    for _ in range(n - 1):
        h = x.astype(BF16)
        parts.append(h)
        x = x - h.astype(F32)
    parts.append(x.astype(BF16))
    return parts


def _dot_split_rhs(a, x, n):
    parts = _split_bf16(x, n)
    out = _dot(a, parts[0])
    for p in parts[1:]:
        out = out + _dot(a, p)
    return out


def _dot_split_lhs(x, b, n):
    parts = _split_bf16(x, n)
    out = _dot(parts[0], b)
    for p in parts[1:]:
        out = out + _dot(p, b)
    return out


def _unit_lower_inverses(mats, eye, in16, in32):
    b = lambda m: m.astype(BF16)
    ds = [b(jnp.where(in16, a, 0.0)) for a in mats]
    xs = [eye - d for d in ds]
    ps = [b(_dot(d, d)) for d in ds]
    for step in range(3):
        xs = [x + _dot(b(x), p) for x, p in zip(xs, ps)]
        if step < 2:
            ps = [b(_dot(p, p)) for p in ps]
    for lowers in ([jnp.where(in32 & jnp.logical_not(in16), a, 0.0) for a in mats],
                   [jnp.where(in32, 0.0, a) for a in mats]):
        xbs = [b(x) for x in xs]
        mids = [b(_dot(xb, b(lo))) for xb, lo in zip(xbs, lowers)]
        xs = [x - _dot(mid, xb) for x, mid, xb in zip(xs, mids, xbs)]
    return xs


def _rwkv_scan_kernel(r_ref, lw_ref, k_ref, v_ref, kk_ref, kka_ref, g_ref, gn_ref, o_ref, state,
                      kap_s, bt_s, rt_s, arb_s, tinv_s, u_s, y0_s, sv_s, gl_s, y_s, *, tt):
    @pl.when(pl.program_id(1) == 0)
    def _():
        state[...] = jnp.zeros_like(state)

    ri = lax.broadcasted_iota(jnp.int32, (SR, GW), 0)
    ci = lax.broadcasted_iota(jnp.int32, (SR, GW), 1)
    same_head = (ri // CHUNK) == (ci // HEAD_DIM)
    strict = (ci % CHUNK) < (ri % CHUNK)
    incl = (ci % CHUNK) <= (ri % CHUNK)
    in16 = (ri // 16) == (ci // 16)
    in32 = (ri // 32) == (ci // 32)
    eye = jnp.where(ri == ci, 1.0, 0.0)
    head_ones = jnp.where(same_head, 1.0, 0.0).astype(BF16)
    tr = lax.broadcasted_iota(jnp.int32, (CHUNK, CHUNK), 0)
    tc = lax.broadcasted_iota(jnp.int32, (CHUNK, CHUNK), 1)
    tri = jnp.where(tc <= tr, 1.0, 0.0).astype(BF16)
    b = lambda m: m.astype(BF16)

    def stack(x):
        return b(jnp.where(same_head, jnp.concatenate([x] * GROUP_HEADS, axis=0), 0.0))

    n_groups = RWKV_HEADS // GROUP_HEADS

    def chunk_rows(c):
        return pl.ds(pl.multiple_of(c * CHUNK, CHUNK), CHUNK)

    def prepare_pair(it, carry):
        items = [(2 * it + u, gi) for u in range(2) for gi in range(n_groups)]
        n = range(len(items))
        kap, bt, kt, rt, vs, gl = [], [], [], [], [], []
        for c, gi in items:
            rows = chunk_rows(c)
            lanes = slice(gi * GW, (gi + 1) * GW)
            lw = lw_ref[0, rows, lanes]
            cs = _dot_split_rhs(tri, lw, 3)
            g_in = jnp.exp(cs)
            g_inv = jnp.exp(-cs)
            kap.append(stack(kk_ref[0, rows, lanes] * jnp.exp(cs - lw)))
            bt.append(stack(kka_ref[0, rows, lanes] * g_inv))
            kt.append(stack(k_ref[0, rows, lanes] * g_inv))
            rt.append(stack(r_ref[0, rows, lanes] * g_in))
            vs.append(stack(v_ref[0, rows, lanes]))
            gl.append(jnp.broadcast_to(g_in[CHUNK - 1:CHUNK, :], (8, GW)))
        a_kb = [jnp.where(strict, _dot_nt(kap[i], bt[i]), 0.0) for i in n]
        a_kk = [b(jnp.where(strict, _dot_nt(kap[i], kt[i]), 0.0)) for i in n]
        a_rk = [b(jnp.where(incl, _dot_nt(rt[i], kt[i]), 0.0)) for i in n]
        a_rb = [b(jnp.where(incl, _dot_nt(rt[i], bt[i]), 0.0)) for i in n]
        u = [_dot(a_kk[i], vs[i]) for i in n]
        y0 = [_dot(a_rk[i], vs[i]) for i in n]
        sv = [_dot_tn(vs[i], kt[i]) for i in n]
        t_inv = _unit_lower_inverses(a_kb, eye, in16, in32)
        for i, (c, gi) in enumerate(items):
            slot = c * n_groups + gi
            kap_s[slot] = kap[i]
            bt_s[slot] = bt[i]
            rt_s[slot] = rt[i]
            arb_s[slot] = a_rb[i]
            tinv_s[slot] = b(t_inv[i])
            u_s[slot] = u[i]
            y0_s[slot] = y0[i]
            sv_s[slot] = sv[i]
            gl_s[slot] = gl[i]
        return carry

    lax.fori_loop(0, tt // (2 * CHUNK), prepare_pair, 0)

    def advance(c, carry):
        gs = range(n_groups)
        slot = [c * n_groups + gi for gi in gs]
        s0 = [state[gi] for gi in gs]
        s0b = [b(s) for s in s0]
        rhs = [b(_dot_nt(kap_s[slot[gi]], s0b[gi]) + u_s[slot[gi]]) for gi in gs]
        pb = [b(_dot(tinv_s[slot[gi]], rhs[gi])) for gi in gs]
        for gi in gs:
            state[gi] = (s0[gi] + sv_s[slot[gi]] - _dot_tn(pb[gi], bt_s[slot[gi]])) * gl_s[slot[gi]][0:1, :]
        for gi in gs:
            y = _dot_nt(rt_s[slot[gi]], s0b[gi]) + y0_s[slot[gi]] - _dot(arb_s[slot[gi]], pb[gi])
            yd = y[0:CHUNK]
            for h in range(1, GROUP_HEADS):
                yd = yd + y[h * CHUNK:(h + 1) * CHUNK]
            y_s[chunk_rows(c), gi * GW:(gi + 1) * GW] = yd
        return carry

    lax.fori_loop(0, tt // CHUNK, advance, 0)

    def finish_pair(it, carry):
        items = [(chunk_rows(2 * it + u), slice(gi * GW, (gi + 1) * GW)) for u in range(2) for gi in range(n_groups)]
        yd = [y_s[rows, lanes] for rows, lanes in items]
        ym = [_dot_split_lhs(y, head_ones, 2) * (1.0 / HEAD_DIM) for y in yd]
        yc = [y - m for y, m in zip(yd, ym)]
        yv = [_dot_split_lhs(c * c, head_ones, 2) * (1.0 / HEAD_DIM) for c in yc]
        rk = [r_ref[0, rows, lanes] * k_ref[0, rows, lanes] * gn_ref[2:3, lanes] for rows, lanes in items]
        bonus = [_dot_split_lhs(x, head_ones, 2) for x in rk]
        for i, (rows, lanes) in enumerate(items):
            yn = yc[i] * lax.rsqrt(yv[i] + GN_EPS) * gn_ref[0:1, lanes] + gn_ref[1:2, lanes]
            o_ref[0, rows, lanes] = b((yn + bonus[i] * v_ref[0, rows, lanes]) * g_ref[0, rows, lanes])
        return carry

    lax.fori_loop(0, tt // (2 * CHUNK), finish_pair, 0)


def _rwkv_scan(r, lw, k, v, kk, kka, g, vec, tt):
    B, S, C = r.shape
    gn = jnp.pad(jnp.stack([vec[5], vec[6], vec[4]]), ((0, 5), (0, 0)))
    dense = pl.BlockSpec((1, tt, C), lambda b, i: (b, i, 0))
    slots = (tt // CHUNK) * (C // GW)
    assert tt % (2 * CHUNK) == 0
    stacked = lambda dtype: pltpu.VMEM((slots, SR, GW), dtype)
    return pl.pallas_call(
        functools.partial(_rwkv_scan_kernel, tt=tt),
        grid=(B, S // tt),
        in_specs=[dense] * 7 + [pl.BlockSpec((8, C), lambda b, i: (0, 0))],
        out_specs=dense,
        out_shape=jax.ShapeDtypeStruct((B, S, C), BF16),
        scratch_shapes=[pltpu.VMEM((C // GW, GW, GW), F32)] + [stacked(BF16)] * 5 + [stacked(F32)] * 3
                       + [pltpu.VMEM((slots, 8, GW), F32), pltpu.VMEM((tt, C), F32)],
        compiler_params=_params("parallel", "arbitrary"),
        name="rwkv_scan",
    )(r, lw, k, v, kk, kka, g, gn)


def _merge_kernel(x_ref, an_ref, ac_ref, ar_ref, gl_ref, wn_ref, wc_ref, wr_ref, wo_ref, g2_ref,
                  xo_ref, h_ref):
    D = x_ref.shape[1]
    o_n = _dot(an_ref[...].astype(BF16), wn_ref[...])
    mixed = _sigmoid(gl_ref[:, 0:D]) * o_n
    o_c = _dot(ac_ref[...], wc_ref[...])
    mixed = mixed + _sigmoid(gl_ref[:, D:2 * D]) * o_c
    o_r = _dot(ar_ref[...], wr_ref[...])
    mixed = mixed + _sigmoid(gl_ref[:, 2 * D:3 * D]) * o_r
    x = x_ref[...] + _dot(mixed.astype(BF16), wo_ref[...])
    xo_ref[...] = x
    ms = jnp.mean(x * x, axis=-1, keepdims=True)
    h_ref[...] = (x * lax.rsqrt(ms + RMS_EPS) * g2_ref[...]).astype(BF16)


def _merge(x, a_nsa, a_conv, a_rwkv, gl, w_n, w_c, w_r, w_o, g2, tm):
    T, D = x.shape
    row = lambda w: pl.BlockSpec((tm, w), lambda i: (i, 0))
    const = lambda shape: pl.BlockSpec(shape, lambda i: (0,) * len(shape))
    return pl.pallas_call(
        _merge_kernel,
        grid=(T // tm,),
        in_specs=[row(D), row(a_nsa.shape[1]), row(a_conv.shape[1]), row(a_rwkv.shape[1]), row(3 * D),
                  const(w_n.shape), const(w_c.shape), const(w_r.shape), const(w_o.shape), const((1, D))],
        out_specs=[row(D), row(D)],
        out_shape=[jax.ShapeDtypeStruct((T, D), F32), jax.ShapeDtypeStruct((T, D), BF16)],
        compiler_params=_params("parallel"),
        name="merge",
    )(x, a_nsa, a_conv, a_rwkv, gl, w_n.astype(BF16), w_c.astype(BF16), w_r.astype(BF16),
      w_o.astype(BF16), g2.reshape(1, D))


def _ffn_kernel(x_ref, h_ref, wg_ref, wu_ref, wd_ref, o_ref):
    j = pl.program_id(1)
    h = h_ref[...]
    gate = _dot(h, wg_ref[...])
    up = _dot(h, wu_ref[...])
    act = (gate * _sigmoid(gate) * up).astype(BF16)
    y = _dot(act, wd_ref[...])

    @pl.when(j == 0)
    def _():
        o_ref[...] = x_ref[...] + y

    @pl.when(j > 0)
    def _():
        o_ref[...] += y


def _ffn(x, h, w_gu, w_down, tm, tf):
    T, D = x.shape
    FF = w_down.shape[0]
    wg = w_gu[:, :FF].astype(BF16)
    wu = w_gu[:, FF:].astype(BF16)
    return pl.pallas_call(
        _ffn_kernel,
        grid=(T // tm, FF // tf),
        in_specs=[pl.BlockSpec((tm, D), lambda i, j: (i, 0)),
                  pl.BlockSpec((tm, D), lambda i, j: (i, 0)),
                  pl.BlockSpec((D, tf), lambda i, j: (0, j)),
                  pl.BlockSpec((D, tf), lambda i, j: (0, j)),
                  pl.BlockSpec((tf, D), lambda i, j: (j, 0))],
        out_specs=pl.BlockSpec((tm, D), lambda i, j: (i, 0)),
        out_shape=jax.ShapeDtypeStruct((T, D), F32),
        compiler_params=_params("parallel", "arbitrary"),
        name="ffn_dense",
    )(x, h, wg, wu, w_down.astype(BF16))


MOE_ROWS = 256
RANK_BLK = 256


def _router_kernel(h_ref, w_ref, b_ref, gw_ref, pos_ref, cnt_ref):
    logits = _dot(h_ref[...], w_ref[...]) + b_ref[...]
    tt = logits.shape[0]
    lane = lax.broadcasted_iota(jnp.int32, logits.shape, 1)
    m1 = jnp.max(logits, axis=-1, keepdims=True)
    i1 = jnp.min(jnp.where(logits == m1, lane, LANE), axis=-1, keepdims=True)
    rest = jnp.where(lane == i1, -jnp.inf, logits)
    m2 = jnp.max(rest, axis=-1, keepdims=True)
    i2 = jnp.min(jnp.where(rest == m2, lane, LANE), axis=-1, keepdims=True)
    e2 = jnp.exp(m2 - m1)
    w1 = 1.0 / (1.0 + e2)
    w2 = e2 / (1.0 + e2)
    gw_ref[...] = jnp.where(lane == i1, w1, jnp.where(lane == i2, w2, 0.0))
    routed = (lane == i1) | (lane == i2)
    r = lax.broadcasted_iota(jnp.int32, (RANK_BLK, RANK_BLK), 0)
    c = lax.broadcasted_iota(jnp.int32, (RANK_BLK, RANK_BLK), 1)
    before = jnp.where(c < r, 1.0, 0.0).astype(BF16)
    run = jnp.zeros((1, LANE), F32)
    ranks = []
    for blk in range(tt // RANK_BLK):
        hit = jnp.where(routed[blk * RANK_BLK:(blk + 1) * RANK_BLK], 1.0, 0.0)
        ranks.append(_dot(before, hit.astype(BF16)) + run)
        run = run + jnp.sum(hit, axis=0, keepdims=True)
    pos = jnp.where(routed, jnp.concatenate(ranks, axis=0), -1.0)
    pos_ref[0] = pos.T[0:N_EXPERTS]
    cnt_ref[0] = jnp.broadcast_to(run, (8, LANE))


def _router(h, router_w, router_b, tt):
    T, D = h.shape
    nt = T // tt
    w = _pad_cols(router_w, LANE).astype(BF16)
    b = jnp.pad(router_b, (0, LANE - N_EXPERTS), constant_values=-jnp.inf).reshape(1, LANE)
    return pl.pallas_call(
        _router_kernel,
        grid=(nt,),
        in_specs=[pl.BlockSpec((tt, D), lambda i: (i, 0)),
                  pl.BlockSpec((D, LANE), lambda i: (0, 0)),
                  pl.BlockSpec((1, LANE), lambda i: (0, 0))],
        out_specs=[pl.BlockSpec((tt, LANE), lambda i: (i, 0)),
                   pl.BlockSpec((1, N_EXPERTS, tt), lambda i: (i, 0, 0)),
                   pl.BlockSpec((1, 8, LANE), lambda i: (i, 0, 0))],
        out_shape=[jax.ShapeDtypeStruct((T, LANE), F32),
                   jax.ShapeDtypeStruct((nt, N_EXPERTS, tt), F32),
                   jax.ShapeDtypeStruct((nt, 8, LANE), F32)],
        compiler_params=_params("parallel"),
        name="moe_router",
    )(h, w, b)


def _moe_kernel(nsb_ref, x_ref, h_ref, gw_ref, pos_ref, wg_ref, wu_ref, wd_ref, o_ref, xc, yc):
    i = pl.program_id(0)
    e = pl.program_id(1)
    j = pl.program_id(2)
    n_sb = nsb_ref[i * pl.num_programs(1) + e]
    tt = h_ref.shape[0]

    @pl.when((e == 0) & (j == 0))
    def _():
        o_ref[...] = x_ref[...]

    pos = pos_ref[0, pl.ds(e, 1), :]
    slot = lax.broadcasted_iota(jnp.int32, (MOE_ROWS, 1), 0).astype(F32)

    def select(sb):
        return jnp.where(pos == slot + (sb * MOE_ROWS).astype(F32), 1.0, 0.0).astype(BF16)

    def rows(sb):
        return pl.ds(pl.multiple_of(sb * MOE_ROWS, MOE_ROWS), MOE_ROWS)

    @pl.when(j == 0)
    def _():
        def gather(sb, carry):
            xc[rows(sb), :] = _dot(select(sb), h_ref[...]).astype(BF16)
            yc[rows(sb), :] = jnp.zeros((MOE_ROWS, yc.shape[1]), F32)
            return carry
        lax.fori_loop(0, n_sb, gather, 0)

    def expert(sb, carry):
        xb = xc[rows(sb), :]
        gate = _dot(xb, wg_ref[0])
        up = _dot(xb, wu_ref[0])
        act = (gate * _sigmoid(gate) * up).astype(BF16)
        yc[rows(sb), :] += _dot(act, wd_ref[0])
        return carry
    lax.fori_loop(0, n_sb, expert, 0)

    @pl.when(j == pl.num_programs(2) - 1)
    def _():
        gw = gw_ref[...]
        lane = lax.broadcasted_iota(jnp.int32, gw.shape, 1)
        w_e = jnp.sum(jnp.where(lane == e, gw, 0.0), axis=-1, keepdims=True)

        def scatter(sb, carry):
            o_ref[...] += w_e * _dot_tn(select(sb), yc[rows(sb), :].astype(BF16))
            return carry
        lax.fori_loop(0, n_sb, scatter, 0)


def _moe(x, h, gw, pos, counts, w_gu, w_down, tt, tf):
    T, D = x.shape
    E, FF, _ = w_down.shape
    wg = w_gu[:, :, :FF].astype(BF16)
    wu = w_gu[:, :, FF:].astype(BF16)
    n_sb = ((counts[:, 0, :E].astype(jnp.int32) + MOE_ROWS - 1) // MOE_ROWS).reshape(-1)
    once = dict(pipeline_mode=pl.Buffered(1))
    return pl.pallas_call(
        _moe_kernel,
        grid_spec=pltpu.PrefetchScalarGridSpec(
            num_scalar_prefetch=1,
            grid=(T // tt, E, FF // tf),
            in_specs=[pl.BlockSpec((tt, D), lambda i, e, j, n: (i, 0), **once),
                      pl.BlockSpec((tt, D), lambda i, e, j, n: (i, 0), **once),
                      pl.BlockSpec((tt, LANE), lambda i, e, j, n: (i, 0)),
                      pl.BlockSpec((1, E, tt), lambda i, e, j, n: (i, 0, 0)),
                      pl.BlockSpec((1, D, tf), lambda i, e, j, n: (e, 0, j)),
                      pl.BlockSpec((1, D, tf), lambda i, e, j, n: (e, 0, j)),
                      pl.BlockSpec((1, tf, D), lambda i, e, j, n: (e, j, 0))],
            out_specs=pl.BlockSpec((tt, D), lambda i, e, j, n: (i, 0), **once),
            scratch_shapes=[pltpu.VMEM((tt, D), BF16), pltpu.VMEM((tt, D), F32)]),
        out_shape=jax.ShapeDtypeStruct((T, D), F32),
        compiler_params=_params("parallel", "arbitrary", "arbitrary"),
        name="moe_experts",
    )(n_sb, x, h, gw, pos, wg, wu, w_down.astype(BF16))


def _pad_cols(w, n):
    return jnp.pad(w, ((0, 0), (0, n - w.shape[1])))


def kernel(x, mix_norm_g, w_in_first, w_in_rest, rwkv_mu_first, rwkv_mu_rest, nsa_qk_g, nsa_cmp_pos, nsa_cmp_w1, nsa_cmp_w2, nsa_w_o, conv_w, conv_vec, conv_w_o, rwkv_vec, rwkv_w_up, rwkv_a_up, rwkv_g_up, rwkv_vres_up, rwkv_vres_b, rwkv_w_o, w_out, ffn_norm_g, dense_w_gu, dense_w_down, router_w, router_b, moe_w_gu, moe_w_down):
    B, S, D = x.shape
    T = B * S
    depth = mix_norm_g.shape[0]
    off_conv = NSA_IN
    off_gate = off_conv + 2 * CONV_CH
    off_rwkv = off_gate + 3 * D
    tm = min(512, T)
    xt = x.reshape(T, D)
    v_first = None
    for layer in range(depth):
        if layer == 0:
            w_in, mu, vres_up, vres_b = w_in_first, rwkv_mu_first, None, None
        else:
            w_in, mu = w_in_rest[layer - 1], rwkv_mu_rest[layer - 1]
            vres_up, vres_b = rwkv_vres_up[layer - 1], rwkv_vres_b[layer - 1]
        g1 = mix_norm_g[layer]
        w_nsa = _pad_cols(w_in[:, :off_conv], NSA_IN_PAD).astype(BF16)
        w_conv = w_in[:, off_conv:off_gate].astype(BF16)
        w_gate = w_in[:, off_gate:off_rwkv].astype(BF16)
        w_rwkv = _pad_cols(w_in[:, off_rwkv:], RWKV_IN_PAD).astype(BF16)
        p_nsa, p_conv, p_gate, p_rwkv = _rms_proj(xt, g1, [w_nsa, w_conv, w_gate, w_rwkv], min(256, T))
        p_nsa = p_nsa.reshape(B, S, NSA_IN_PAD)
        p_conv = p_conv.reshape(B, S, 2 * CONV_CH)
        p_rwkv = p_rwkv.reshape(B, S, RWKV_IN_PAD)

        a_nsa = _nsa_branch(p_nsa, nsa_qk_g[layer], nsa_cmp_pos[layer], nsa_cmp_w1[layer],
                            nsa_cmp_w2[layer], B, S)
        a_conv = _conformer(p_conv, conv_w[layer], conv_vec[layer], ts=min(512, S))
        r, lw, k, v, kk, kka, g = _rwkv_prep(
            p_rwkv, mu, rwkv_vec[layer], rwkv_w_up[layer], rwkv_a_up[layer], rwkv_g_up[layer],
            v_first, vres_up, vres_b, ts=min(256, S))
        if layer == 0:
            v_first = v
        a_rwkv = _rwkv_scan(r, lw, k, v, kk, kka, g, rwkv_vec[layer], tt=min(512, S))

        xt, h2 = _merge(xt, a_nsa.reshape(T, NSA_Q_W), a_conv.reshape(T, CONV_CH),
                        a_rwkv.reshape(T, RWKV_W), p_gate, nsa_w_o[layer], conv_w_o[layer],
                        rwkv_w_o[layer], w_out[layer], ffn_norm_g[layer], tm)
        if layer % 2 == 0:
            xt = _ffn(xt, h2, dense_w_gu[layer // 2], dense_w_down[layer // 2], min(512, T), 1408)
        else:
            tt = min(2048, T)
            gw, pos, counts = _router(h2, router_w[layer // 2], router_b[layer // 2], tt)
            xt = _moe(xt, h2, gw, pos, counts, moe_w_gu[layer // 2], moe_w_down[layer // 2], tt, 896)
    return xt.reshape(B, S, D)
```

```python
import functools

import jax
import jax.numpy as jnp
from jax import lax
from jax.experimental import pallas as pl
from jax.experimental.pallas import tpu as pltpu

F32 = jnp.float32
BF16 = jnp.bfloat16
HI = lax.Precision.HIGHEST

NSA_HEADS = 8
NSA_KV_HEADS = 2
NSA_GROUP = NSA_HEADS // NSA_KV_HEADS
HEAD_DIM = 64
NSA_Q_W = NSA_HEADS * HEAD_DIM
NSA_KV_W = NSA_KV_HEADS * HEAD_DIM
CMP_STRIDE = 16
CMP_LEN = 32
SEL_BLOCK = 64
N_SEL = 8
WINDOW = 512
SEL_FORCE = 1e4
CONV_CH = 512
CONV_K = 31
RWKV_HEADS = 8
RWKV_W = RWKV_HEADS * HEAD_DIM
DECAY_LORA = 64
ICLR_LORA = 64
GATE_LORA = 160
VRES_LORA = 32
N_EXPERTS = 8
ROPE_THETA = 10000.0
RMS_EPS = 1e-6
LN_EPS = 1e-5
GN_EPS = 64e-5
NEG_INF = -1e30
LOG2E = 1.4426950408889634

NSA_IN = NSA_Q_W + 6 * NSA_KV_W + 3 * NSA_HEADS
NSA_IN_PAD = 1408
NSA_GATE_BLK = (NSA_Q_W + 6 * NSA_KV_W) // 128
RWKV_IN_PAD = 1920
LANE = 128
VMEM_LIMIT = 56 * 1024 * 1024

CHUNK = 64


def _dot(a, b, precision=None):
    return jnp.dot(a, b, preferred_element_type=F32, precision=precision)


def _dot_nt(a, b, precision=None):
    return lax.dot_general(a, b, (((1,), (1,)), ((), ())), preferred_element_type=F32,
                           precision=precision)


def _dot_tn(a, b, precision=None):
    return lax.dot_general(a, b, (((0,), (0,)), ((), ())), preferred_element_type=F32,
                           precision=precision)


def _sigmoid(x):
    return 1.0 / (1.0 + jnp.exp(-x))


def _params(*sem):
    return pltpu.CompilerParams(dimension_semantics=sem, vmem_limit_bytes=VMEM_LIMIT)


def _rms_proj_kernel(x_ref, g_ref, *refs):
    n = len(refs) // 2
    x = x_ref[...]
    ms = jnp.mean(x * x, axis=-1, keepdims=True)
    h = (x * lax.rsqrt(ms + RMS_EPS) * g_ref[...]).astype(BF16)
    for w_ref, o_ref in zip(refs[:n], refs[n:]):
        o_ref[...] = _dot(h, w_ref[...])


def _rms_proj(x, g, ws, tm):
    T, D = x.shape
    once = dict(pipeline_mode=pl.Buffered(1))
    return pl.pallas_call(
        _rms_proj_kernel,
        grid=(T // tm,),
        in_specs=[pl.BlockSpec((tm, D), lambda i: (i, 0)),
                  pl.BlockSpec((1, D), lambda i: (0, 0))]
                 + [pl.BlockSpec(w.shape, lambda i: (0, 0), **once) for w in ws],
        out_specs=[pl.BlockSpec((tm, w.shape[1]), lambda i: (i, 0)) for w in ws],
        out_shape=[jax.ShapeDtypeStruct((T, w.shape[1]), F32) for w in ws],
        compiler_params=_params("parallel"),
        name="rms_proj",
    )(x, g.reshape(1, D), *ws)


def _rope_dense(x, cos, sin_signed):
    w = x.shape[1]
    lane = lax.broadcasted_iota(jnp.int32, x.shape, 1)
    first_half = (lane % HEAD_DIM) < (HEAD_DIM // 2)
    rot = jnp.where(first_half, pltpu.roll(x, w - HEAD_DIM // 2, 1), pltpu.roll(x, HEAD_DIM // 2, 1))
    return x * cos + rot * sin_signed


def _head_rms(x, bd, g):
    ms = _dot_split_lhs(x * x, bd, 3)
    return x * lax.rsqrt(ms + RMS_EPS) * g


def _nsa_prep_kernel(p_ref, cos_ref, sin_ref, gq_ref, gks_ref, gkw_ref, bd_ref,
                     q_ref, kc_ref, vc_ref, ks_ref, vs_ref, kw_ref, vw_ref):
    cos = cos_ref[...]
    sin = sin_ref[...]
    bd = bd_ref[...]
    q = p_ref[0, :, 0:NSA_Q_W]
    q = _rope_dense(_head_rms(q, bd, gq_ref[...]), cos, sin) * (HEAD_DIM ** -0.5 * LOG2E)
    for h in range(NSA_HEADS):
        q_ref[0, h] = q[:, h * HEAD_DIM:(h + 1) * HEAD_DIM].astype(BF16)

    def slab(i):
        return p_ref[0, :, NSA_Q_W + i * NSA_KV_W:NSA_Q_W + (i + 1) * NSA_KV_W]

    cos_k = cos[:, :NSA_KV_W]
    sin_k = sin[:, :NSA_KV_W]
    bd_k = bd[:NSA_KV_W, :NSA_KV_W]
    ks = _rope_dense(_head_rms(slab(2), bd_k, gks_ref[...]), cos_k, sin_k)
    kw = _rope_dense(_head_rms(slab(4), bd_k, gkw_ref[...]), cos_k, sin_k)
    for val, ref in ((slab(0), kc_ref), (slab(1), vc_ref), (ks, ks_ref), (slab(3), vs_ref),
                     (kw, kw_ref), (slab(5), vw_ref)):
        for g in range(NSA_KV_HEADS):
            ref[0, g] = val[:, g * HEAD_DIM:(g + 1) * HEAD_DIM].astype(BF16)


def _nsa_prep(p, cos, sin, qk_g, ts):
    B, S, _ = p.shape
    bd = jnp.kron(jnp.eye(NSA_HEADS, dtype=F32), jnp.full((HEAD_DIM, HEAD_DIM), 1.0 / HEAD_DIM, F32)).astype(BF16)
    gq = jnp.tile(qk_g[0], NSA_HEADS).reshape(1, NSA_Q_W)
    gks = jnp.tile(qk_g[2], NSA_KV_HEADS).reshape(1, NSA_KV_W)
    gkw = jnp.tile(qk_g[3], NSA_KV_HEADS).reshape(1, NSA_KV_W)
    kv_shape = jax.ShapeDtypeStruct((B, NSA_KV_HEADS, S, HEAD_DIM), BF16)
    kv_spec = pl.BlockSpec((1, NSA_KV_HEADS, ts, HEAD_DIM), lambda b, i: (b, 0, i, 0))
    const = lambda shape: pl.BlockSpec(shape, lambda b, i: (0,) * len(shape))
    return pl.pallas_call(
        _nsa_prep_kernel,
        grid=(B, S // ts),
        in_specs=[pl.BlockSpec((1, ts, NSA_IN_PAD), lambda b, i: (b, i, 0)),
                  pl.BlockSpec((ts, NSA_Q_W), lambda b, i: (i, 0)),
                  pl.BlockSpec((ts, NSA_Q_W), lambda b, i: (i, 0)),
                  const((1, NSA_Q_W)), const((1, NSA_KV_W)), const((1, NSA_KV_W)),
                  const((NSA_Q_W, NSA_Q_W))],
        out_specs=[pl.BlockSpec((1, NSA_HEADS, ts, HEAD_DIM), lambda b, i: (b, 0, i, 0))] + [kv_spec] * 6,
        out_shape=[jax.ShapeDtypeStruct((B, NSA_HEADS, S, HEAD_DIM), BF16)] + [kv_shape] * 6,
        compiler_params=_params("parallel", "parallel"),
        name="nsa_prep",
    )(p, cos, sin, gq, gks, gkw, bd)


def _gelu_tanh(x):
    return 0.5 * x * (1.0 + jnp.tanh(0.7978845608028654 * (x + 0.044715 * x * x * x)))


def _compress_kernel(a_ref, posa_ref, posb_ref, w1a_ref, w1b_ref, w2_ref, g_ref, cos_ref, sin_ref,
                     perm_ref, o_ref, *, is_key):
    a = a_ref[0]
    ncp = a.shape[0]
    w1a = w1a_ref[...]
    w1b = w1b_ref[...]
    p1 = _dot(a, w1a)
    p2 = _dot(a, w1b)
    hp = _dot(posa_ref[...], w1a) + _dot(posb_ref[...], w1b)
    h = _gelu_tanh(p1 + pltpu.roll(p2, ncp - 1, 0) + hp[0:1])
    o = _dot(h.astype(BF16), w2_ref[...])
    if is_key:
        ms = jnp.mean(o * o, axis=-1, keepdims=True)
        o = o * lax.rsqrt(ms + RMS_EPS) * g_ref[...]
        o = o * cos_ref[...] + _dot(o, perm_ref[...], precision=HI) * sin_ref[...]
    row = lax.broadcasted_iota(jnp.int32, o.shape, 0)
    o_ref[0] = jnp.where(row < ncp - 1, o, 0.0).astype(BF16)


def _compress(t, pos_emb, w1, w2, g, cos_c, sin_c, is_key):
    BG, ncp, cw = t.shape
    hid = w1.shape[1]
    pos = pos_emb.reshape(1, CMP_LEN * HEAD_DIM)
    posa = jnp.tile(pos[:, :cw], (8, 1)).astype(BF16)
    posb = jnp.tile(pos[:, cw:], (8, 1)).astype(BF16)
    half = HEAD_DIM // 2
    perm = jnp.roll(jnp.eye(HEAD_DIM, dtype=F32), half, axis=0)
    const = lambda shape: pl.BlockSpec(shape, lambda b: (0,) * len(shape))
    return pl.pallas_call(
        functools.partial(_compress_kernel, is_key=is_key),
        grid=(BG,),
        in_specs=[pl.BlockSpec((1, ncp, cw), lambda b: (b, 0, 0)),
                  const((8, cw)), const((8, cw)), const((cw, hid)), const((cw, hid)),
                  const((hid, HEAD_DIM)), const((1, HEAD_DIM)),
                  const((ncp, HEAD_DIM)), const((ncp, HEAD_DIM)), const((HEAD_DIM, HEAD_DIM))],
        out_specs=pl.BlockSpec((1, ncp, HEAD_DIM), lambda b: (b, 0, 0)),
        out_shape=jax.ShapeDtypeStruct((BG, ncp, HEAD_DIM), BF16),
        compiler_params=_params("parallel"),
        name="nsa_compress_k" if is_key else "nsa_compress_v",
    )(t, posa, posb, w1[:cw].astype(BF16), w1[cw:].astype(BF16), w2.astype(BF16),
      g.reshape(1, HEAD_DIM), cos_c, sin_c, perm)


def _exp_cols(s):
    m = jnp.max(s, axis=0, keepdims=True)
    e = jnp.exp2(s - m)
    return e, jnp.sum(e, axis=0, keepdims=True)


def _heads_to_rows(o, tq):
    return jnp.concatenate([o[:, h * tq:(h + 1) * tq] for h in range(NSA_GROUP)], axis=0)


def _nsa_attn_kernel(q_ref, kc_ref, vc_ref, ks_ref, vs_ref, kw_ref, vw_ref, gl_ref, o_ref, selb_ref, s_ref,
                     *, tq, tk, seq):
    g = pl.program_id(1)
    i = pl.program_id(2)
    rows = NSA_GROUP * tq
    ncp = seq // CMP_STRIDE
    ns = seq // SEL_BLOCK
    n_sel = min(N_SEL, ns)
    q = q_ref[0].reshape(rows, HEAD_DIM)
    t0 = i * tq
    t_q = t0 + lax.broadcasted_iota(jnp.int32, (1, tq), 1)

    def per_head(x):
        return jnp.concatenate([x] * NSA_GROUP, axis=1)

    c_end = lax.broadcasted_iota(jnp.int32, (ncp, 1), 0) * CMP_STRIDE + (CMP_LEN - 1)
    bias = jnp.where(c_end <= t_q, 0.0, NEG_INF)
    e, l = _exp_cols(_dot_nt(kc_ref[0], q) + per_head(bias))
    p_c = e * (per_head(jnp.where(t_q >= CMP_LEN - 1, 1.0, 0.0)) / l)
    o_c = _dot_tn(vc_ref[0], p_c.astype(BF16))

    p_sum = p_c[:, 0:tq]
    for h in range(1, NSA_GROUP):
        p_sum = p_sum + p_c[:, h * tq:(h + 1) * tq]
    s0 = lax.broadcasted_iota(jnp.int32, (ns, ncp), 0) * SEL_BLOCK
    c0 = lax.broadcasted_iota(jnp.int32, (ns, ncp), 1) * CMP_STRIDE
    overlap = jnp.where((c0 < s0 + SEL_BLOCK) & (c0 + CMP_LEN > s0), 1.0, 0.0).astype(BF16)
    imp = _dot_split_rhs(overlap, p_sum, 2)
    j = lax.broadcasted_iota(jnp.int32, (ns, tq), 0).astype(F32)
    cur = (t_q // SEL_BLOCK).astype(F32)
    forced = (j == 0.0) | (j == cur) | (j == cur - 1.0)
    score = jnp.where(forced, SEL_FORCE, jnp.where(j <= cur, imp, -1.0))
    for _ in range(n_sel):
        mx = jnp.max(score, axis=0, keepdims=True)
        first = jnp.min(jnp.where(score == mx, j, float(ns)), axis=0, keepdims=True)
        score = jnp.where(j == first, -jnp.inf, score)
    selb_ref[...] = jnp.where(score == -jnp.inf, 0.0, NEG_INF)

    wlen = WINDOW + tq
    w0 = pl.multiple_of(jnp.maximum(t0 - WINDOW, 0), tq)
    kpos = w0 + lax.broadcasted_iota(jnp.int32, (wlen, 1), 0)
    bias = jnp.where((kpos <= t_q) & (kpos > t_q - WINDOW), 0.0, NEG_INF)
    e, l = _exp_cols(_dot_nt(kw_ref[0, 0, pl.ds(w0, wlen), :], q) + per_head(bias))
    o_w = _dot_tn(vw_ref[0, 0, pl.ds(w0, wlen), :], e.astype(BF16)) / l

    key_off = lax.broadcasted_iota(jnp.int32, (SEL_BLOCK, 1), 0)

    last_tile = seq // tk - 1

    def scores(jt, slot):
        jt = jnp.minimum(jt, last_tile)
        k0 = pl.multiple_of(jt * tk, tk)
        bias = []
        for b in range(tk // SEL_BLOCK):
            picked = selb_ref[pl.ds(jt * (tk // SEL_BLOCK) + b, 1), :]
            causal = k0 + b * SEL_BLOCK + key_off <= t_q
            bias.append(jnp.where(causal, picked, NEG_INF))
        bias = jnp.concatenate(bias, axis=0)
        s_ref[slot] = _dot_nt(ks_ref[0, 0, pl.ds(k0, tk), :], q) + per_head(bias)

    def absorb(jt, slot, carry):
        m_run, l_run, acc = carry
        k0 = pl.multiple_of(jt * tk, tk)
        s = s_ref[slot]
        m_new = jnp.maximum(m_run, jnp.max(s, axis=0, keepdims=True))
        e = jnp.exp2(s - m_new)
        alpha = jnp.exp2(m_run - m_new)
        l_new = alpha * l_run + jnp.sum(e, axis=0, keepdims=True)
        acc = alpha * acc + _dot_tn(vs_ref[0, 0, pl.ds(k0, tk), :], e.astype(BF16))
        return m_new, l_new, acc

    def sel_pair(it, carry):
        scores(2 * it + 1, 1)
        carry = absorb(2 * it, 0, carry)
        scores(2 * it + 2, 0)
        return absorb(2 * it + 1, 1, carry)

    n_tiles = (t0 + tq - 1) // tk + 1
    init = (jnp.full((1, rows), NEG_INF, F32), jnp.zeros((1, rows), F32), jnp.zeros((HEAD_DIM, rows), F32))
    scores(0, 0)
    _, l_s, acc_s = lax.fori_loop(0, (n_tiles + 1) // 2, sel_pair, init)
    o_s = acc_s / l_s

    gw = NSA_GROUP * HEAD_DIM
    n = lax.broadcasted_iota(jnp.int32, (3 * gw, LANE), 0)
    col = lax.broadcasted_iota(jnp.int32, (3 * gw, LANE), 1)
    head = g * NSA_GROUP + (n % gw) // HEAD_DIM
    spread = jnp.where(col == head * 3 + n // gw, 1.0, 0.0).astype(BF16)
    gates = None
    for piece in _split_bf16(_sigmoid(gl_ref[0]), 3):
        part = _dot_nt(spread, piece)
        gates = part if gates is None else gates + part
    out = gates[0:gw] * _heads_to_rows(o_c, tq)
    out = out + gates[gw:2 * gw] * _heads_to_rows(o_s, tq)
    out = out + gates[2 * gw:] * _heads_to_rows(o_w, tq)
    o_ref[0] = out.T


def _nsa_attn(q, kc, vc, ks, vs, kw, vw, p_nsa, tq, tk):
    B, H, S, _ = q.shape
    G = NSA_KV_HEADS
    ncp = S // CMP_STRIDE
    assert S % (2 * tk) == 0
    cmp_spec = pl.BlockSpec((1, ncp, HEAD_DIM), lambda b, g, i: (b * G + g, 0, 0))
    seq_spec = pl.BlockSpec((1, 1, S, HEAD_DIM), lambda b, g, i: (b, g, 0, 0))
    return pl.pallas_call(
        functools.partial(_nsa_attn_kernel, tq=tq, tk=tk, seq=S),
        grid=(B, G, S // tq),
        in_specs=[pl.BlockSpec((1, NSA_GROUP, tq, HEAD_DIM), lambda b, g, i: (b, g, i, 0)),
                  cmp_spec, cmp_spec, seq_spec, seq_spec, seq_spec, seq_spec,
                  pl.BlockSpec((1, tq, LANE), lambda b, g, i: (b, i, NSA_GATE_BLK))],
        out_specs=pl.BlockSpec((1, tq, NSA_GROUP * HEAD_DIM), lambda b, g, i: (b, i, g)),
        out_shape=jax.ShapeDtypeStruct((B, S, NSA_Q_W), F32),
        scratch_shapes=[pltpu.VMEM((S // SEL_BLOCK, tq), F32),
                        pltpu.VMEM((2, tk, NSA_GROUP * tq), F32)],
        compiler_params=_params("parallel", "parallel", "arbitrary"),
        name="nsa_attn",
    )(q, kc, vc, ks, vs, kw, vw, p_nsa)


def _rope_tables(S):
    half = HEAD_DIM // 2
    inv = ROPE_THETA ** (-jnp.arange(half, dtype=F32) / half)
    ang = jnp.arange(S, dtype=F32)[:, None] * inv[None, :]
    cos, sin = jnp.cos(ang), jnp.sin(ang)
    cos64 = jnp.concatenate([cos, cos], axis=1)
    sin64 = jnp.concatenate([-sin, sin], axis=1)
    return cos64, sin64


def _nsa_branch(p_nsa, qk_g, cmp_pos, cmp_w1, cmp_w2, B, S):
    cos64, sin64 = _rope_tables(S)
    cos = jnp.tile(cos64, (1, NSA_HEADS))
    sin = jnp.tile(sin64, (1, NSA_HEADS))
    q, kc, vc, ks, vs, kw, vw = _nsa_prep(p_nsa, cos, sin, qk_g, ts=min(256, S))
    ncp = S // CMP_STRIDE
    cmp_end = jnp.minimum(jnp.arange(ncp) * CMP_STRIDE + CMP_LEN - 1, S - 1)
    cw = CMP_STRIDE * HEAD_DIM
    kc = kc.reshape(B * NSA_KV_HEADS, ncp, cw)
    vc = vc.reshape(B * NSA_KV_HEADS, ncp, cw)
    k_cmp = _compress(kc, cmp_pos[0], cmp_w1[0], cmp_w2[0], qk_g[1], cos64[cmp_end], sin64[cmp_end], True)
    v_cmp = _compress(vc, cmp_pos[1], cmp_w1[1], cmp_w2[1], qk_g[1], cos64[cmp_end], sin64[cmp_end], False)
    return _nsa_attn(q, k_cmp, v_cmp, ks, vs, kw, vw, p_nsa, tq=256, tk=256)


CONV_HALO = 32


def _conv_kernel(cur_ref, halo_ref, w_ref, vec_ref, o_ref, buf, y_scr, shifted, *, ts):
    i = pl.program_id(1)

    def glu(p):
        return p[:, :CONV_CH] * _sigmoid(p[:, CONV_CH:])

    buf[0:CONV_HALO, :] = jnp.where(i > 0, glu(halo_ref[0]), 0.0)
    buf[CONV_HALO:, :] = glu(cur_ref[0])
    rt = min(ts, 128)
    shift = CONV_HALO - (CONV_K - 1)
    for c in range(CONV_CH // LANE):
        cs = slice(c * LANE, (c + 1) * LANE)
        for b in range(8):
            span = ts + 8 * ((CONV_K - 1 - b) // 8)
            shifted[b, 0:span, :] = buf[shift + b:shift + b + span, cs]
        for r in range(ts // rt):
            acc = jnp.zeros((rt, LANE), F32)
            for j in range(CONV_K):
                acc = acc + w_ref[j:j + 1, cs] * shifted[j % 8, r * rt + 8 * (j // 8):r * rt + 8 * (j // 8) + rt, :]
            y_scr[r * rt:(r + 1) * rt, cs] = acc
    y = y_scr[...] + vec_ref[0:1, :]
    mu = jnp.mean(y, axis=-1, keepdims=True)
    yc = y - mu
    var = jnp.mean(yc * yc, axis=-1, keepdims=True)
    y = yc * lax.rsqrt(var + LN_EPS) * vec_ref[1:2, :] + vec_ref[2:3, :]
    o_ref[0] = (y * _sigmoid(y)).astype(BF16)


def _conformer(p, conv_w, conv_vec, ts):
    B, S, _ = p.shape
    w = jnp.pad(conv_w, ((0, 32 - CONV_K), (0, 0)))
    vec = jnp.pad(conv_vec, ((0, 5), (0, 0)))
    hb = ts // CONV_HALO
    return pl.pallas_call(
        functools.partial(_conv_kernel, ts=ts),
        grid=(B, S // ts),
        in_specs=[pl.BlockSpec((1, ts, 2 * CONV_CH), lambda b, i: (b, i, 0)),
                  pl.BlockSpec((1, CONV_HALO, 2 * CONV_CH), lambda b, i: (b, jnp.maximum(i * hb - 1, 0), 0)),
                  pl.BlockSpec((32, CONV_CH), lambda b, i: (0, 0)),
                  pl.BlockSpec((8, CONV_CH), lambda b, i: (0, 0))],
        out_specs=pl.BlockSpec((1, ts, CONV_CH), lambda b, i: (b, i, 0)),
        out_shape=jax.ShapeDtypeStruct((B, S, CONV_CH), BF16),
        scratch_shapes=[pltpu.VMEM((CONV_HALO + ts, CONV_CH), F32), pltpu.VMEM((ts, CONV_CH), F32),
                        pltpu.VMEM((8, ts + 8 * ((CONV_K - 1) // 8), LANE), F32)],
        compiler_params=_params("parallel", "parallel"),
        name="conformer",
    )(p, p, w, vec)


RW_LORA_OFF = 3 * RWKV_W
RW_GATE_OFF = RW_LORA_OFF + DECAY_LORA + ICLR_LORA
RW_VRES_OFF = RWKV_IN_PAD - LANE


def _rwkv_prep_kernel(*refs, ts, has_vres):
    if has_vres:
        (cur_ref, halo_ref, mu_ref, vec_ref, wup_ref, aup_ref, gup_ref, bd_ref, vf_ref, vb_ref, vup_ref,
         r_ref, lw_ref, k_ref, v_ref, kk_ref, kka_ref, g_ref) = refs
    else:
        (cur_ref, halo_ref, mu_ref, vec_ref, wup_ref, aup_ref, gup_ref, bd_ref,
         r_ref, lw_ref, k_ref, v_ref, kk_ref, kka_ref, g_ref) = refs
    i = pl.program_id(1)
    p = cur_ref[0]
    last = jnp.where(i > 0, halo_ref[0, 7:8, :], 0.0)
    row = lax.broadcasted_iota(jnp.int32, p.shape, 0)
    prev = jnp.where(row == 0, last, pltpu.roll(p, 1, 0))
    u = p + (prev - p) * mu_ref[...]
    C = RWKV_W
    r, k, v = u[:, :C], u[:, C:2 * C], u[:, 2 * C:3 * C]
    lora = u[:, RW_LORA_OFF:RW_GATE_OFF]
    w_in = vec_ref[0:1, :] + _dot(jnp.tanh(lora).astype(BF16), wup_ref[...])
    sp = jnp.maximum(-w_in, 0.0) + jnp.log(1.0 + jnp.exp(-jnp.abs(w_in)))
    lw = -jnp.exp(-sp - 0.5)
    a = _sigmoid(vec_ref[1:2, :] + _dot(lora.astype(BF16), aup_ref[...]))
    g = _dot(_sigmoid(u[:, RW_GATE_OFF:]).astype(BF16), gup_ref[...])
    if has_vres:
        mix = _sigmoid(vb_ref[...] + _dot(u[:, RW_VRES_OFF:].astype(BF16), vup_ref[...]))
        v = v + (vf_ref[0] - v) * mix
    kk = k * vec_ref[2:3, :]
    ss = _dot_split_lhs(kk * kk, bd_ref[...], 3)
    kk = kk / jnp.maximum(jnp.sqrt(ss), 1e-12)
    k = k * (1.0 + (a - 1.0) * vec_ref[3:4, :])
    for val, ref in ((r, r_ref), (lw, lw_ref), (k, k_ref), (v, v_ref), (kk, kk_ref), (kk * a, kka_ref),
                     (g, g_ref)):
        ref[0] = val


def _pad_rows(w, n):
    return jnp.pad(w, ((0, n - w.shape[0]), (0, 0)))


def _rwkv_prep(p, mu, vec, w_up, a_up, g_up, v_first, vres_up, vres_b, ts):
    B, S, _ = p.shape
    C = RWKV_W
    has_vres = v_first is not None
    mu_p = jnp.pad(mu, (0, RWKV_IN_PAD - mu.shape[0])).reshape(1, RWKV_IN_PAD)
    n_lora = DECAY_LORA + ICLR_LORA
    wup = _pad_rows(w_up, n_lora).astype(BF16)
    aup = jnp.pad(a_up, ((DECAY_LORA, 0), (0, 0))).astype(BF16)
    gup = _pad_rows(g_up, RWKV_IN_PAD - RW_GATE_OFF).astype(BF16)
    bd = jnp.kron(jnp.eye(RWKV_HEADS, dtype=F32), jnp.ones((HEAD_DIM, HEAD_DIM), F32)).astype(BF16)
    const = lambda shape: pl.BlockSpec(shape, lambda b, i: (0,) * len(shape))
    dense = pl.BlockSpec((1, ts, C), lambda b, i: (b, i, 0))
    in_specs = [pl.BlockSpec((1, ts, RWKV_IN_PAD), lambda b, i: (b, i, 0)),
                pl.BlockSpec((1, 8, RWKV_IN_PAD), lambda b, i: (b, jnp.maximum(i * (ts // 8) - 1, 0), 0)),
                const((1, RWKV_IN_PAD)), const((8, C)), const((n_lora, C)), const((n_lora, C)),
                const((RWKV_IN_PAD - RW_GATE_OFF, C)), const((C, C))]
    args = [p, p, mu_p, jnp.pad(vec, ((0, 1), (0, 0))), wup, aup, gup, bd]
    if has_vres:
        off = RWKV_IN_PAD - LANE
        lo = 3 * C + n_lora + GATE_LORA - off
        vup = jnp.pad(vres_up, ((lo, LANE - lo - VRES_LORA), (0, 0))).astype(BF16)
        in_specs += [dense, const((1, C)), const((LANE, C))]
        args += [v_first, vres_b.reshape(1, C), vup]
    return pl.pallas_call(
        functools.partial(_rwkv_prep_kernel, ts=ts, has_vres=has_vres),
        grid=(B, S // ts),
        in_specs=in_specs,
        out_specs=[dense] * 7,
        out_shape=[jax.ShapeDtypeStruct((B, S, C), F32)] * 7,
        compiler_params=_params("parallel", "parallel"),
        name="rwkv_prep",
    )(*args)


GROUP_HEADS = 4
GW = GROUP_HEADS * HEAD_DIM
SR = GROUP_HEADS * CHUNK


def _split_bf16(x, n):
    parts = []
    for _ in range(n - 1):
        h = x.astype(BF16)
        parts.append(h)
        x = x - h.astype(F32)
    parts.append(x.astype(BF16))
    return parts


def _dot_split_rhs(a, x, n):
    parts = _split_bf16(x, n)
    out = _dot(a, parts[0])
    for p in parts[1:]:
        out = out + _dot(a, p)
    return out


def _dot_split_lhs(x, b, n):
    parts = _split_bf16(x, n)
    out = _dot(parts[0], b)
    for p in parts[1:]:
        out = out + _dot(p, b)
    return out


def _unit_lower_inverses(mats, eye, in16, in32):
    b = lambda m: m.astype(BF16)
    ds = [b(jnp.where(in16, a, 0.0)) for a in mats]
    xs = [eye - d for d in ds]
    ps = [b(_dot(d, d)) for d in ds]
    for step in range(3):
        xs = [x + _dot(b(x), p) for x, p in zip(xs, ps)]
        if step < 2:
            ps = [b(_dot(p, p)) for p in ps]
    for lowers in ([jnp.where(in32 & jnp.logical_not(in16), a, 0.0) for a in mats],
                   [jnp.where(in32, 0.0, a) for a in mats]):
        xbs = [b(x) for x in xs]
        mids = [b(_dot(xb, b(lo))) for xb, lo in zip(xbs, lowers)]
        xs = [x - _dot(mid, xb) for x, mid, xb in zip(xs, mids, xbs)]
    return xs


def _rwkv_scan_kernel(r_ref, lw_ref, k_ref, v_ref, kk_ref, kka_ref, g_ref, gn_ref, o_ref, state,
                      kap_s, bt_s, rt_s, arb_s, tinv_s, u_s, y0_s, sv_s, gl_s, y_s, *, tt):
    @pl.when(pl.program_id(1) == 0)
    def _():
        state[...] = jnp.zeros_like(state)

    ri = lax.broadcasted_iota(jnp.int32, (SR, GW), 0)
    ci = lax.broadcasted_iota(jnp.int32, (SR, GW), 1)
    same_head = (ri // CHUNK) == (ci // HEAD_DIM)
    strict = (ci % CHUNK) < (ri % CHUNK)
    incl = (ci % CHUNK) <= (ri % CHUNK)
    in16 = (ri // 16) == (ci // 16)
    in32 = (ri // 32) == (ci // 32)
    eye = jnp.where(ri == ci, 1.0, 0.0)
    head_ones = jnp.where(same_head, 1.0, 0.0).astype(BF16)
    tr = lax.broadcasted_iota(jnp.int32, (CHUNK, CHUNK), 0)
    tc = lax.broadcasted_iota(jnp.int32, (CHUNK, CHUNK), 1)
    tri = jnp.where(tc <= tr, 1.0, 0.0).astype(BF16)
    b = lambda m: m.astype(BF16)

    def stack(x):
        return b(jnp.where(same_head, jnp.concatenate([x] * GROUP_HEADS, axis=0), 0.0))

    n_groups = RWKV_HEADS // GROUP_HEADS

    def chunk_rows(c):
        return pl.ds(pl.multiple_of(c * CHUNK, CHUNK), CHUNK)

    def prepare_pair(it, carry):
        items = [(2 * it + u, gi) for u in range(2) for gi in range(n_groups)]
        n = range(len(items))
        kap, bt, kt, rt, vs, gl = [], [], [], [], [], []
        for c, gi in items:
            rows = chunk_rows(c)
            lanes = slice(gi * GW, (gi + 1) * GW)
            lw = lw_ref[0, rows, lanes]
            cs = _dot_split_rhs(tri, lw, 3)
            g_in = jnp.exp(cs)
            g_inv = jnp.exp(-cs)
            kap.append(stack(kk_ref[0, rows, lanes] * jnp.exp(cs - lw)))
            bt.append(stack(kka_ref[0, rows, lanes] * g_inv))
            kt.append(stack(k_ref[0, rows, lanes] * g_inv))
            rt.append(stack(r_ref[0, rows, lanes] * g_in))
            vs.append(stack(v_ref[0, rows, lanes]))
            gl.append(jnp.broadcast_to(g_in[CHUNK - 1:CHUNK, :], (8, GW)))
        a_kb = [jnp.where(strict, _dot_nt(kap[i], bt[i]), 0.0) for i in n]
        a_kk = [b(jnp.where(strict, _dot_nt(kap[i], kt[i]), 0.0)) for i in n]
        a_rk = [b(jnp.where(incl, _dot_nt(rt[i], kt[i]), 0.0)) for i in n]
        a_rb = [b(jnp.where(incl, _dot_nt(rt[i], bt[i]), 0.0)) for i in n]
        u = [_dot(a_kk[i], vs[i]) for i in n]
        y0 = [_dot(a_rk[i], vs[i]) for i in n]
        sv = [_dot_tn(vs[i], kt[i]) for i in n]
        t_inv = _unit_lower_inverses(a_kb, eye, in16, in32)
        for i, (c, gi) in enumerate(items):
            slot = c * n_groups + gi
            kap_s[slot] = kap[i]
            bt_s[slot] = bt[i]
            rt_s[slot] = rt[i]
            arb_s[slot] = a_rb[i]
            tinv_s[slot] = b(t_inv[i])
            u_s[slot] = u[i]
            y0_s[slot] = y0[i]
            sv_s[slot] = sv[i]
            gl_s[slot] = gl[i]
        return carry

    lax.fori_loop(0, tt // (2 * CHUNK), prepare_pair, 0)

    def advance(c, carry):
        gs = range(n_groups)
        slot = [c * n_groups + gi for gi in gs]
        s0 = [state[gi] for gi in gs]
        s0b = [b(s) for s in s0]
        rhs = [b(_dot_nt(kap_s[slot[gi]], s0b[gi]) + u_s[slot[gi]]) for gi in gs]
        pb = [b(_dot(tinv_s[slot[gi]], rhs[gi])) for gi in gs]
        for gi in gs:
            state[gi] = (s0[gi] + sv_s[slot[gi]] - _dot_tn(pb[gi], bt_s[slot[gi]])) * gl_s[slot[gi]][0:1, :]
        for gi in gs:
            y = _dot_nt(rt_s[slot[gi]], s0b[gi]) + y0_s[slot[gi]] - _dot(arb_s[slot[gi]], pb[gi])
            yd = y[0:CHUNK]
            for h in range(1, GROUP_HEADS):
                yd = yd + y[h * CHUNK:(h + 1) * CHUNK]
            y_s[chunk_rows(c), gi * GW:(gi + 1) * GW] = yd
        return carry

    lax.fori_loop(0, tt // CHUNK, advance, 0)

    def finish_pair(it, carry):
        items = [(chunk_rows(2 * it + u), slice(gi * GW, (gi + 1) * GW)) for u in range(2) for gi in range(n_groups)]
        yd = [y_s[rows, lanes] for rows, lanes in items]
        ym = [_dot_split_lhs(y, head_ones, 2) * (1.0 / HEAD_DIM) for y in yd]
        yc = [y - m for y, m in zip(yd, ym)]
        yv = [_dot_split_lhs(c * c, head_ones, 2) * (1.0 / HEAD_DIM) for c in yc]
        rk = [r_ref[0, rows, lanes] * k_ref[0, rows, lanes] * gn_ref[2:3, lanes] for rows, lanes in items]
        bonus = [_dot_split_lhs(x, head_ones, 2) for x in rk]
        for i, (rows, lanes) in enumerate(items):
            yn = yc[i] * lax.rsqrt(yv[i] + GN_EPS) * gn_ref[0:1, lanes] + gn_ref[1:2, lanes]
            o_ref[0, rows, lanes] = b((yn + bonus[i] * v_ref[0, rows, lanes]) * g_ref[0, rows, lanes])
        return carry

    lax.fori_loop(0, tt // (2 * CHUNK), finish_pair, 0)


def _rwkv_scan(r, lw, k, v, kk, kka, g, vec, tt):
    B, S, C = r.shape
    gn = jnp.pad(jnp.stack([vec[5], vec[6], vec[4]]), ((0, 5), (0, 0)))
    dense = pl.BlockSpec((1, tt, C), lambda b, i: (b, i, 0))
    slots = (tt // CHUNK) * (C // GW)
    assert tt % (2 * CHUNK) == 0
    stacked = lambda dtype: pltpu.VMEM((slots, SR, GW), dtype)
    return pl.pallas_call(
        functools.partial(_rwkv_scan_kernel, tt=tt),
        grid=(B, S // tt),
        in_specs=[dense] * 7 + [pl.BlockSpec((8, C), lambda b, i: (0, 0))],
        out_specs=dense,
        out_shape=jax.ShapeDtypeStruct((B, S, C), BF16),
        scratch_shapes=[pltpu.VMEM((C // GW, GW, GW), F32)] + [stacked(BF16)] * 5 + [stacked(F32)] * 3
                       + [pltpu.VMEM((slots, 8, GW), F32), pltpu.VMEM((tt, C), F32)],
        compiler_params=_params("parallel", "arbitrary"),
        name="rwkv_scan",
    )(r, lw, k, v, kk, kka, g, gn)


def _merge_kernel(x_ref, an_ref, ac_ref, ar_ref, gl_ref, wn_ref, wc_ref, wr_ref, wo_ref, g2_ref,
                  xo_ref, h_ref):
    D = x_ref.shape[1]
    o_n = _dot(an_ref[...].astype(BF16), wn_ref[...])
    mixed = _sigmoid(gl_ref[:, 0:D]) * o_n
    o_c = _dot(ac_ref[...], wc_ref[...])
    mixed = mixed + _sigmoid(gl_ref[:, D:2 * D]) * o_c
    o_r = _dot(ar_ref[...], wr_ref[...])
    mixed = mixed + _sigmoid(gl_ref[:, 2 * D:3 * D]) * o_r
    x = x_ref[...] + _dot(mixed.astype(BF16), wo_ref[...])
    xo_ref[...] = x
    ms = jnp.mean(x * x, axis=-1, keepdims=True)
    h_ref[...] = (x * lax.rsqrt(ms + RMS_EPS) * g2_ref[...]).astype(BF16)


def _merge(x, a_nsa, a_conv, a_rwkv, gl, w_n, w_c, w_r, w_o, g2, tm):
    T, D = x.shape
    row = lambda w: pl.BlockSpec((tm, w), lambda i: (i, 0))
    const = lambda shape: pl.BlockSpec(shape, lambda i: (0,) * len(shape))
    return pl.pallas_call(
        _merge_kernel,
        grid=(T // tm,),
        in_specs=[row(D), row(a_nsa.shape[1]), row(a_conv.shape[1]), row(a_rwkv.shape[1]), row(3 * D),
                  const(w_n.shape), const(w_c.shape), const(w_r.shape), const(w_o.shape), const((1, D))],
        out_specs=[row(D), row(D)],
        out_shape=[jax.ShapeDtypeStruct((T, D), F32), jax.ShapeDtypeStruct((T, D), BF16)],
        compiler_params=_params("parallel"),
        name="merge",
    )(x, a_nsa, a_conv, a_rwkv, gl, w_n.astype(BF16), w_c.astype(BF16), w_r.astype(BF16),
      w_o.astype(BF16), g2.reshape(1, D))


def _interleave_gate_up(w_gu, tf):
    *lead, d, two_ff = w_gu.shape
    nj = two_ff // (2 * tf)
    w = w_gu.astype(BF16).reshape(*lead, d, 2, nj, tf)
    return jnp.swapaxes(w, -3, -2).reshape(*lead, d, two_ff)


def _ffn_kernel(x_ref, h_ref, wgu_ref, wd_ref, o_ref):
    j = pl.program_id(1)
    h = h_ref[...]
    gu = _dot(h, wgu_ref[...])
    tf = gu.shape[1] // 2
    gate, up = gu[:, :tf], gu[:, tf:]
    act = (gate * _sigmoid(gate) * up).astype(BF16)
    y = _dot(act, wd_ref[...])

    @pl.when(j == 0)
    def _():
        o_ref[...] = x_ref[...] + y

    @pl.when(j > 0)
    def _():
        o_ref[...] += y


def _ffn(x, h, w_gu, w_down, tm, tf):
    T, D = x.shape
    FF = w_down.shape[0]
    wgu = _interleave_gate_up(w_gu, tf)
    return pl.pallas_call(
        _ffn_kernel,
        grid=(T // tm, FF // tf),
        in_specs=[pl.BlockSpec((tm, D), lambda i, j: (i, 0)),
                  pl.BlockSpec((tm, D), lambda i, j: (i, 0)),
                  pl.BlockSpec((D, 2 * tf), lambda i, j: (0, j)),
                  pl.BlockSpec((tf, D), lambda i, j: (j, 0))],
        out_specs=pl.BlockSpec((tm, D), lambda i, j: (i, 0)),
        out_shape=jax.ShapeDtypeStruct((T, D), F32),
        compiler_params=_params("parallel", "arbitrary"),
        name="ffn_dense",
    )(x, h, wgu, w_down.astype(BF16))


MOE_ROWS = 256
MOE_TAIL = 128
RANK_BLK = 256


def _router_kernel(h_ref, w_ref, b_ref, gw_ref, pos_ref, cnt_ref):
    logits = _dot(h_ref[...], w_ref[...]) + b_ref[...]
    tt = logits.shape[0]
    lane = lax.broadcasted_iota(jnp.int32, logits.shape, 1)
    m1 = jnp.max(logits, axis=-1, keepdims=True)
    i1 = jnp.min(jnp.where(logits == m1, lane, LANE), axis=-1, keepdims=True)
    rest = jnp.where(lane == i1, -jnp.inf, logits)
    m2 = jnp.max(rest, axis=-1, keepdims=True)
    i2 = jnp.min(jnp.where(rest == m2, lane, LANE), axis=-1, keepdims=True)
    e2 = jnp.exp(m2 - m1)
    w1 = 1.0 / (1.0 + e2)
    w2 = e2 / (1.0 + e2)
    gw_ref[...] = jnp.where(lane == i1, w1, jnp.where(lane == i2, w2, 0.0))
    routed = (lane == i1) | (lane == i2)
    r = lax.broadcasted_iota(jnp.int32, (RANK_BLK, RANK_BLK), 0)
    c = lax.broadcasted_iota(jnp.int32, (RANK_BLK, RANK_BLK), 1)
    before = jnp.where(c < r, 1.0, 0.0).astype(BF16)
    run = jnp.zeros((1, LANE), F32)
    ranks = []
    for blk in range(tt // RANK_BLK):
        hit = jnp.where(routed[blk * RANK_BLK:(blk + 1) * RANK_BLK], 1.0, 0.0)
        ranks.append(_dot(before, hit.astype(BF16)) + run)
        run = run + jnp.sum(hit, axis=0, keepdims=True)
    pos = jnp.where(routed, jnp.concatenate(ranks, axis=0), -1.0)
    pos_ref[0] = pos.T[0:N_EXPERTS]
    cnt_ref[0] = jnp.broadcast_to(run, (8, LANE))


def _router(h, router_w, router_b, tt):
    T, D = h.shape
    nt = T // tt
    w = _pad_cols(router_w, LANE).astype(BF16)
    b = jnp.pad(router_b, (0, LANE - N_EXPERTS), constant_values=-jnp.inf).reshape(1, LANE)
    return pl.pallas_call(
        _router_kernel,
        grid=(nt,),
        in_specs=[pl.BlockSpec((tt, D), lambda i: (i, 0)),
                  pl.BlockSpec((D, LANE), lambda i: (0, 0)),
                  pl.BlockSpec((1, LANE), lambda i: (0, 0))],
        out_specs=[pl.BlockSpec((tt, LANE), lambda i: (i, 0)),
                   pl.BlockSpec((1, N_EXPERTS, tt), lambda i: (i, 0, 0)),
                   pl.BlockSpec((1, 8, LANE), lambda i: (i, 0, 0))],
        out_shape=[jax.ShapeDtypeStruct((T, LANE), F32),
                   jax.ShapeDtypeStruct((nt, N_EXPERTS, tt), F32),
                   jax.ShapeDtypeStruct((nt, 8, LANE), F32)],
        compiler_params=_params("parallel"),
        name="moe_router",
    )(h, w, b)


def _moe_kernel(nsb_ref, x_ref, h_ref, gw_ref, pos_ref, wgu_ref, wd_ref, o_ref, xc, yc):
    i = pl.program_id(0)
    e = pl.program_id(1)
    j = pl.program_id(2)
    n_full = nsb_ref[0, i * pl.num_programs(1) + e]
    has_tail = nsb_ref[1, i * pl.num_programs(1) + e]

    @pl.when((e == 0) & (j == 0))
    def _():
        o_ref[...] = x_ref[...]

    pos = pos_ref[0, pl.ds(e, 1), :]

    def for_blocks(fn):
        def body(sb, carry):
            fn(pl.multiple_of(sb * MOE_ROWS, MOE_ROWS), MOE_ROWS)
            return carry
        lax.fori_loop(0, n_full, body, 0)

        @pl.when(has_tail == 1)
        def _():
            fn(pl.multiple_of(n_full * MOE_ROWS, MOE_TAIL), MOE_TAIL)

    def select(r0, n):
        slot = lax.broadcasted_iota(jnp.int32, (n, 1), 0) + r0
        return jnp.where(pos == slot.astype(F32), 1.0, 0.0).astype(BF16)

    def gather(r0, n):
        xc[pl.ds(r0, n), :] = _dot(select(r0, n), h_ref[...]).astype(BF16)
        yc[pl.ds(r0, n), :] = jnp.zeros((n, yc.shape[1]), F32)

    def expert(r0, n):
        gu = _dot(xc[pl.ds(r0, n), :], wgu_ref[0])
        tf = gu.shape[1] // 2
        gate, up = gu[:, :tf], gu[:, tf:]
        act = (gate * _sigmoid(gate) * up).astype(BF16)
        yc[pl.ds(r0, n), :] += _dot(act, wd_ref[0])

    @pl.when(j == 0)
    def _():
        for_blocks(gather)

    for_blocks(expert)

    @pl.when(j == pl.num_programs(2) - 1)
    def _():
        gw = gw_ref[...]
        lane = lax.broadcasted_iota(jnp.int32, gw.shape, 1)
        w_e = jnp.sum(jnp.where(lane == e, gw, 0.0), axis=-1, keepdims=True)

        def scatter(r0, n):
            o_ref[...] += w_e * _dot_tn(select(r0, n), yc[pl.ds(r0, n), :].astype(BF16))
        for_blocks(scatter)


def _moe(x, h, gw, pos, counts, w_gu, w_down, tt, tf):
    T, D = x.shape
    E, FF, _ = w_down.shape
    wgu = _interleave_gate_up(w_gu, tf)
    n_tail = ((counts[:, 0, :E].astype(jnp.int32) + MOE_TAIL - 1) // MOE_TAIL).reshape(-1)
    per_full = MOE_ROWS // MOE_TAIL
    n_sb = jnp.stack([n_tail // per_full, n_tail % per_full])
    once = dict(pipeline_mode=pl.Buffered(1))
    return pl.pallas_call(
        _moe_kernel,
        grid_spec=pltpu.PrefetchScalarGridSpec(
            num_scalar_prefetch=1,
            grid=(T // tt, E, FF // tf),
            in_specs=[pl.BlockSpec((tt, D), lambda i, e, j, n: (i, 0), **once),
                      pl.BlockSpec((tt, D), lambda i, e, j, n: (i, 0), **once),
                      pl.BlockSpec((tt, LANE), lambda i, e, j, n: (i, 0)),
                      pl.BlockSpec((1, E, tt), lambda i, e, j, n: (i, 0, 0)),
                      pl.BlockSpec((1, D, 2 * tf), lambda i, e, j, n: (e, 0, j)),
                      pl.BlockSpec((1, tf, D), lambda i, e, j, n: (e, j, 0))],
            out_specs=pl.BlockSpec((tt, D), lambda i, e, j, n: (i, 0), **once),
            scratch_shapes=[pltpu.VMEM((tt, D), BF16), pltpu.VMEM((tt, D), F32)]),
        out_shape=jax.ShapeDtypeStruct((T, D), F32),
        compiler_params=_params("parallel", "arbitrary", "arbitrary"),
        name="moe_experts",
    )(n_sb, x, h, gw, pos, wgu, w_down.astype(BF16))


def _pad_cols(w, n):
    return jnp.pad(w, ((0, 0), (0, n - w.shape[1])))


def kernel(x, mix_norm_g, w_in_first, w_in_rest, rwkv_mu_first, rwkv_mu_rest, nsa_qk_g, nsa_cmp_pos, nsa_cmp_w1, nsa_cmp_w2, nsa_w_o, conv_w, conv_vec, conv_w_o, rwkv_vec, rwkv_w_up, rwkv_a_up, rwkv_g_up, rwkv_vres_up, rwkv_vres_b, rwkv_w_o, w_out, ffn_norm_g, dense_w_gu, dense_w_down, router_w, router_b, moe_w_gu, moe_w_down):
    B, S, D = x.shape
    T = B * S
    depth = mix_norm_g.shape[0]
    off_conv = NSA_IN
    off_gate = off_conv + 2 * CONV_CH
    off_rwkv = off_gate + 3 * D
    tm = min(512, T)
    xt = x.reshape(T, D)
    v_first = None
    for layer in range(depth):
        if layer == 0:
            w_in, mu, vres_up, vres_b = w_in_first, rwkv_mu_first, None, None
        else:
            w_in, mu = w_in_rest[layer - 1], rwkv_mu_rest[layer - 1]
            vres_up, vres_b = rwkv_vres_up[layer - 1], rwkv_vres_b[layer - 1]
        g1 = mix_norm_g[layer]
        w_nsa = _pad_cols(w_in[:, :off_conv], NSA_IN_PAD).astype(BF16)
        w_conv = w_in[:, off_conv:off_gate].astype(BF16)
        w_gate = w_in[:, off_gate:off_rwkv].astype(BF16)
        w_rwkv = _pad_cols(w_in[:, off_rwkv:], RWKV_IN_PAD).astype(BF16)
        p_nsa, p_conv, p_gate, p_rwkv = _rms_proj(xt, g1, [w_nsa, w_conv, w_gate, w_rwkv], min(256, T))
        p_nsa = p_nsa.reshape(B, S, NSA_IN_PAD)
        p_conv = p_conv.reshape(B, S, 2 * CONV_CH)
        p_rwkv = p_rwkv.reshape(B, S, RWKV_IN_PAD)

        a_nsa = _nsa_branch(p_nsa, nsa_qk_g[layer], nsa_cmp_pos[layer], nsa_cmp_w1[layer],
                            nsa_cmp_w2[layer], B, S)
        a_conv = _conformer(p_conv, conv_w[layer], conv_vec[layer], ts=min(512, S))
        r, lw, k, v, kk, kka, g = _rwkv_prep(
            p_rwkv, mu, rwkv_vec[layer], rwkv_w_up[layer], rwkv_a_up[layer], rwkv_g_up[layer],
            v_first, vres_up, vres_b, ts=min(256, S))
        if layer == 0:
            v_first = v
        a_rwkv = _rwkv_scan(r, lw, k, v, kk, kka, g, rwkv_vec[layer], tt=min(512, S))

        xt, h2 = _merge(xt, a_nsa.reshape(T, NSA_Q_W), a_conv.reshape(T, CONV_CH),
                        a_rwkv.reshape(T, RWKV_W), p_gate, nsa_w_o[layer], conv_w_o[layer],
                        rwkv_w_o[layer], w_out[layer], ffn_norm_g[layer], tm)
        if layer % 2 == 0:
            xt = _ffn(xt, h2, dense_w_gu[layer // 2], dense_w_down[layer // 2], min(512, T), 1408)
        else:
            tt = min(2048, T)
            gw, pos, counts = _router(h2, router_w[layer // 2], router_b[layer // 2], tt)
            xt = _moe(xt, h2, gw, pos, counts, moe_w_gu[layer // 2], moe_w_down[layer // 2], tt, 896)
    return xt.reshape(B, S, D)
```

```python
import functools

import jax
import jax.numpy as jnp
from jax import lax
from jax.experimental import pallas as pl
from jax.experimental.pallas import tpu as pltpu

F32 = jnp.float32
BF16 = jnp.bfloat16
HI = lax.Precision.HIGHEST

NSA_HEADS = 8
NSA_KV_HEADS = 2
NSA_GROUP = NSA_HEADS // NSA_KV_HEADS
HEAD_DIM = 64
NSA_Q_W = NSA_HEADS * HEAD_DIM
NSA_KV_W = NSA_KV_HEADS * HEAD_DIM
CMP_STRIDE = 16
CMP_LEN = 32
SEL_BLOCK = 64
N_SEL = 8
WINDOW = 512
SEL_FORCE = 1e4
CONV_CH = 512
CONV_K = 31
RWKV_HEADS = 8
RWKV_W = RWKV_HEADS * HEAD_DIM
DECAY_LORA = 64
ICLR_LORA = 64
GATE_LORA = 160
VRES_LORA = 32
N_EXPERTS = 8
ROPE_THETA = 10000.0
RMS_EPS = 1e-6
LN_EPS = 1e-5
GN_EPS = 64e-5
NEG_INF = -1e30
LOG2E = 1.4426950408889634
ACC_ROWS = 72

NSA_IN = NSA_Q_W + 6 * NSA_KV_W + 3 * NSA_HEADS
NSA_IN_PAD = 1408
NSA_GATE_BLK = (NSA_Q_W + 6 * NSA_KV_W) // 128
RWKV_IN_PAD = 1920
LANE = 128
VMEM_LIMIT = 56 * 1024 * 1024

CHUNK = 64


def _dot(a, b, precision=None):
    return jnp.dot(a, b, preferred_element_type=F32, precision=precision)


def _dot_nt(a, b, precision=None):
    return lax.dot_general(a, b, (((1,), (1,)), ((), ())), preferred_element_type=F32,
                           precision=precision)


def _dot_tn(a, b, precision=None):
    return lax.dot_general(a, b, (((0,), (0,)), ((), ())), preferred_element_type=F32,
                           precision=precision)


def _sigmoid(x):
    return 1.0 / (1.0 + jnp.exp(-x))


def _params(*sem):
    return pltpu.CompilerParams(dimension_semantics=sem, vmem_limit_bytes=VMEM_LIMIT)


def _rms_proj_kernel(x_ref, g_ref, *refs):
    n = len(refs) // 2
    x = x_ref[...]
    ms = jnp.mean(x * x, axis=-1, keepdims=True)
    h = (x * lax.rsqrt(ms + RMS_EPS) * g_ref[...]).astype(BF16)
    for w_ref, o_ref in zip(refs[:n], refs[n:]):
        o_ref[...] = _dot(h, w_ref[...])


def _rms_proj(x, g, ws, tm):
    T, D = x.shape
    once = dict(pipeline_mode=pl.Buffered(1))
    return pl.pallas_call(
        _rms_proj_kernel,
        grid=(T // tm,),
        in_specs=[pl.BlockSpec((tm, D), lambda i: (i, 0)),
                  pl.BlockSpec((1, D), lambda i: (0, 0))]
                 + [pl.BlockSpec(w.shape, lambda i: (0, 0), **once) for w in ws],
        out_specs=[pl.BlockSpec((tm, w.shape[1]), lambda i: (i, 0)) for w in ws],
        out_shape=[jax.ShapeDtypeStruct((T, w.shape[1]), F32) for w in ws],
        compiler_params=_params("parallel"),
        name="rms_proj",
    )(x, g.reshape(1, D), *ws)


def _rope_dense(x, cos, sin_signed):
    w = x.shape[1]
    lane = lax.broadcasted_iota(jnp.int32, x.shape, 1)
    first_half = (lane % HEAD_DIM) < (HEAD_DIM // 2)
    rot = jnp.where(first_half, pltpu.roll(x, w - HEAD_DIM // 2, 1), pltpu.roll(x, HEAD_DIM // 2, 1))
    return x * cos + rot * sin_signed


def _head_rms(x, bd, g):
    ms = _dot_split_lhs(x * x, bd, 3)
    return x * lax.rsqrt(ms + RMS_EPS) * g


def _nsa_prep_kernel(p_ref, cos_ref, sin_ref, gq_ref, gks_ref, gkw_ref, bd_ref,
                     q_ref, kc_ref, vc_ref, ks_ref, vs_ref, kw_ref, vw_ref):
    cos = cos_ref[...]
    sin = sin_ref[...]
    bd = bd_ref[...]
    q = p_ref[0, :, 0:NSA_Q_W]
    q = _rope_dense(_head_rms(q, bd, gq_ref[...]), cos, sin) * (HEAD_DIM ** -0.5 * LOG2E)
    for h in range(NSA_HEADS):
        q_ref[0, h] = q[:, h * HEAD_DIM:(h + 1) * HEAD_DIM].astype(BF16)

    def slab(i):
        return p_ref[0, :, NSA_Q_W + i * NSA_KV_W:NSA_Q_W + (i + 1) * NSA_KV_W]

    cos_k = cos[:, :NSA_KV_W]
    sin_k = sin[:, :NSA_KV_W]
    bd_k = bd[:NSA_KV_W, :NSA_KV_W]
    ks = _rope_dense(_head_rms(slab(2), bd_k, gks_ref[...]), cos_k, sin_k)
    kw = _rope_dense(_head_rms(slab(4), bd_k, gkw_ref[...]), cos_k, sin_k)
    for val, ref in ((slab(0), kc_ref), (slab(1), vc_ref), (ks, ks_ref), (kw, kw_ref)):
        for g in range(NSA_KV_HEADS):
            ref[0, g] = val[:, g * HEAD_DIM:(g + 1) * HEAD_DIM].astype(BF16)
    lane = lax.broadcasted_iota(jnp.int32, (1, NSA_KV_W), 1)
    tail = jnp.where(lane == HEAD_DIM, 1.0, 0.0)
    for val, ref in ((slab(3), vs_ref), (slab(5), vw_ref)):
        for g in range(NSA_KV_HEADS):
            head_first = val if g == 0 else pltpu.roll(val, NSA_KV_W - g * HEAD_DIM, 1)
            ref[0, g] = jnp.where(lane < HEAD_DIM, head_first, tail).astype(BF16)


def _nsa_prep(p, cos, sin, qk_g, ts):
    B, S, _ = p.shape
    bd = jnp.kron(jnp.eye(NSA_HEADS, dtype=F32), jnp.full((HEAD_DIM, HEAD_DIM), 1.0 / HEAD_DIM, F32)).astype(BF16)
    gq = jnp.tile(qk_g[0], NSA_HEADS).reshape(1, NSA_Q_W)
    gks = jnp.tile(qk_g[2], NSA_KV_HEADS).reshape(1, NSA_KV_W)
    gkw = jnp.tile(qk_g[3], NSA_KV_HEADS).reshape(1, NSA_KV_W)
    kv_shape = jax.ShapeDtypeStruct((B, NSA_KV_HEADS, S, HEAD_DIM), BF16)
    kv_spec = pl.BlockSpec((1, NSA_KV_HEADS, ts, HEAD_DIM), lambda b, i: (b, 0, i, 0))
    aug_shape = jax.ShapeDtypeStruct((B, NSA_KV_HEADS, S, NSA_KV_W), BF16)
    aug_spec = pl.BlockSpec((1, NSA_KV_HEADS, ts, NSA_KV_W), lambda b, i: (b, 0, i, 0))
    const = lambda shape: pl.BlockSpec(shape, lambda b, i: (0,) * len(shape))
    return pl.pallas_call(
        _nsa_prep_kernel,
        grid=(B, S // ts),
        in_specs=[pl.BlockSpec((1, ts, NSA_IN_PAD), lambda b, i: (b, i, 0)),
                  pl.BlockSpec((ts, NSA_Q_W), lambda b, i: (i, 0)),
                  pl.BlockSpec((ts, NSA_Q_W), lambda b, i: (i, 0)),
                  const((1, NSA_Q_W)), const((1, NSA_KV_W)), const((1, NSA_KV_W)),
                  const((NSA_Q_W, NSA_Q_W))],
        out_specs=[pl.BlockSpec((1, NSA_HEADS, ts, HEAD_DIM), lambda b, i: (b, 0, i, 0)),
                   kv_spec, kv_spec, kv_spec, aug_spec, kv_spec, aug_spec],
        out_shape=[jax.ShapeDtypeStruct((B, NSA_HEADS, S, HEAD_DIM), BF16),
                   kv_shape, kv_shape, kv_shape, aug_shape, kv_shape, aug_shape],
        compiler_params=_params("parallel", "parallel"),
        name="nsa_prep",
    )(p, cos, sin, gq, gks, gkw, bd)


def _gelu_tanh(x):
    return 0.5 * x * (1.0 + jnp.tanh(0.7978845608028654 * (x + 0.044715 * x * x * x)))


def _compress_kernel(a_ref, posa_ref, posb_ref, w1a_ref, w1b_ref, w2_ref, g_ref, cos_ref, sin_ref,
                     perm_ref, o_ref, *, is_key):
    a = a_ref[0]
    ncp = a.shape[0]
    w1a = w1a_ref[...]
    w1b = w1b_ref[...]
    p1 = _dot(a, w1a)
    p2 = _dot(a, w1b)
    hp = _dot(posa_ref[...], w1a) + _dot(posb_ref[...], w1b)
    h = _gelu_tanh(p1 + pltpu.roll(p2, ncp - 1, 0) + hp[0:1])
    o = _dot(h.astype(BF16), w2_ref[...])
    if is_key:
        ms = jnp.mean(o * o, axis=-1, keepdims=True)
        o = o * lax.rsqrt(ms + RMS_EPS) * g_ref[...]
        o = o * cos_ref[...] + _dot(o, perm_ref[...], precision=HI) * sin_ref[...]
    row = lax.broadcasted_iota(jnp.int32, o.shape, 0)
    o_ref[0] = jnp.where(row < ncp - 1, o, 0.0).astype(BF16)


def _compress(t, pos_emb, w1, w2, g, cos_c, sin_c, is_key):
    BG, ncp, cw = t.shape
    hid = w1.shape[1]
    pos = pos_emb.reshape(1, CMP_LEN * HEAD_DIM)
    posa = jnp.tile(pos[:, :cw], (8, 1)).astype(BF16)
    posb = jnp.tile(pos[:, cw:], (8, 1)).astype(BF16)
    half = HEAD_DIM // 2
    perm = jnp.roll(jnp.eye(HEAD_DIM, dtype=F32), half, axis=0)
    const = lambda shape: pl.BlockSpec(shape, lambda b: (0,) * len(shape))
    return pl.pallas_call(
        functools.partial(_compress_kernel, is_key=is_key),
        grid=(BG,),
        in_specs=[pl.BlockSpec((1, ncp, cw), lambda b: (b, 0, 0)),
                  const((8, cw)), const((8, cw)), const((cw, hid)), const((cw, hid)),
                  const((hid, HEAD_DIM)), const((1, HEAD_DIM)),
                  const((ncp, HEAD_DIM)), const((ncp, HEAD_DIM)), const((HEAD_DIM, HEAD_DIM))],
        out_specs=pl.BlockSpec((1, ncp, HEAD_DIM), lambda b: (b, 0, 0)),
        out_shape=jax.ShapeDtypeStruct((BG, ncp, HEAD_DIM), BF16),
        compiler_params=_params("parallel"),
        name="nsa_compress_k" if is_key else "nsa_compress_v",
    )(t, posa, posb, w1[:cw].astype(BF16), w1[cw:].astype(BF16), w2.astype(BF16),
      g.reshape(1, HEAD_DIM), cos_c, sin_c, perm)


def _exp_cols(s):
    m = jnp.max(s, axis=0, keepdims=True)
    e = jnp.exp2(s - m)
    return e, jnp.sum(e, axis=0, keepdims=True)


def _heads_to_rows(o, tq):
    return jnp.concatenate([o[:, h * tq:(h + 1) * tq] for h in range(NSA_GROUP)], axis=0)


def _nsa_attn_kernel(q_ref, kc_ref, vc_ref, ks_ref, vs_ref, kw_ref, vw_ref, gl_ref, o_ref, selb_ref, s_ref,
                     *, tq, tk, seq):
    g = pl.program_id(1)
    i = pl.program_id(2)
    rows = NSA_GROUP * tq
    ncp = seq // CMP_STRIDE
    ns = seq // SEL_BLOCK
    n_sel = min(N_SEL, ns)
    q = q_ref[0].reshape(rows, HEAD_DIM)
    t0 = i * tq
    t_q = t0 + lax.broadcasted_iota(jnp.int32, (1, tq), 1)

    def per_head(x):
        return jnp.concatenate([x] * NSA_GROUP, axis=1)

    c_end = lax.broadcasted_iota(jnp.int32, (ncp, 1), 0) * CMP_STRIDE + (CMP_LEN - 1)
    bias = jnp.where(c_end <= t_q, 0.0, NEG_INF)
    e, l = _exp_cols(_dot_nt(kc_ref[0], q) + per_head(bias))
    p_c = e * (per_head(jnp.where(t_q >= CMP_LEN - 1, 1.0, 0.0)) / l)
    o_c = _dot_tn(vc_ref[0], p_c.astype(BF16))

    p_sum = p_c[:, 0:tq]
    for h in range(1, NSA_GROUP):
        p_sum = p_sum + p_c[:, h * tq:(h + 1) * tq]
    s0 = lax.broadcasted_iota(jnp.int32, (ns, ncp), 0) * SEL_BLOCK
    c0 = lax.broadcasted_iota(jnp.int32, (ns, ncp), 1) * CMP_STRIDE
    overlap = jnp.where((c0 < s0 + SEL_BLOCK) & (c0 + CMP_LEN > s0), 1.0, 0.0).astype(BF16)
    imp = _dot_split_rhs(overlap, p_sum, 2)
    j = lax.broadcasted_iota(jnp.int32, (ns, tq), 0).astype(F32)
    cur = (t_q // SEL_BLOCK).astype(F32)
    forced = (j == 0.0) | (j == cur) | (j == cur - 1.0)
    score = jnp.where(forced, SEL_FORCE, jnp.where(j <= cur, imp, -1.0))
    for _ in range(n_sel):
        mx = jnp.max(score, axis=0, keepdims=True)
        first = jnp.min(jnp.where(score == mx, j, float(ns)), axis=0, keepdims=True)
        score = jnp.where(j == first, -jnp.inf, score)
    selb_ref[...] = jnp.where(score == -jnp.inf, 0.0, NEG_INF)

    wlen = WINDOW + tq
    w0 = pl.multiple_of(jnp.maximum(t0 - WINDOW, 0), tq)
    kpos = w0 + lax.broadcasted_iota(jnp.int32, (wlen, 1), 0)
    bias = jnp.where((kpos <= t_q) & (kpos > t_q - WINDOW), 0.0, NEG_INF)
    s = _dot_nt(kw_ref[0, 0, pl.ds(w0, wlen), :], q) + per_head(bias)
    e = jnp.exp2(s - jnp.max(s, axis=0, keepdims=True))
    num = _dot_tn(vw_ref[0, 0, pl.ds(w0, wlen), :], e.astype(BF16))[0:ACC_ROWS]
    o_w = num[0:HEAD_DIM] / num[HEAD_DIM:HEAD_DIM + 1]

    key_off = lax.broadcasted_iota(jnp.int32, (SEL_BLOCK, 1), 0)

    last_tile = seq // tk - 1

    def scores(jt, slot):
        jt = jnp.minimum(jt, last_tile)
        k0 = pl.multiple_of(jt * tk, tk)
        bias = []
        for b in range(tk // SEL_BLOCK):
            picked = selb_ref[pl.ds(jt * (tk // SEL_BLOCK) + b, 1), :]
            causal = k0 + b * SEL_BLOCK + key_off <= t_q
            bias.append(jnp.where(causal, picked, NEG_INF))
        bias = jnp.concatenate(bias, axis=0)
        s_ref[slot] = _dot_nt(ks_ref[0, 0, pl.ds(k0, tk), :], q) + per_head(bias)

    def absorb(jt, slot, carry):
        m_run, acc = carry
        k0 = pl.multiple_of(jt * tk, tk)
        s = s_ref[slot]
        m_new = jnp.maximum(m_run, jnp.max(s, axis=0, keepdims=True))
        e = jnp.exp2(s - m_new)
        alpha = jnp.exp2(m_run - m_new)
        acc = alpha * acc + _dot_tn(vs_ref[0, 0, pl.ds(k0, tk), :], e.astype(BF16))[0:ACC_ROWS]
        return m_new, acc

    def sel_pair(it, carry):
        scores(2 * it + 1, 1)
        carry = absorb(2 * it, 0, carry)
        scores(2 * it + 2, 0)
        return absorb(2 * it + 1, 1, carry)

    n_tiles = (t0 + tq - 1) // tk + 1
    init = (jnp.full((1, rows), NEG_INF, F32), jnp.zeros((ACC_ROWS, rows), F32))
    scores(0, 0)
    _, acc_s = lax.fori_loop(0, (n_tiles + 1) // 2, sel_pair, init)
    o_s = acc_s[0:HEAD_DIM] / acc_s[HEAD_DIM:HEAD_DIM + 1]

    gw = NSA_GROUP * HEAD_DIM
    n = lax.broadcasted_iota(jnp.int32, (3 * gw, LANE), 0)
    col = lax.broadcasted_iota(jnp.int32, (3 * gw, LANE), 1)
    head = g * NSA_GROUP + (n % gw) // HEAD_DIM
    spread = jnp.where(col == head * 3 + n // gw, 1.0, 0.0).astype(BF16)
    gates = None
    for piece in _split_bf16(_sigmoid(gl_ref[0]), 3):
        part = _dot_nt(spread, piece)
        gates = part if gates is None else gates + part
    out = gates[0:gw] * _heads_to_rows(o_c, tq)
    out = out + gates[gw:2 * gw] * _heads_to_rows(o_s, tq)
    out = out + gates[2 * gw:] * _heads_to_rows(o_w, tq)
    o_ref[0] = out.T


def _nsa_attn(q, kc, vc, ks, vs, kw, vw, p_nsa, tq, tk):
    B, H, S, _ = q.shape
    G = NSA_KV_HEADS
    ncp = S // CMP_STRIDE
    assert S % (2 * tk) == 0
    cmp_spec = pl.BlockSpec((1, ncp, HEAD_DIM), lambda b, g, i: (b * G + g, 0, 0))
    seq_spec = pl.BlockSpec((1, 1, S, HEAD_DIM), lambda b, g, i: (b, g, 0, 0))
    aug_spec = pl.BlockSpec((1, 1, S, NSA_KV_W), lambda b, g, i: (b, g, 0, 0))
    return pl.pallas_call(
        functools.partial(_nsa_attn_kernel, tq=tq, tk=tk, seq=S),
        grid=(B, G, S // tq),
        in_specs=[pl.BlockSpec((1, NSA_GROUP, tq, HEAD_DIM), lambda b, g, i: (b, g, i, 0)),
                  cmp_spec, cmp_spec, seq_spec, aug_spec, seq_spec, aug_spec,
                  pl.BlockSpec((1, tq, LANE), lambda b, g, i: (b, i, NSA_GATE_BLK))],
        out_specs=pl.BlockSpec((1, tq, NSA_GROUP * HEAD_DIM), lambda b, g, i: (b, i, g)),
        out_shape=jax.ShapeDtypeStruct((B, S, NSA_Q_W), F32),
        scratch_shapes=[pltpu.VMEM((S // SEL_BLOCK, tq), F32),
                        pltpu.VMEM((2, tk, NSA_GROUP * tq), F32)],
        compiler_params=_params("parallel", "parallel", "arbitrary"),
        name="nsa_attn",
    )(q, kc, vc, ks, vs, kw, vw, p_nsa)


def _rope_tables(S):
    half = HEAD_DIM // 2
    inv = ROPE_THETA ** (-jnp.arange(half, dtype=F32) / half)
    ang = jnp.arange(S, dtype=F32)[:, None] * inv[None, :]
    cos, sin = jnp.cos(ang), jnp.sin(ang)
    cos64 = jnp.concatenate([cos, cos], axis=1)
    sin64 = jnp.concatenate([-sin, sin], axis=1)
    return cos64, sin64


def _nsa_branch(p_nsa, qk_g, cmp_pos, cmp_w1, cmp_w2, B, S):
    cos64, sin64 = _rope_tables(S)
    cos = jnp.tile(cos64, (1, NSA_HEADS))
    sin = jnp.tile(sin64, (1, NSA_HEADS))
    q, kc, vc, ks, vs, kw, vw = _nsa_prep(p_nsa, cos, sin, qk_g, ts=min(256, S))
    ncp = S // CMP_STRIDE
    cmp_end = jnp.minimum(jnp.arange(ncp) * CMP_STRIDE + CMP_LEN - 1, S - 1)
    cw = CMP_STRIDE * HEAD_DIM
    kc = kc.reshape(B * NSA_KV_HEADS, ncp, cw)
    vc = vc.reshape(B * NSA_KV_HEADS, ncp, cw)
    k_cmp = _compress(kc, cmp_pos[0], cmp_w1[0], cmp_w2[0], qk_g[1], cos64[cmp_end], sin64[cmp_end], True)
    v_cmp = _compress(vc, cmp_pos[1], cmp_w1[1], cmp_w2[1], qk_g[1], cos64[cmp_end], sin64[cmp_end], False)
    return _nsa_attn(q, k_cmp, v_cmp, ks, vs, kw, vw, p_nsa, tq=256, tk=256)


CONV_HALO = 32


def _conv_kernel(cur_ref, halo_ref, w_ref, vec_ref, o_ref, buf, y_scr, shifted, *, ts):
    i = pl.program_id(1)

    def glu(p):
        return p[:, :CONV_CH] * _sigmoid(p[:, CONV_CH:])

    buf[0:CONV_HALO, :] = jnp.where(i > 0, glu(halo_ref[0]), 0.0)
    buf[CONV_HALO:, :] = glu(cur_ref[0])
    rt = min(ts, 128)
    shift = CONV_HALO - (CONV_K - 1)
    for c in range(CONV_CH // LANE):
        cs = slice(c * LANE, (c + 1) * LANE)
        for b in range(8):
            span = ts + 8 * ((CONV_K - 1 - b) // 8)
            shifted[b, 0:span, :] = buf[shift + b:shift + b + span, cs]
        for r in range(ts // rt):
            acc = jnp.zeros((rt, LANE), F32)
            for j in range(CONV_K):
                acc = acc + w_ref[j:j + 1, cs] * shifted[j % 8, r * rt + 8 * (j // 8):r * rt + 8 * (j // 8) + rt, :]
            y_scr[r * rt:(r + 1) * rt, cs] = acc
    y = y_scr[...] + vec_ref[0:1, :]
    mu = jnp.mean(y, axis=-1, keepdims=True)
    yc = y - mu
    var = jnp.mean(yc * yc, axis=-1, keepdims=True)
    y = yc * lax.rsqrt(var + LN_EPS) * vec_ref[1:2, :] + vec_ref[2:3, :]
    o_ref[0] = (y * _sigmoid(y)).astype(BF16)


def _conformer(p, conv_w, conv_vec, ts):
    B, S, _ = p.shape
    w = jnp.pad(conv_w, ((0, 32 - CONV_K), (0, 0)))
    vec = jnp.pad(conv_vec, ((0, 5), (0, 0)))
    hb = ts // CONV_HALO
    return pl.pallas_call(
        functools.partial(_conv_kernel, ts=ts),
        grid=(B, S // ts),
        in_specs=[pl.BlockSpec((1, ts, 2 * CONV_CH), lambda b, i: (b, i, 0)),
                  pl.BlockSpec((1, CONV_HALO, 2 * CONV_CH), lambda b, i: (b, jnp.maximum(i * hb - 1, 0), 0)),
                  pl.BlockSpec((32, CONV_CH), lambda b, i: (0, 0)),
                  pl.BlockSpec((8, CONV_CH), lambda b, i: (0, 0))],
        out_specs=pl.BlockSpec((1, ts, CONV_CH), lambda b, i: (b, i, 0)),
        out_shape=jax.ShapeDtypeStruct((B, S, CONV_CH), BF16),
        scratch_shapes=[pltpu.VMEM((CONV_HALO + ts, CONV_CH), F32), pltpu.VMEM((ts, CONV_CH), F32),
                        pltpu.VMEM((8, ts + 8 * ((CONV_K - 1) // 8), LANE), F32)],
        compiler_params=_params("parallel", "parallel"),
        name="conformer",
    )(p, p, w, vec)


RW_LORA_OFF = 3 * RWKV_W
RW_GATE_OFF = RW_LORA_OFF + DECAY_LORA + ICLR_LORA
RW_VRES_OFF = RWKV_IN_PAD - LANE


def _rwkv_prep_kernel(*refs, ts, has_vres):
    if has_vres:
        (cur_ref, halo_ref, mu_ref, vec_ref, wup_ref, aup_ref, gup_ref, bd_ref, vf_ref, vb_ref, vup_ref,
         r_ref, lw_ref, k_ref, v_ref, kk_ref, kka_ref, g_ref) = refs
    else:
        (cur_ref, halo_ref, mu_ref, vec_ref, wup_ref, aup_ref, gup_ref, bd_ref,
         r_ref, lw_ref, k_ref, v_ref, kk_ref, kka_ref, g_ref) = refs
    i = pl.program_id(1)
    p = cur_ref[0]
    last = jnp.where(i > 0, halo_ref[0, 7:8, :], 0.0)
    row = lax.broadcasted_iota(jnp.int32, p.shape, 0)
    prev = jnp.where(row == 0, last, pltpu.roll(p, 1, 0))
    u = p + (prev - p) * mu_ref[...]
    C = RWKV_W
    r, k, v = u[:, :C], u[:, C:2 * C], u[:, 2 * C:3 * C]
    lora = u[:, RW_LORA_OFF:RW_GATE_OFF]
    w_in = vec_ref[0:1, :] + _dot(jnp.tanh(lora).astype(BF16), wup_ref[...])
    sp = jnp.maximum(-w_in, 0.0) + jnp.log(1.0 + jnp.exp(-jnp.abs(w_in)))
    lw = -jnp.exp(-sp - 0.5)
    a = _sigmoid(vec_ref[1:2, :] + _dot(lora.astype(BF16), aup_ref[...]))
    g = _dot(_sigmoid(u[:, RW_GATE_OFF:]).astype(BF16), gup_ref[...])
    if has_vres:
        mix = _sigmoid(vb_ref[...] + _dot(u[:, RW_VRES_OFF:].astype(BF16), vup_ref[...]))
        v = v + (vf_ref[0] - v) * mix
    kk = k * vec_ref[2:3, :]
    ss = _dot_split_lhs(kk * kk, bd_ref[...], 3)
    kk = kk / jnp.maximum(jnp.sqrt(ss), 1e-12)
    k = k * (1.0 + (a - 1.0) * vec_ref[3:4, :])
    for val, ref in ((r, r_ref), (lw, lw_ref), (k, k_ref), (v, v_ref), (kk, kk_ref), (kk * a, kka_ref),
                     (g, g_ref)):
        ref[0] = val


def _pad_rows(w, n):
    return jnp.pad(w, ((0, n - w.shape[0]), (0, 0)))


def _rwkv_prep(p, mu, vec, w_up, a_up, g_up, v_first, vres_up, vres_b, ts):
    B, S, _ = p.shape
    C = RWKV_W
    has_vres = v_first is not None
    mu_p = jnp.pad(mu, (0, RWKV_IN_PAD - mu.shape[0])).reshape(1, RWKV_IN_PAD)
    n_lora = DECAY_LORA + ICLR_LORA
    wup = _pad_rows(w_up, n_lora).astype(BF16)
    aup = jnp.pad(a_up, ((DECAY_LORA, 0), (0, 0))).astype(BF16)
    gup = _pad_rows(g_up, RWKV_IN_PAD - RW_GATE_OFF).astype(BF16)
    bd = jnp.kron(jnp.eye(RWKV_HEADS, dtype=F32), jnp.ones((HEAD_DIM, HEAD_DIM), F32)).astype(BF16)
    const = lambda shape: pl.BlockSpec(shape, lambda b, i: (0,) * len(shape))
    dense = pl.BlockSpec((1, ts, C), lambda b, i: (b, i, 0))
    in_specs = [pl.BlockSpec((1, ts, RWKV_IN_PAD), lambda b, i: (b, i, 0)),
                pl.BlockSpec((1, 8, RWKV_IN_PAD), lambda b, i: (b, jnp.maximum(i * (ts // 8) - 1, 0), 0)),
                const((1, RWKV_IN_PAD)), const((8, C)), const((n_lora, C)), const((n_lora, C)),
                const((RWKV_IN_PAD - RW_GATE_OFF, C)), const((C, C))]
    args = [p, p, mu_p, jnp.pad(vec, ((0, 1), (0, 0))), wup, aup, gup, bd]
    if has_vres:
        off = RWKV_IN_PAD - LANE
        lo = 3 * C + n_lora + GATE_LORA - off
        vup = jnp.pad(vres_up, ((lo, LANE - lo - VRES_LORA), (0, 0))).astype(BF16)
        in_specs += [dense, const((1, C)), const((LANE, C))]
        args += [v_first, vres_b.reshape(1, C), vup]
    return pl.pallas_call(
        functools.partial(_rwkv_prep_kernel, ts=ts, has_vres=has_vres),
        grid=(B, S // ts),
        in_specs=in_specs,
        out_specs=[dense] * 7,
        out_shape=[jax.ShapeDtypeStruct((B, S, C), F32)] * 7,
        compiler_params=_params("parallel", "parallel"),
        name="rwkv_prep",
    )(*args)


GROUP_HEADS = 4
GW = GROUP_HEADS * HEAD_DIM
SR = GROUP_HEADS * CHUNK


def _split_bf16(x, n):
    parts = []
    for _ in range(n - 1):
        h = x.astype(BF16)
        parts.append(h)
        x = x - h.astype(F32)
    parts.append(x.astype(BF16))
    return parts


def _dot_split_rhs(a, x, n):
    parts = _split_bf16(x, n)
    out = _dot(a, parts[0])
    for p in parts[1:]:
        out = out + _dot(a, p)
    return out


def _dot_split_lhs(x, b, n):
    parts = _split_bf16(x, n)
    out = _dot(parts[0], b)
    for p in parts[1:]:
        out = out + _dot(p, b)
    return out


def _unit_lower_inverses(mats, eye, in16, in32):
    b = lambda m: m.astype(BF16)
    ds = [b(jnp.where(in16, a, 0.0)) for a in mats]
    xs = [eye - d for d in ds]
    ps = [b(_dot(d, d)) for d in ds]
    for step in range(3):
        xs = [x + _dot(b(x), p) for x, p in zip(xs, ps)]
        if step < 2:
            ps = [b(_dot(p, p)) for p in ps]
    for lowers in ([jnp.where(in32 & jnp.logical_not(in16), a, 0.0) for a in mats],
                   [jnp.where(in32, 0.0, a) for a in mats]):
        xbs = [b(x) for x in xs]
        mids = [b(_dot(xb, b(lo))) for xb, lo in zip(xbs, lowers)]
        xs = [x - _dot(mid, xb) for x, mid, xb in zip(xs, mids, xbs)]
    return xs


def _rwkv_scan_kernel(r_ref, lw_ref, k_ref, v_ref, kk_ref, kka_ref, g_ref, gn_ref, o_ref, state,
                      kap_s, bt_s, rt_s, arb_s, tinv_s, u_s, y0_s, sv_s, gl_s, y_s, *, tt):
    @pl.when(pl.program_id(1) == 0)
    def _():
        state[...] = jnp.zeros_like(state)

    ri = lax.broadcasted_iota(jnp.int32, (SR, GW), 0)
    ci = lax.broadcasted_iota(jnp.int32, (SR, GW), 1)
    same_head = (ri // CHUNK) == (ci // HEAD_DIM)
    strict = (ci % CHUNK) < (ri % CHUNK)
    incl = (ci % CHUNK) <= (ri % CHUNK)
    in16 = (ri // 16) == (ci // 16)
    in32 = (ri // 32) == (ci // 32)
    eye = jnp.where(ri == ci, 1.0, 0.0)
    head_ones = jnp.where(same_head, 1.0, 0.0).astype(BF16)
    tr = lax.broadcasted_iota(jnp.int32, (CHUNK, CHUNK), 0)
    tc = lax.broadcasted_iota(jnp.int32, (CHUNK, CHUNK), 1)
    tri = jnp.where(tc <= tr, 1.0, 0.0).astype(BF16)
    b = lambda m: m.astype(BF16)

    def stack(x):
        return b(jnp.where(same_head, jnp.concatenate([x] * GROUP_HEADS, axis=0), 0.0))

    n_groups = RWKV_HEADS // GROUP_HEADS

    def chunk_rows(c):
        return pl.ds(pl.multiple_of(c * CHUNK, CHUNK), CHUNK)

    def prepare_pair(it, carry):
        items = [(2 * it + u, gi) for u in range(2) for gi in range(n_groups)]
        n = range(len(items))
        kap, bt, kt, rt, vs, gl = [], [], [], [], [], []
        for c, gi in items:
            rows = chunk_rows(c)
            lanes = slice(gi * GW, (gi + 1) * GW)
            lw = lw_ref[0, rows, lanes]
            cs = _dot_split_rhs(tri, lw, 3)
            g_in = jnp.exp(cs)
            g_inv = jnp.exp(-cs)
            kap.append(stack(kk_ref[0, rows, lanes] * jnp.exp(cs - lw)))
            bt.append(stack(kka_ref[0, rows, lanes] * g_inv))
            kt.append(stack(k_ref[0, rows, lanes] * g_inv))
            rt.append(stack(r_ref[0, rows, lanes] * g_in))
            vs.append(stack(v_ref[0, rows, lanes]))
            gl.append(jnp.broadcast_to(g_in[CHUNK - 1:CHUNK, :], (8, GW)))
        a_kb = [jnp.where(strict, _dot_nt(kap[i], bt[i]), 0.0) for i in n]
        a_kk = [b(jnp.where(strict, _dot_nt(kap[i], kt[i]), 0.0)) for i in n]
        a_rk = [b(jnp.where(incl, _dot_nt(rt[i], kt[i]), 0.0)) for i in n]
        a_rb = [b(jnp.where(incl, _dot_nt(rt[i], bt[i]), 0.0)) for i in n]
        u = [_dot(a_kk[i], vs[i]) for i in n]
        y0 = [_dot(a_rk[i], vs[i]) for i in n]
        sv = [_dot_tn(vs[i], kt[i]) for i in n]
        t_inv = _unit_lower_inverses(a_kb, eye, in16, in32)
        for i, (c, gi) in enumerate(items):
            slot = c * n_groups + gi
            kap_s[slot] = kap[i]
            bt_s[slot] = bt[i]
            rt_s[slot] = rt[i]
            arb_s[slot] = a_rb[i]
            tinv_s[slot] = b(t_inv[i])
            u_s[slot] = u[i]
            y0_s[slot] = y0[i]
            sv_s[slot] = sv[i]
            gl_s[slot] = gl[i]
        return carry

    lax.fori_loop(0, tt // (2 * CHUNK), prepare_pair, 0)

    def advance(c, carry):
        gs = range(n_groups)
        slot = [c * n_groups + gi for gi in gs]
        s0 = [state[gi] for gi in gs]
        s0b = [b(s) for s in s0]
        rhs = [b(_dot_nt(kap_s[slot[gi]], s0b[gi]) + u_s[slot[gi]]) for gi in gs]
        pb = [b(_dot(tinv_s[slot[gi]], rhs[gi])) for gi in gs]
        for gi in gs:
            state[gi] = (s0[gi] + sv_s[slot[gi]] - _dot_tn(pb[gi], bt_s[slot[gi]])) * gl_s[slot[gi]][0:1, :]
        for gi in gs:
            y = _dot_nt(rt_s[slot[gi]], s0b[gi]) + y0_s[slot[gi]] - _dot(arb_s[slot[gi]], pb[gi])
            yd = y[0:CHUNK]
            for h in range(1, GROUP_HEADS):
                yd = yd + y[h * CHUNK:(h + 1) * CHUNK]
            y_s[chunk_rows(c), gi * GW:(gi + 1) * GW] = yd
        return carry

    lax.fori_loop(0, tt // CHUNK, advance, 0)

    def finish_pair(it, carry):
        items = [(chunk_rows(2 * it + u), slice(gi * GW, (gi + 1) * GW)) for u in range(2) for gi in range(n_groups)]
        yd = [y_s[rows, lanes] for rows, lanes in items]
        ym = [_dot_split_lhs(y, head_ones, 2) * (1.0 / HEAD_DIM) for y in yd]
        yc = [y - m for y, m in zip(yd, ym)]
        yv = [_dot_split_lhs(c * c, head_ones, 2) * (1.0 / HEAD_DIM) for c in yc]
        rk = [r_ref[0, rows, lanes] * k_ref[0, rows, lanes] * gn_ref[2:3, lanes] for rows, lanes in items]
        bonus = [_dot_split_lhs(x, head_ones, 2) for x in rk]
        for i, (rows, lanes) in enumerate(items):
            yn = yc[i] * lax.rsqrt(yv[i] + GN_EPS) * gn_ref[0:1, lanes] + gn_ref[1:2, lanes]
            o_ref[0, rows, lanes] = b((yn + bonus[i] * v_ref[0, rows, lanes]) * g_ref[0, rows, lanes])
        return carry

    lax.fori_loop(0, tt // (2 * CHUNK), finish_pair, 0)


def _rwkv_scan(r, lw, k, v, kk, kka, g, vec, tt):
    B, S, C = r.shape
    gn = jnp.pad(jnp.stack([vec[5], vec[6], vec[4]]), ((0, 5), (0, 0)))
    dense = pl.BlockSpec((1, tt, C), lambda b, i: (b, i, 0))
    slots = (tt // CHUNK) * (C // GW)
    assert tt % (2 * CHUNK) == 0
    stacked = lambda dtype: pltpu.VMEM((slots, SR, GW), dtype)
    return pl.pallas_call(
        functools.partial(_rwkv_scan_kernel, tt=tt),
        grid=(B, S // tt),
        in_specs=[dense] * 7 + [pl.BlockSpec((8, C), lambda b, i: (0, 0))],
        out_specs=dense,
        out_shape=jax.ShapeDtypeStruct((B, S, C), BF16),
        scratch_shapes=[pltpu.VMEM((C // GW, GW, GW), F32)] + [stacked(BF16)] * 5 + [stacked(F32)] * 3
                       + [pltpu.VMEM((slots, 8, GW), F32), pltpu.VMEM((tt, C), F32)],
        compiler_params=_params("parallel", "arbitrary"),
        name="rwkv_scan",
    )(r, lw, k, v, kk, kka, g, gn)


def _merge_kernel(x_ref, an_ref, ac_ref, ar_ref, gl_ref, wn_ref, wc_ref, wr_ref, wo_ref, g2_ref,
                  xo_ref, h_ref):
    D = x_ref.shape[1]
    o_n = _dot(an_ref[...].astype(BF16), wn_ref[...])
    mixed = _sigmoid(gl_ref[:, 0:D]) * o_n
    o_c = _dot(ac_ref[...], wc_ref[...])
    mixed = mixed + _sigmoid(gl_ref[:, D:2 * D]) * o_c
    o_r = _dot(ar_ref[...], wr_ref[...])
    mixed = mixed + _sigmoid(gl_ref[:, 2 * D:3 * D]) * o_r
    x = x_ref[...] + _dot(mixed.astype(BF16), wo_ref[...])
    xo_ref[...] = x
    ms = jnp.mean(x * x, axis=-1, keepdims=True)
    h_ref[...] = (x * lax.rsqrt(ms + RMS_EPS) * g2_ref[...]).astype(BF16)


def _merge(x, a_nsa, a_conv, a_rwkv, gl, w_n, w_c, w_r, w_o, g2, tm):
    T, D = x.shape
    row = lambda w: pl.BlockSpec((tm, w), lambda i: (i, 0))
    const = lambda shape: pl.BlockSpec(shape, lambda i: (0,) * len(shape))
    return pl.pallas_call(
        _merge_kernel,
        grid=(T // tm,),
        in_specs=[row(D), row(a_nsa.shape[1]), row(a_conv.shape[1]), row(a_rwkv.shape[1]), row(3 * D),
                  const(w_n.shape), const(w_c.shape), const(w_r.shape), const(w_o.shape), const((1, D))],
        out_specs=[row(D), row(D)],
        out_shape=[jax.ShapeDtypeStruct((T, D), F32), jax.ShapeDtypeStruct((T, D), BF16)],
        compiler_params=_params("parallel"),
        name="merge",
    )(x, a_nsa, a_conv, a_rwkv, gl, w_n.astype(BF16), w_c.astype(BF16), w_r.astype(BF16),
      w_o.astype(BF16), g2.reshape(1, D))


def _cast_kernel(w_ref, o_ref):
    o_ref[...] = w_ref[...].astype(BF16)


def _interleave_gate_up(w_gu, tf):
    E, d, two_ff = w_gu.shape
    nj = two_ff // (2 * tf)
    return pl.pallas_call(
        _cast_kernel,
        grid=(E, nj, 2),
        in_specs=[pl.BlockSpec((1, d, tf), lambda e, j, half: (e, 0, half * nj + j))],
        out_specs=pl.BlockSpec((1, d, tf), lambda e, j, half: (e, 0, 2 * j + half)),
        out_shape=jax.ShapeDtypeStruct((E, d, two_ff), BF16),
        compiler_params=_params("parallel", "parallel", "parallel"),
        name="ffn_weight_layout",
    )(w_gu)


def _ffn_kernel(x_ref, h_ref, wgu_ref, wd_ref, o_ref):
    j = pl.program_id(1)
    h = h_ref[...]
    gu = _dot(h, wgu_ref[...])
    tf = gu.shape[1] // 2
    gate, up = gu[:, :tf], gu[:, tf:]
    act = (gate * _sigmoid(gate) * up).astype(BF16)
    y = _dot(act, wd_ref[...])

    @pl.when(j == 0)
    def _():
        o_ref[...] = x_ref[...] + y

    @pl.when(j > 0)
    def _():
        o_ref[...] += y


def _ffn(x, h, w_gu, w_down, tm, tf):
    T, D = x.shape
    FF = w_down.shape[0]
    wgu = _interleave_gate_up(w_gu[None], tf)[0]
    return pl.pallas_call(
        _ffn_kernel,
        grid=(T // tm, FF // tf),
        in_specs=[pl.BlockSpec((tm, D), lambda i, j: (i, 0)),
                  pl.BlockSpec((tm, D), lambda i, j: (i, 0)),
                  pl.BlockSpec((D, 2 * tf), lambda i, j: (0, j)),
                  pl.BlockSpec((tf, D), lambda i, j: (j, 0))],
        out_specs=pl.BlockSpec((tm, D), lambda i, j: (i, 0)),
        out_shape=jax.ShapeDtypeStruct((T, D), F32),
        compiler_params=_params("parallel", "arbitrary"),
        name="ffn_dense",
    )(x, h, wgu, w_down.astype(BF16))


MOE_ROWS = 256
MOE_TAIL = 128
RANK_BLK = 256


def _router_kernel(h_ref, w_ref, b_ref, gw_ref, pos_ref, cnt_ref):
    logits = _dot(h_ref[...], w_ref[...]) + b_ref[...]
    tt = logits.shape[0]
    lane = lax.broadcasted_iota(jnp.int32, logits.shape, 1)
    m1 = jnp.max(logits, axis=-1, keepdims=True)
    i1 = jnp.min(jnp.where(logits == m1, lane, LANE), axis=-1, keepdims=True)
    rest = jnp.where(lane == i1, -jnp.inf, logits)
    m2 = jnp.max(rest, axis=-1, keepdims=True)
    i2 = jnp.min(jnp.where(rest == m2, lane, LANE), axis=-1, keepdims=True)
    e2 = jnp.exp(m2 - m1)
    w1 = 1.0 / (1.0 + e2)
    w2 = e2 / (1.0 + e2)
    gw_ref[...] = jnp.where(lane == i1, w1, jnp.where(lane == i2, w2, 0.0))
    routed = (lane == i1) | (lane == i2)
    r = lax.broadcasted_iota(jnp.int32, (RANK_BLK, RANK_BLK), 0)
    c = lax.broadcasted_iota(jnp.int32, (RANK_BLK, RANK_BLK), 1)
    before = jnp.where(c < r, 1.0, 0.0).astype(BF16)
    run = jnp.zeros((1, LANE), F32)
    ranks = []
    for blk in range(tt // RANK_BLK):
        hit = jnp.where(routed[blk * RANK_BLK:(blk + 1) * RANK_BLK], 1.0, 0.0)
        ranks.append(_dot(before, hit.astype(BF16)) + run)
        run = run + jnp.sum(hit, axis=0, keepdims=True)
    pos = jnp.where(routed, jnp.concatenate(ranks, axis=0), -1.0)
    pos_ref[0] = pos.T[0:N_EXPERTS]
    cnt_ref[0] = jnp.broadcast_to(run, (8, LANE))


def _router(h, router_w, router_b, tt):
    T, D = h.shape
    nt = T // tt
    w = _pad_cols(router_w, LANE).astype(BF16)
    b = jnp.pad(router_b, (0, LANE - N_EXPERTS), constant_values=-jnp.inf).reshape(1, LANE)
    return pl.pallas_call(
        _router_kernel,
        grid=(nt,),
        in_specs=[pl.BlockSpec((tt, D), lambda i: (i, 0)),
                  pl.BlockSpec((D, LANE), lambda i: (0, 0)),
                  pl.BlockSpec((1, LANE), lambda i: (0, 0))],
        out_specs=[pl.BlockSpec((tt, LANE), lambda i: (i, 0)),
                   pl.BlockSpec((1, N_EXPERTS, tt), lambda i: (i, 0, 0)),
                   pl.BlockSpec((1, 8, LANE), lambda i: (i, 0, 0))],
        out_shape=[jax.ShapeDtypeStruct((T, LANE), F32),
                   jax.ShapeDtypeStruct((nt, N_EXPERTS, tt), F32),
                   jax.ShapeDtypeStruct((nt, 8, LANE), F32)],
        compiler_params=_params("parallel"),
        name="moe_router",
    )(h, w, b)


def _moe_kernel(nsb_ref, x_ref, h_ref, gw_ref, pos_ref, wgu_ref, wd_ref, o_ref, xc, yc):
    i = pl.program_id(0)
    e = pl.program_id(1)
    j = pl.program_id(2)
    n_full = nsb_ref[0, i * pl.num_programs(1) + e]
    has_tail = nsb_ref[1, i * pl.num_programs(1) + e]

    @pl.when((e == 0) & (j == 0))
    def _():
        o_ref[...] = x_ref[...]

    pos = pos_ref[0, pl.ds(e, 1), :]

    def for_blocks(fn):
        def body(sb, carry):
            fn(pl.multiple_of(sb * MOE_ROWS, MOE_ROWS), MOE_ROWS)
            return carry
        lax.fori_loop(0, n_full, body, 0)

        @pl.when(has_tail == 1)
        def _():
            fn(pl.multiple_of(n_full * MOE_ROWS, MOE_TAIL), MOE_TAIL)

    def select(r0, n):
        slot = lax.broadcasted_iota(jnp.int32, (n, 1), 0) + r0
        return jnp.where(pos == slot.astype(F32), 1.0, 0.0).astype(BF16)

    def gather(r0, n):
        xc[pl.ds(r0, n), :] = _dot(select(r0, n), h_ref[...]).astype(BF16)
        yc[pl.ds(r0, n), :] = jnp.zeros((n, yc.shape[1]), F32)

    def expert(r0, n):
        gu = _dot(xc[pl.ds(r0, n), :], wgu_ref[0])
        tf = gu.shape[1] // 2
        gate, up = gu[:, :tf], gu[:, tf:]
        act = (gate * _sigmoid(gate) * up).astype(BF16)
        yc[pl.ds(r0, n), :] += _dot(act, wd_ref[0])

    @pl.when(j == 0)
    def _():
        for_blocks(gather)

    for_blocks(expert)

    @pl.when(j == pl.num_programs(2) - 1)
    def _():
        gw = gw_ref[...]
        lane = lax.broadcasted_iota(jnp.int32, gw.shape, 1)
        w_e = jnp.sum(jnp.where(lane == e, gw, 0.0), axis=-1, keepdims=True)

        def scatter(r0, n):
            o_ref[...] += w_e * _dot_tn(select(r0, n), yc[pl.ds(r0, n), :].astype(BF16))
        for_blocks(scatter)


def _moe(x, h, gw, pos, counts, w_gu, w_down, tt, tf):
    T, D = x.shape
    E, FF, _ = w_down.shape
    wgu = _interleave_gate_up(w_gu, tf)
    n_tail = ((counts[:, 0, :E].astype(jnp.int32) + MOE_TAIL - 1) // MOE_TAIL).reshape(-1)
    per_full = MOE_ROWS // MOE_TAIL
    n_sb = jnp.stack([n_tail // per_full, n_tail % per_full])
    once = dict(pipeline_mode=pl.Buffered(1))
    return pl.pallas_call(
        _moe_kernel,
        grid_spec=pltpu.PrefetchScalarGridSpec(
            num_scalar_prefetch=1,
            grid=(T // tt, E, FF // tf),
            in_specs=[pl.BlockSpec((tt, D), lambda i, e, j, n: (i, 0), **once),
                      pl.BlockSpec((tt, D), lambda i, e, j, n: (i, 0), **once),
                      pl.BlockSpec((tt, LANE), lambda i, e, j, n: (i, 0)),
                      pl.BlockSpec((1, E, tt), lambda i, e, j, n: (i, 0, 0)),
                      pl.BlockSpec((1, D, 2 * tf), lambda i, e, j, n: (e, 0, j)),
                      pl.BlockSpec((1, tf, D), lambda i, e, j, n: (e, j, 0))],
            out_specs=pl.BlockSpec((tt, D), lambda i, e, j, n: (i, 0), **once),
            scratch_shapes=[pltpu.VMEM((tt, D), BF16), pltpu.VMEM((tt, D), F32)]),
        out_shape=jax.ShapeDtypeStruct((T, D), F32),
        compiler_params=_params("parallel", "arbitrary", "arbitrary"),
        name="moe_experts",
    )(n_sb, x, h, gw, pos, wgu, w_down.astype(BF16))


def _pad_cols(w, n):
    return jnp.pad(w, ((0, 0), (0, n - w.shape[1])))


def kernel(x, mix_norm_g, w_in_first, w_in_rest, rwkv_mu_first, rwkv_mu_rest, nsa_qk_g, nsa_cmp_pos, nsa_cmp_w1, nsa_cmp_w2, nsa_w_o, conv_w, conv_vec, conv_w_o, rwkv_vec, rwkv_w_up, rwkv_a_up, rwkv_g_up, rwkv_vres_up, rwkv_vres_b, rwkv_w_o, w_out, ffn_norm_g, dense_w_gu, dense_w_down, router_w, router_b, moe_w_gu, moe_w_down):
    B, S, D = x.shape
    T = B * S
    depth = mix_norm_g.shape[0]
    off_conv = NSA_IN
    off_gate = off_conv + 2 * CONV_CH
    off_rwkv = off_gate + 3 * D
    tm = min(512, T)
    xt = x.reshape(T, D)
    v_first = None
    for layer in range(depth):
        if layer == 0:
            w_in, mu, vres_up, vres_b = w_in_first, rwkv_mu_first, None, None
        else:
            w_in, mu = w_in_rest[layer - 1], rwkv_mu_rest[layer - 1]
            vres_up, vres_b = rwkv_vres_up[layer - 1], rwkv_vres_b[layer - 1]
        g1 = mix_norm_g[layer]
        w_nsa = _pad_cols(w_in[:, :off_conv], NSA_IN_PAD).astype(BF16)
        w_conv = w_in[:, off_conv:off_gate].astype(BF16)
        w_gate = w_in[:, off_gate:off_rwkv].astype(BF16)
        w_rwkv = _pad_cols(w_in[:, off_rwkv:], RWKV_IN_PAD).astype(BF16)
        p_nsa, p_conv, p_gate, p_rwkv = _rms_proj(xt, g1, [w_nsa, w_conv, w_gate, w_rwkv], min(256, T))
        p_nsa = p_nsa.reshape(B, S, NSA_IN_PAD)
        p_conv = p_conv.reshape(B, S, 2 * CONV_CH)
        p_rwkv = p_rwkv.reshape(B, S, RWKV_IN_PAD)

        a_nsa = _nsa_branch(p_nsa, nsa_qk_g[layer], nsa_cmp_pos[layer], nsa_cmp_w1[layer],
                            nsa_cmp_w2[layer], B, S)
        a_conv = _conformer(p_conv, conv_w[layer], conv_vec[layer], ts=min(512, S))
        r, lw, k, v, kk, kka, g = _rwkv_prep(
            p_rwkv, mu, rwkv_vec[layer], rwkv_w_up[layer], rwkv_a_up[layer], rwkv_g_up[layer],
            v_first, vres_up, vres_b, ts=min(256, S))
        if layer == 0:
            v_first = v
        a_rwkv = _rwkv_scan(r, lw, k, v, kk, kka, g, rwkv_vec[layer], tt=min(512, S))

        xt, h2 = _merge(xt, a_nsa.reshape(T, NSA_Q_W), a_conv.reshape(T, CONV_CH),
                        a_rwkv.reshape(T, RWKV_W), p_gate, nsa_w_o[layer], conv_w_o[layer],
                        rwkv_w_o[layer], w_out[layer], ffn_norm_g[layer], tm)
        if layer % 2 == 0:
            xt = _ffn(xt, h2, dense_w_gu[layer // 2], dense_w_down[layer // 2], min(512, T), 1408)
        else:
            tt = min(2048, T)
            gw, pos, counts = _router(h2, router_w[layer // 2], router_b[layer // 2], tt)
            xt = _moe(xt, h2, gw, pos, counts, moe_w_gu[layer // 2], moe_w_down[layer // 2], tt, 896)
    return xt.reshape(B, S, D)
```

```python
import functools

import jax
import jax.numpy as jnp
from jax import lax
from jax.experimental import pallas as pl
from jax.experimental.pallas import tpu as pltpu

F32 = jnp.float32
BF16 = jnp.bfloat16
HI = lax.Precision.HIGHEST

NSA_HEADS = 8
NSA_KV_HEADS = 2
NSA_GROUP = NSA_HEADS // NSA_KV_HEADS
HEAD_DIM = 64
NSA_Q_W = NSA_HEADS * HEAD_DIM
NSA_KV_W = NSA_KV_HEADS * HEAD_DIM
CMP_STRIDE = 16
CMP_LEN = 32
SEL_BLOCK = 64
N_SEL = 8
WINDOW = 512
SEL_FORCE = 1e4
CONV_CH = 512
CONV_K = 31
RWKV_HEADS = 8
RWKV_W = RWKV_HEADS * HEAD_DIM
DECAY_LORA = 64
ICLR_LORA = 64
GATE_LORA = 160
VRES_LORA = 32
N_EXPERTS = 8
ROPE_THETA = 10000.0
RMS_EPS = 1e-6
LN_EPS = 1e-5
GN_EPS = 64e-5
NEG_INF = -1e30
LOG2E = 1.4426950408889634
ACC_ROWS = 72

NSA_IN = NSA_Q_W + 6 * NSA_KV_W + 3 * NSA_HEADS
NSA_IN_PAD = 1408
NSA_GATE_BLK = (NSA_Q_W + 6 * NSA_KV_W) // 128
RWKV_IN_PAD = 1920
LANE = 128
VMEM_LIMIT = 56 * 1024 * 1024

CHUNK = 64


def _dot(a, b, precision=None):
    return jnp.dot(a, b, preferred_element_type=F32, precision=precision)


def _dot_nt(a, b, precision=None):
    return lax.dot_general(a, b, (((1,), (1,)), ((), ())), preferred_element_type=F32,
                           precision=precision)


def _dot_tn(a, b, precision=None):
    return lax.dot_general(a, b, (((0,), (0,)), ((), ())), preferred_element_type=F32,
                           precision=precision)


def _sigmoid(x):
    return 1.0 / (1.0 + jnp.exp(-x))


def _params(*sem):
    return pltpu.CompilerParams(dimension_semantics=sem, vmem_limit_bytes=VMEM_LIMIT)


def _rms_proj_kernel(x_ref, g_ref, *refs):
    n = len(refs) // 2
    x = x_ref[...]
    ms = jnp.mean(x * x, axis=-1, keepdims=True)
    h = (x * lax.rsqrt(ms + RMS_EPS) * g_ref[...]).astype(BF16)
    for w_ref, o_ref in zip(refs[:n], refs[n:]):
        o_ref[...] = _dot(h, w_ref[...])


def _rms_proj(x, g, ws, tm):
    T, D = x.shape
    once = dict(pipeline_mode=pl.Buffered(1))
    return pl.pallas_call(
        _rms_proj_kernel,
        grid=(T // tm,),
        in_specs=[pl.BlockSpec((tm, D), lambda i: (i, 0)),
                  pl.BlockSpec((1, D), lambda i: (0, 0))]
                 + [pl.BlockSpec(w.shape, lambda i: (0, 0), **once) for w in ws],
        out_specs=[pl.BlockSpec((tm, w.shape[1]), lambda i: (i, 0)) for w in ws],
        out_shape=[jax.ShapeDtypeStruct((T, w.shape[1]), F32) for w in ws],
        compiler_params=_params("parallel"),
        name="rms_proj",
    )(x, g.reshape(1, D), *ws)


def _rope_dense(x, cos, sin_signed):
    w = x.shape[1]
    lane = lax.broadcasted_iota(jnp.int32, x.shape, 1)
    first_half = (lane % HEAD_DIM) < (HEAD_DIM // 2)
    rot = jnp.where(first_half, pltpu.roll(x, w - HEAD_DIM // 2, 1), pltpu.roll(x, HEAD_DIM // 2, 1))
    return x * cos + rot * sin_signed


def _head_rms(x, bd, g):
    ms = _dot_split_lhs(x * x, bd, 3)
    return x * lax.rsqrt(ms + RMS_EPS) * g


def _nsa_prep_kernel(p_ref, cos_ref, sin_ref, gq_ref, gks_ref, gkw_ref, bd_ref,
                     q_ref, kc_ref, vc_ref, ks_ref, vs_ref, kw_ref, vw_ref):
    cos = cos_ref[...]
    sin = sin_ref[...]
    bd = bd_ref[...]
    q = p_ref[0, :, 0:NSA_Q_W]
    q = _rope_dense(_head_rms(q, bd, gq_ref[...]), cos, sin) * (HEAD_DIM ** -0.5 * LOG2E)
    for h in range(NSA_HEADS):
        q_ref[0, h] = q[:, h * HEAD_DIM:(h + 1) * HEAD_DIM].astype(BF16)

    def slab(i):
        return p_ref[0, :, NSA_Q_W + i * NSA_KV_W:NSA_Q_W + (i + 1) * NSA_KV_W]

    cos_k = cos[:, :NSA_KV_W]
    sin_k = sin[:, :NSA_KV_W]
    bd_k = bd[:NSA_KV_W, :NSA_KV_W]
    ks = _rope_dense(_head_rms(slab(2), bd_k, gks_ref[...]), cos_k, sin_k)
    kw = _rope_dense(_head_rms(slab(4), bd_k, gkw_ref[...]), cos_k, sin_k)
    for val, ref in ((slab(0), kc_ref), (slab(1), vc_ref), (ks, ks_ref), (kw, kw_ref)):
        for g in range(NSA_KV_HEADS):
            ref[0, g] = val[:, g * HEAD_DIM:(g + 1) * HEAD_DIM].astype(BF16)
    lane = lax.broadcasted_iota(jnp.int32, (1, NSA_KV_W), 1)
    tail = jnp.where(lane == HEAD_DIM, 1.0, 0.0)
    for val, ref in ((slab(3), vs_ref), (slab(5), vw_ref)):
        for g in range(NSA_KV_HEADS):
            head_first = val if g == 0 else pltpu.roll(val, NSA_KV_W - g * HEAD_DIM, 1)
            ref[0, g] = jnp.where(lane < HEAD_DIM, head_first, tail).astype(BF16)


def _nsa_prep(p, cos, sin, qk_g, ts):
    B, S, _ = p.shape
    bd = jnp.kron(jnp.eye(NSA_HEADS, dtype=F32), jnp.full((HEAD_DIM, HEAD_DIM), 1.0 / HEAD_DIM, F32)).astype(BF16)
    gq = jnp.tile(qk_g[0], NSA_HEADS).reshape(1, NSA_Q_W)
    gks = jnp.tile(qk_g[2], NSA_KV_HEADS).reshape(1, NSA_KV_W)
    gkw = jnp.tile(qk_g[3], NSA_KV_HEADS).reshape(1, NSA_KV_W)
    kv_shape = jax.ShapeDtypeStruct((B, NSA_KV_HEADS, S, HEAD_DIM), BF16)
    kv_spec = pl.BlockSpec((1, NSA_KV_HEADS, ts, HEAD_DIM), lambda b, i: (b, 0, i, 0))
    aug_shape = jax.ShapeDtypeStruct((B, NSA_KV_HEADS, S, NSA_KV_W), BF16)
    aug_spec = pl.BlockSpec((1, NSA_KV_HEADS, ts, NSA_KV_W), lambda b, i: (b, 0, i, 0))
    const = lambda shape: pl.BlockSpec(shape, lambda b, i: (0,) * len(shape))
    return pl.pallas_call(
        _nsa_prep_kernel,
        grid=(B, S // ts),
        in_specs=[pl.BlockSpec((1, ts, NSA_IN_PAD), lambda b, i: (b, i, 0)),
                  pl.BlockSpec((ts, NSA_Q_W), lambda b, i: (i, 0)),
                  pl.BlockSpec((ts, NSA_Q_W), lambda b, i: (i, 0)),
                  const((1, NSA_Q_W)), const((1, NSA_KV_W)), const((1, NSA_KV_W)),
                  const((NSA_Q_W, NSA_Q_W))],
        out_specs=[pl.BlockSpec((1, NSA_HEADS, ts, HEAD_DIM), lambda b, i: (b, 0, i, 0)),
                   kv_spec, kv_spec, kv_spec, aug_spec, kv_spec, aug_spec],
        out_shape=[jax.ShapeDtypeStruct((B, NSA_HEADS, S, HEAD_DIM), BF16),
                   kv_shape, kv_shape, kv_shape, aug_shape, kv_shape, aug_shape],
        compiler_params=_params("parallel", "parallel"),
        name="nsa_prep",
    )(p, cos, sin, gq, gks, gkw, bd)


def _gelu_tanh(x):
    return 0.5 * x * (1.0 + jnp.tanh(0.7978845608028654 * (x + 0.044715 * x * x * x)))


def _compress_kernel(a_ref, posa_ref, posb_ref, w1a_ref, w1b_ref, w2_ref, g_ref, cos_ref, sin_ref,
                     perm_ref, o_ref, *, is_key):
    a = a_ref[0]
    ncp = a.shape[0]
    w1a = w1a_ref[...]
    w1b = w1b_ref[...]
    p1 = _dot(a, w1a)
    p2 = _dot(a, w1b)
    hp = _dot(posa_ref[...], w1a) + _dot(posb_ref[...], w1b)
    h = _gelu_tanh(p1 + pltpu.roll(p2, ncp - 1, 0) + hp[0:1])
    o = _dot(h.astype(BF16), w2_ref[...])
    if is_key:
        ms = jnp.mean(o * o, axis=-1, keepdims=True)
        o = o * lax.rsqrt(ms + RMS_EPS) * g_ref[...]
        o = o * cos_ref[...] + _dot(o, perm_ref[...], precision=HI) * sin_ref[...]
    row = lax.broadcasted_iota(jnp.int32, o.shape, 0)
    o_ref[0] = jnp.where(row < ncp - 1, o, 0.0).astype(BF16)


def _compress(t, pos_emb, w1, w2, g, cos_c, sin_c, is_key):
    BG, ncp, cw = t.shape
    hid = w1.shape[1]
    pos = pos_emb.reshape(1, CMP_LEN * HEAD_DIM)
    posa = jnp.tile(pos[:, :cw], (8, 1)).astype(BF16)
    posb = jnp.tile(pos[:, cw:], (8, 1)).astype(BF16)
    half = HEAD_DIM // 2
    perm = jnp.roll(jnp.eye(HEAD_DIM, dtype=F32), half, axis=0)
    const = lambda shape: pl.BlockSpec(shape, lambda b: (0,) * len(shape))
    return pl.pallas_call(
        functools.partial(_compress_kernel, is_key=is_key),
        grid=(BG,),
        in_specs=[pl.BlockSpec((1, ncp, cw), lambda b: (b, 0, 0)),
                  const((8, cw)), const((8, cw)), const((cw, hid)), const((cw, hid)),
                  const((hid, HEAD_DIM)), const((1, HEAD_DIM)),
                  const((ncp, HEAD_DIM)), const((ncp, HEAD_DIM)), const((HEAD_DIM, HEAD_DIM))],
        out_specs=pl.BlockSpec((1, ncp, HEAD_DIM), lambda b: (b, 0, 0)),
        out_shape=jax.ShapeDtypeStruct((BG, ncp, HEAD_DIM), BF16),
        compiler_params=_params("parallel"),
        name="nsa_compress_k" if is_key else "nsa_compress_v",
    )(t, posa, posb, w1[:cw].astype(BF16), w1[cw:].astype(BF16), w2.astype(BF16),
      g.reshape(1, HEAD_DIM), cos_c, sin_c, perm)


def _exp_cols(s):
    m = jnp.max(s, axis=0, keepdims=True)
    e = jnp.exp2(s - m)
    return e, jnp.sum(e, axis=0, keepdims=True)


def _heads_to_rows(o, tq):
    return jnp.concatenate([o[:, h * tq:(h + 1) * tq] for h in range(NSA_GROUP)], axis=0)


def _nsa_attn_kernel(q_ref, kc_ref, vc_ref, ks_ref, vs_ref, kw_ref, vw_ref, gl_ref, o_ref, selb_ref, s_ref,
                     *, tq, tk, seq):
    g = pl.program_id(1)
    i = pl.program_id(2)
    rows = NSA_GROUP * tq
    ncp = seq // CMP_STRIDE
    ns = seq // SEL_BLOCK
    n_sel = min(N_SEL, ns)
    q = q_ref[0].reshape(rows, HEAD_DIM)
    t0 = i * tq
    t_q = t0 + lax.broadcasted_iota(jnp.int32, (1, tq), 1)

    def per_head(x):
        return jnp.concatenate([x] * NSA_GROUP, axis=1)

    c_end = lax.broadcasted_iota(jnp.int32, (ncp, 1), 0) * CMP_STRIDE + (CMP_LEN - 1)
    bias = jnp.where(c_end <= t_q, 0.0, NEG_INF)
    e, l = _exp_cols(_dot_nt(kc_ref[0], q) + per_head(bias))
    p_c = e * (per_head(jnp.where(t_q >= CMP_LEN - 1, 1.0, 0.0)) / l)
    o_c = _dot_tn(vc_ref[0], p_c.astype(BF16))

    p_sum = p_c[:, 0:tq]
    for h in range(1, NSA_GROUP):
        p_sum = p_sum + p_c[:, h * tq:(h + 1) * tq]
    s0 = lax.broadcasted_iota(jnp.int32, (ns, ncp), 0) * SEL_BLOCK
    c0 = lax.broadcasted_iota(jnp.int32, (ns, ncp), 1) * CMP_STRIDE
    overlap = jnp.where((c0 < s0 + SEL_BLOCK) & (c0 + CMP_LEN > s0), 1.0, 0.0).astype(BF16)
    imp = _dot_split_rhs(overlap, p_sum, 2)
    j = lax.broadcasted_iota(jnp.int32, (ns, tq), 0).astype(F32)
    cur = (t_q // SEL_BLOCK).astype(F32)
    forced = (j == 0.0) | (j == cur) | (j == cur - 1.0)
    score = jnp.where(forced, SEL_FORCE, jnp.where(j <= cur, imp, -1.0))
    for _ in range(n_sel):
        mx = jnp.max(score, axis=0, keepdims=True)
        first = jnp.min(jnp.where(score == mx, j, float(ns)), axis=0, keepdims=True)
        score = jnp.where(j == first, -jnp.inf, score)
    selb_ref[...] = jnp.where(score == -jnp.inf, 0.0, NEG_INF)

    wlen = WINDOW + tq
    w0 = pl.multiple_of(jnp.maximum(t0 - WINDOW, 0), tq)
    kpos = w0 + lax.broadcasted_iota(jnp.int32, (wlen, 1), 0)
    bias = jnp.where((kpos <= t_q) & (kpos > t_q - WINDOW), 0.0, NEG_INF)
    s = _dot_nt(kw_ref[0, 0, pl.ds(w0, wlen), :], q) + per_head(bias)
    e = jnp.exp2(s - jnp.max(s, axis=0, keepdims=True))
    num = _dot_tn(vw_ref[0, 0, pl.ds(w0, wlen), :], e.astype(BF16))[0:ACC_ROWS]
    o_w = num[0:HEAD_DIM] / num[HEAD_DIM:HEAD_DIM + 1]

    key_off = lax.broadcasted_iota(jnp.int32, (SEL_BLOCK, 1), 0)

    last_tile = seq // tk - 1

    def scores(jt, slot):
        jt = jnp.minimum(jt, last_tile)
        k0 = pl.multiple_of(jt * tk, tk)
        bias = []
        for b in range(tk // SEL_BLOCK):
            picked = selb_ref[pl.ds(jt * (tk // SEL_BLOCK) + b, 1), :]
            causal = k0 + b * SEL_BLOCK + key_off <= t_q
            bias.append(jnp.where(causal, picked, NEG_INF))
        bias = jnp.concatenate(bias, axis=0)
        s_ref[slot] = _dot_nt(ks_ref[0, 0, pl.ds(k0, tk), :], q) + per_head(bias)

    def absorb(jt, slot, carry):
        m_run, acc = carry
        k0 = pl.multiple_of(jt * tk, tk)
        s = s_ref[slot]
        m_new = jnp.maximum(m_run, jnp.max(s, axis=0, keepdims=True))
        e = jnp.exp2(s - m_new)
        alpha = jnp.exp2(m_run - m_new)
        acc = alpha * acc + _dot_tn(vs_ref[0, 0, pl.ds(k0, tk), :], e.astype(BF16))[0:ACC_ROWS]
        return m_new, acc

    def sel_pair(it, carry):
        scores(2 * it + 1, 1)
        carry = absorb(2 * it, 0, carry)
        scores(2 * it + 2, 0)
        return absorb(2 * it + 1, 1, carry)

    n_tiles = (t0 + tq - 1) // tk + 1
    init = (jnp.full((1, rows), NEG_INF, F32), jnp.zeros((ACC_ROWS, rows), F32))
    scores(0, 0)
    _, acc_s = lax.fori_loop(0, (n_tiles + 1) // 2, sel_pair, init)
    o_s = acc_s[0:HEAD_DIM] / acc_s[HEAD_DIM:HEAD_DIM + 1]

    gw = NSA_GROUP * HEAD_DIM
    n = lax.broadcasted_iota(jnp.int32, (3 * gw, LANE), 0)
    col = lax.broadcasted_iota(jnp.int32, (3 * gw, LANE), 1)
    head = g * NSA_GROUP + (n % gw) // HEAD_DIM
    spread = jnp.where(col == head * 3 + n // gw, 1.0, 0.0).astype(BF16)
    gates = None
    for piece in _split_bf16(_sigmoid(gl_ref[0]), 3):
        part = _dot_nt(spread, piece)
        gates = part if gates is None else gates + part
    out = gates[0:gw] * _heads_to_rows(o_c, tq)
    out = out + gates[gw:2 * gw] * _heads_to_rows(o_s, tq)
    out = out + gates[2 * gw:] * _heads_to_rows(o_w, tq)
    o_ref[0] = out.T.astype(BF16)


def _nsa_attn(q, kc, vc, ks, vs, kw, vw, p_nsa, tq, tk):
    B, H, S, _ = q.shape
    G = NSA_KV_HEADS
    ncp = S // CMP_STRIDE
    assert S % (2 * tk) == 0
    cmp_spec = pl.BlockSpec((1, ncp, HEAD_DIM), lambda b, g, i: (b * G + g, 0, 0))
    seq_spec = pl.BlockSpec((1, 1, S, HEAD_DIM), lambda b, g, i: (b, g, 0, 0))
    aug_spec = pl.BlockSpec((1, 1, S, NSA_KV_W), lambda b, g, i: (b, g, 0, 0))
    return pl.pallas_call(
        functools.partial(_nsa_attn_kernel, tq=tq, tk=tk, seq=S),
        grid=(B, G, S // tq),
        in_specs=[pl.BlockSpec((1, NSA_GROUP, tq, HEAD_DIM), lambda b, g, i: (b, g, i, 0)),
                  cmp_spec, cmp_spec, seq_spec, aug_spec, seq_spec, aug_spec,
                  pl.BlockSpec((1, tq, LANE), lambda b, g, i: (b, i, NSA_GATE_BLK))],
        out_specs=pl.BlockSpec((1, tq, NSA_GROUP * HEAD_DIM), lambda b, g, i: (b, i, g)),
        out_shape=jax.ShapeDtypeStruct((B, S, NSA_Q_W), BF16),
        scratch_shapes=[pltpu.VMEM((S // SEL_BLOCK, tq), F32),
                        pltpu.VMEM((2, tk, NSA_GROUP * tq), F32)],
        compiler_params=_params("parallel", "parallel", "arbitrary"),
        name="nsa_attn",
    )(q, kc, vc, ks, vs, kw, vw, p_nsa)


def _rope_tables(S):
    half = HEAD_DIM // 2
    inv = ROPE_THETA ** (-jnp.arange(half, dtype=F32) / half)
    ang = jnp.arange(S, dtype=F32)[:, None] * inv[None, :]
    cos, sin = jnp.cos(ang), jnp.sin(ang)
    cos64 = jnp.concatenate([cos, cos], axis=1)
    sin64 = jnp.concatenate([-sin, sin], axis=1)
    return cos64, sin64


def _nsa_branch(p_nsa, qk_g, cmp_pos, cmp_w1, cmp_w2, B, S):
    cos64, sin64 = _rope_tables(S)
    cos = jnp.tile(cos64, (1, NSA_HEADS))
    sin = jnp.tile(sin64, (1, NSA_HEADS))
    q, kc, vc, ks, vs, kw, vw = _nsa_prep(p_nsa, cos, sin, qk_g, ts=min(256, S))
    ncp = S // CMP_STRIDE
    cmp_end = jnp.minimum(jnp.arange(ncp) * CMP_STRIDE + CMP_LEN - 1, S - 1)
    cw = CMP_STRIDE * HEAD_DIM
    kc = kc.reshape(B * NSA_KV_HEADS, ncp, cw)
    vc = vc.reshape(B * NSA_KV_HEADS, ncp, cw)
    k_cmp = _compress(kc, cmp_pos[0], cmp_w1[0], cmp_w2[0], qk_g[1], cos64[cmp_end], sin64[cmp_end], True)
    v_cmp = _compress(vc, cmp_pos[1], cmp_w1[1], cmp_w2[1], qk_g[1], cos64[cmp_end], sin64[cmp_end], False)
    return _nsa_attn(q, k_cmp, v_cmp, ks, vs, kw, vw, p_nsa, tq=256, tk=256)


CONV_HALO = 32


def _conv_kernel(cur_ref, halo_ref, w_ref, vec_ref, o_ref, buf, y_scr, shifted, *, ts):
    i = pl.program_id(1)

    def glu(p):
        return p[:, :CONV_CH] * _sigmoid(p[:, CONV_CH:])

    buf[0:CONV_HALO, :] = jnp.where(i > 0, glu(halo_ref[0]), 0.0)
    buf[CONV_HALO:, :] = glu(cur_ref[0])
    rt = min(ts, 128)
    shift = CONV_HALO - (CONV_K - 1)
    for c in range(CONV_CH // LANE):
        cs = slice(c * LANE, (c + 1) * LANE)
        for b in range(8):
            span = ts + 8 * ((CONV_K - 1 - b) // 8)
            shifted[b, 0:span, :] = buf[shift + b:shift + b + span, cs]
        for r in range(ts // rt):
            acc = jnp.zeros((rt, LANE), F32)
            for j in range(CONV_K):
                acc = acc + w_ref[j:j + 1, cs] * shifted[j % 8, r * rt + 8 * (j // 8):r * rt + 8 * (j // 8) + rt, :]
            y_scr[r * rt:(r + 1) * rt, cs] = acc
    y = y_scr[...] + vec_ref[0:1, :]
    mu = jnp.mean(y, axis=-1, keepdims=True)
    yc = y - mu
    var = jnp.mean(yc * yc, axis=-1, keepdims=True)
    y = yc * lax.rsqrt(var + LN_EPS) * vec_ref[1:2, :] + vec_ref[2:3, :]
    o_ref[0] = (y * _sigmoid(y)).astype(BF16)


def _conformer(p, conv_w, conv_vec, ts):
    B, S, _ = p.shape
    w = jnp.pad(conv_w, ((0, 32 - CONV_K), (0, 0)))
    vec = jnp.pad(conv_vec, ((0, 5), (0, 0)))
    hb = ts // CONV_HALO
    return pl.pallas_call(
        functools.partial(_conv_kernel, ts=ts),
        grid=(B, S // ts),
        in_specs=[pl.BlockSpec((1, ts, 2 * CONV_CH), lambda b, i: (b, i, 0)),
                  pl.BlockSpec((1, CONV_HALO, 2 * CONV_CH), lambda b, i: (b, jnp.maximum(i * hb - 1, 0), 0)),
                  pl.BlockSpec((32, CONV_CH), lambda b, i: (0, 0)),
                  pl.BlockSpec((8, CONV_CH), lambda b, i: (0, 0))],
        out_specs=pl.BlockSpec((1, ts, CONV_CH), lambda b, i: (b, i, 0)),
        out_shape=jax.ShapeDtypeStruct((B, S, CONV_CH), BF16),
        scratch_shapes=[pltpu.VMEM((CONV_HALO + ts, CONV_CH), F32), pltpu.VMEM((ts, CONV_CH), F32),
                        pltpu.VMEM((8, ts + 8 * ((CONV_K - 1) // 8), LANE), F32)],
        compiler_params=_params("parallel", "parallel"),
        name="conformer",
    )(p, p, w, vec)


RW_LORA_OFF = 3 * RWKV_W
RW_GATE_OFF = RW_LORA_OFF + DECAY_LORA + ICLR_LORA
RW_VRES_OFF = RWKV_IN_PAD - LANE


def _rwkv_prep_kernel(*refs, ts, has_vres):
    if has_vres:
        (cur_ref, halo_ref, mu_ref, vec_ref, wup_ref, aup_ref, gup_ref, bd_ref, vf_ref, vb_ref, vup_ref,
         r_ref, lw_ref, k_ref, v_ref, kk_ref, kka_ref, g_ref) = refs
    else:
        (cur_ref, halo_ref, mu_ref, vec_ref, wup_ref, aup_ref, gup_ref, bd_ref,
         r_ref, lw_ref, k_ref, v_ref, kk_ref, kka_ref, g_ref) = refs
    i = pl.program_id(1)
    p = cur_ref[0]
    last = jnp.where(i > 0, halo_ref[0, 7:8, :], 0.0)
    row = lax.broadcasted_iota(jnp.int32, p.shape, 0)
    prev = jnp.where(row == 0, last, pltpu.roll(p, 1, 0))
    u = p + (prev - p) * mu_ref[...]
    C = RWKV_W
    r, k, v = u[:, :C], u[:, C:2 * C], u[:, 2 * C:3 * C]
    lora = u[:, RW_LORA_OFF:RW_GATE_OFF]
    w_in = vec_ref[0:1, :] + _dot(jnp.tanh(lora).astype(BF16), wup_ref[...])
    sp = jnp.maximum(-w_in, 0.0) + jnp.log(1.0 + jnp.exp(-jnp.abs(w_in)))
    lw = -jnp.exp(-sp - 0.5)
    a = _sigmoid(vec_ref[1:2, :] + _dot(lora.astype(BF16), aup_ref[...]))
    g = _dot(_sigmoid(u[:, RW_GATE_OFF:]).astype(BF16), gup_ref[...])
    if has_vres:
        mix = _sigmoid(vb_ref[...] + _dot(u[:, RW_VRES_OFF:].astype(BF16), vup_ref[...]))
        v = v + (vf_ref[0] - v) * mix
    kk = k * vec_ref[2:3, :]
    ss = _dot_split_lhs(kk * kk, bd_ref[...], 3)
    kk = kk / jnp.maximum(jnp.sqrt(ss), 1e-12)
    k = k * (1.0 + (a - 1.0) * vec_ref[3:4, :])
    for val, ref in ((r, r_ref), (lw, lw_ref), (k, k_ref), (v, v_ref), (kk, kk_ref), (kk * a, kka_ref),
                     (g, g_ref)):
        ref[0] = val


def _pad_rows(w, n):
    return jnp.pad(w, ((0, n - w.shape[0]), (0, 0)))


def _rwkv_prep(p, mu, vec, w_up, a_up, g_up, v_first, vres_up, vres_b, ts):
    B, S, _ = p.shape
    C = RWKV_W
    has_vres = v_first is not None
    mu_p = jnp.pad(mu, (0, RWKV_IN_PAD - mu.shape[0])).reshape(1, RWKV_IN_PAD)
    n_lora = DECAY_LORA + ICLR_LORA
    wup = _pad_rows(w_up, n_lora).astype(BF16)
    aup = jnp.pad(a_up, ((DECAY_LORA, 0), (0, 0))).astype(BF16)
    gup = _pad_rows(g_up, RWKV_IN_PAD - RW_GATE_OFF).astype(BF16)
    bd = jnp.kron(jnp.eye(RWKV_HEADS, dtype=F32), jnp.ones((HEAD_DIM, HEAD_DIM), F32)).astype(BF16)
    const = lambda shape: pl.BlockSpec(shape, lambda b, i: (0,) * len(shape))
    dense = pl.BlockSpec((1, ts, C), lambda b, i: (b, i, 0))
    in_specs = [pl.BlockSpec((1, ts, RWKV_IN_PAD), lambda b, i: (b, i, 0)),
                pl.BlockSpec((1, 8, RWKV_IN_PAD), lambda b, i: (b, jnp.maximum(i * (ts // 8) - 1, 0), 0)),
                const((1, RWKV_IN_PAD)), const((8, C)), const((n_lora, C)), const((n_lora, C)),
                const((RWKV_IN_PAD - RW_GATE_OFF, C)), const((C, C))]
    args = [p, p, mu_p, jnp.pad(vec, ((0, 1), (0, 0))), wup, aup, gup, bd]
    if has_vres:
        off = RWKV_IN_PAD - LANE
        lo = 3 * C + n_lora + GATE_LORA - off
        vup = jnp.pad(vres_up, ((lo, LANE - lo - VRES_LORA), (0, 0))).astype(BF16)
        in_specs += [dense, const((1, C)), const((LANE, C))]
        args += [v_first, vres_b.reshape(1, C), vup]
    return pl.pallas_call(
        functools.partial(_rwkv_prep_kernel, ts=ts, has_vres=has_vres),
        grid=(B, S // ts),
        in_specs=in_specs,
        out_specs=[dense] * 7,
        out_shape=[jax.ShapeDtypeStruct((B, S, C), F32)] * 7,
        compiler_params=_params("parallel", "parallel"),
        name="rwkv_prep",
    )(*args)


GROUP_HEADS = 4
GW = GROUP_HEADS * HEAD_DIM
SR = GROUP_HEADS * CHUNK
SEQ_PARTS = 2


def _split_bf16(x, n):
    parts = []
    for _ in range(n - 1):
        h = x.astype(BF16)
        parts.append(h)
        x = x - h.astype(F32)
    parts.append(x.astype(BF16))
    return parts


def _dot_split_rhs(a, x, n):
    parts = _split_bf16(x, n)
    out = _dot(a, parts[0])
    for p in parts[1:]:
        out = out + _dot(a, p)
    return out


def _dot_split_lhs(x, b, n):
    parts = _split_bf16(x, n)
    out = _dot(parts[0], b)
    for p in parts[1:]:
        out = out + _dot(p, b)
    return out


def _unit_lower_inverses(mats, eye, in16, in32):
    b = lambda m: m.astype(BF16)
    ds = [b(jnp.where(in16, a, 0.0)) for a in mats]
    xs = [eye - d for d in ds]
    ps = [b(_dot(d, d)) for d in ds]
    for step in range(3):
        xs = [x + _dot(b(x), p) for x, p in zip(xs, ps)]
        if step < 2:
            ps = [b(_dot(p, p)) for p in ps]
    for lowers in ([jnp.where(in32 & jnp.logical_not(in16), a, 0.0) for a in mats],
                   [jnp.where(in32, 0.0, a) for a in mats]):
        xbs = [b(x) for x in xs]
        mids = [b(_dot(xb, b(lo))) for xb, lo in zip(xbs, lowers)]
        xs = [x - _dot(mid, xb) for x, mid, xb in zip(xs, mids, xbs)]
    return xs


def _rwkv_scan_kernel(r_ref, lw_ref, k_ref, v_ref, kk_ref, kka_ref, g_ref, gn_ref, o_ref, state,
                      kap_s, bt_s, rt_s, arb_s, tinv_s, u_s, y0_s, sv_s, gl_s, y_s, *, tt):
    @pl.when(pl.program_id(1) == 0)
    def _():
        state[...] = jnp.zeros_like(state)

    ri = lax.broadcasted_iota(jnp.int32, (SR, GW), 0)
    ci = lax.broadcasted_iota(jnp.int32, (SR, GW), 1)
    same_head = (ri // CHUNK) == (ci // HEAD_DIM)
    strict = (ci % CHUNK) < (ri % CHUNK)
    incl = (ci % CHUNK) <= (ri % CHUNK)
    in16 = (ri // 16) == (ci // 16)
    in32 = (ri // 32) == (ci // 32)
    eye = jnp.where(ri == ci, 1.0, 0.0)
    head_ones = jnp.where(same_head, 1.0, 0.0).astype(BF16)
    tr = lax.broadcasted_iota(jnp.int32, (CHUNK, CHUNK), 0)
    tc = lax.broadcasted_iota(jnp.int32, (CHUNK, CHUNK), 1)
    tri = jnp.where(tc <= tr, 1.0, 0.0).astype(BF16)
    b = lambda m: m.astype(BF16)

    def stack(x):
        return b(jnp.where(same_head, jnp.concatenate([x] * GROUP_HEADS, axis=0), 0.0))

    n_groups = RWKV_HEADS // GROUP_HEADS

    def chunk_rows(c):
        return pl.ds(pl.multiple_of(c * CHUNK, CHUNK), CHUNK)

    def prepare_pair(it, carry):
        items = [(2 * it + u, gi) for u in range(2) for gi in range(n_groups)]
        n = range(len(items))
        kap, bt, kt, rt, vs, gl = [], [], [], [], [], []
        for c, gi in items:
            rows = chunk_rows(c)
            lanes = slice(gi * GW, (gi + 1) * GW)
            lw = lw_ref[0, rows, lanes]
            cs = _dot_split_rhs(tri, lw, 3)
            g_in = jnp.exp(cs)
            g_inv = jnp.exp(-cs)
            kap.append(stack(kk_ref[0, rows, lanes] * jnp.exp(cs - lw)))
            bt.append(stack(kka_ref[0, rows, lanes] * g_inv))
            kt.append(stack(k_ref[0, rows, lanes] * g_inv))
            rt.append(stack(r_ref[0, rows, lanes] * g_in))
            vs.append(stack(v_ref[0, rows, lanes]))
            gl.append(jnp.broadcast_to(g_in[CHUNK - 1:CHUNK, :], (8, GW)))
        a_kb = [jnp.where(strict, _dot_nt(kap[i], bt[i]), 0.0) for i in n]
        a_kk = [b(jnp.where(strict, _dot_nt(kap[i], kt[i]), 0.0)) for i in n]
        a_rk = [b(jnp.where(incl, _dot_nt(rt[i], kt[i]), 0.0)) for i in n]
        a_rb = [b(jnp.where(incl, _dot_nt(rt[i], bt[i]), 0.0)) for i in n]
        u = [_dot(a_kk[i], vs[i]) for i in n]
        y0 = [_dot(a_rk[i], vs[i]) for i in n]
        sv = [_dot_tn(vs[i], kt[i]) for i in n]
        t_inv = _unit_lower_inverses(a_kb, eye, in16, in32)
        for i, (c, gi) in enumerate(items):
            slot = c * n_groups + gi
            kap_s[slot] = kap[i]
            bt_s[slot] = bt[i]
            rt_s[slot] = rt[i]
            arb_s[slot] = a_rb[i]
            tinv_s[slot] = b(t_inv[i])
            u_s[slot] = u[i]
            y0_s[slot] = y0[i]
            sv_s[slot] = sv[i]
            gl_s[slot] = gl[i]
        return carry

    lax.fori_loop(0, tt // (2 * CHUNK), prepare_pair, 0)

    def advance(c, carry):
        half = SR // SEQ_PARTS
        items = [(gi, slice(hh * half, (hh + 1) * half)) for gi in range(n_groups) for hh in range(SEQ_PARTS)]
        n = range(len(items))
        blk = lambda ref, i: ref[c * n_groups + items[i][0], items[i][1], items[i][1]]
        s0 = [state[gi, d, d] for gi, d in items]
        s0b = [b(s) for s in s0]
        rhs = [b(_dot_nt(blk(kap_s, i), s0b[i]) + blk(u_s, i)) for i in n]
        pb = [b(_dot(blk(tinv_s, i), rhs[i])) for i in n]
        for i, (gi, d) in enumerate(items):
            decay = gl_s[c * n_groups + gi, 0:1, d]
            state[gi, d, d] = (s0[i] + blk(sv_s, i) - _dot_tn(pb[i], blk(bt_s, i))) * decay
        for i, (gi, d) in enumerate(items):
            y = _dot_nt(blk(rt_s, i), s0b[i]) + blk(y0_s, i) - _dot(blk(arb_s, i), pb[i])
            yd = y[0:CHUNK]
            for h in range(1, half // CHUNK):
                yd = yd + y[h * CHUNK:(h + 1) * CHUNK]
            y_s[chunk_rows(c), gi * GW + d.start:gi * GW + d.stop] = yd
        return carry

    lax.fori_loop(0, tt // CHUNK, advance, 0)

    def finish_pair(it, carry):
        items = [(chunk_rows(2 * it + u), slice(gi * GW, (gi + 1) * GW)) for u in range(2) for gi in range(n_groups)]
        yd = [y_s[rows, lanes] for rows, lanes in items]
        ym = [_dot_split_lhs(y, head_ones, 2) * (1.0 / HEAD_DIM) for y in yd]
        yc = [y - m for y, m in zip(yd, ym)]
        yv = [_dot_split_lhs(c * c, head_ones, 2) * (1.0 / HEAD_DIM) for c in yc]
        rk = [r_ref[0, rows, lanes] * k_ref[0, rows, lanes] * gn_ref[2:3, lanes] for rows, lanes in items]
        bonus = [_dot_split_lhs(x, head_ones, 2) for x in rk]
        for i, (rows, lanes) in enumerate(items):
            yn = yc[i] * lax.rsqrt(yv[i] + GN_EPS) * gn_ref[0:1, lanes] + gn_ref[1:2, lanes]
            o_ref[0, rows, lanes] = b((yn + bonus[i] * v_ref[0, rows, lanes]) * g_ref[0, rows, lanes])
        return carry

    lax.fori_loop(0, tt // (2 * CHUNK), finish_pair, 0)


def _rwkv_scan(r, lw, k, v, kk, kka, g, vec, tt):
    B, S, C = r.shape
    gn = jnp.pad(jnp.stack([vec[5], vec[6], vec[4]]), ((0, 5), (0, 0)))
    dense = pl.BlockSpec((1, tt, C), lambda b, i: (b, i, 0))
    slots = (tt // CHUNK) * (C // GW)
    assert tt % (2 * CHUNK) == 0
    stacked = lambda dtype: pltpu.VMEM((slots, SR, GW), dtype)
    return pl.pallas_call(
        functools.partial(_rwkv_scan_kernel, tt=tt),
        grid=(B, S // tt),
        in_specs=[dense] * 7 + [pl.BlockSpec((8, C), lambda b, i: (0, 0))],
        out_specs=dense,
        out_shape=jax.ShapeDtypeStruct((B, S, C), BF16),
        scratch_shapes=[pltpu.VMEM((C // GW, GW, GW), F32)] + [stacked(BF16)] * 5 + [stacked(F32)] * 3
                       + [pltpu.VMEM((slots, 8, GW), F32), pltpu.VMEM((tt, C), F32)],
        compiler_params=_params("parallel", "arbitrary"),
        name="rwkv_scan",
    )(r, lw, k, v, kk, kka, g, gn)


def _merge_kernel(x_ref, an_ref, ac_ref, ar_ref, gl_ref, wn_ref, wc_ref, wr_ref, wo_ref, g2_ref,
                  xo_ref, h_ref):
    D = x_ref.shape[1]
    o_n = _dot(an_ref[...], wn_ref[...])
    mixed = _sigmoid(gl_ref[:, 0:D]) * o_n
    o_c = _dot(ac_ref[...], wc_ref[...])
    mixed = mixed + _sigmoid(gl_ref[:, D:2 * D]) * o_c
    o_r = _dot(ar_ref[...], wr_ref[...])
    mixed = mixed + _sigmoid(gl_ref[:, 2 * D:3 * D]) * o_r
    x = x_ref[...] + _dot(mixed.astype(BF16), wo_ref[...])
    xo_ref[...] = x
    ms = jnp.mean(x * x, axis=-1, keepdims=True)
    h_ref[...] = (x * lax.rsqrt(ms + RMS_EPS) * g2_ref[...]).astype(BF16)


def _merge(x, a_nsa, a_conv, a_rwkv, gl, w_n, w_c, w_r, w_o, g2, tm):
    T, D = x.shape
    row = lambda w: pl.BlockSpec((tm, w), lambda i: (i, 0))
    const = lambda shape: pl.BlockSpec(shape, lambda i: (0,) * len(shape))
    return pl.pallas_call(
        _merge_kernel,
        grid=(T // tm,),
        in_specs=[row(D), row(a_nsa.shape[1]), row(a_conv.shape[1]), row(a_rwkv.shape[1]), row(3 * D),
                  const(w_n.shape), const(w_c.shape), const(w_r.shape), const(w_o.shape), const((1, D))],
        out_specs=[row(D), row(D)],
        out_shape=[jax.ShapeDtypeStruct((T, D), F32), jax.ShapeDtypeStruct((T, D), BF16)],
        compiler_params=_params("parallel"),
        name="merge",
    )(x, a_nsa, a_conv, a_rwkv, gl, w_n.astype(BF16), w_c.astype(BF16), w_r.astype(BF16),
      w_o.astype(BF16), g2.reshape(1, D))


def _cast_kernel(w_ref, o_ref):
    o_ref[...] = w_ref[...].astype(BF16)


def _interleave_gate_up(w_gu, tf):
    E, d, two_ff = w_gu.shape
    nj = two_ff // (2 * tf)
    return pl.pallas_call(
        _cast_kernel,
        grid=(E, nj, 2),
        in_specs=[pl.BlockSpec((1, d, tf), lambda e, j, half: (e, 0, half * nj + j))],
        out_specs=pl.BlockSpec((1, d, tf), lambda e, j, half: (e, 0, 2 * j + half)),
        out_shape=jax.ShapeDtypeStruct((E, d, two_ff), BF16),
        compiler_params=_params("parallel", "parallel", "parallel"),
        name="ffn_weight_layout",
    )(w_gu)


def _ffn_kernel(x_ref, h_ref, wgu_ref, wd_ref, o_ref):
    j = pl.program_id(1)
    h = h_ref[...]
    gu = _dot(h, wgu_ref[...])
    tf = gu.shape[1] // 2
    gate, up = gu[:, :tf], gu[:, tf:]
    act = (gate * _sigmoid(gate) * up).astype(BF16)
    y = _dot(act, wd_ref[...])

    @pl.when(j == 0)
    def _():
        o_ref[...] = x_ref[...] + y

    @pl.when(j > 0)
    def _():
        o_ref[...] += y


def _ffn(x, h, w_gu, w_down, tm, tf):
    T, D = x.shape
    FF = w_down.shape[0]
    wgu = _interleave_gate_up(w_gu[None], tf)[0]
    return pl.pallas_call(
        _ffn_kernel,
        grid=(T // tm, FF // tf),
        in_specs=[pl.BlockSpec((tm, D), lambda i, j: (i, 0)),
                  pl.BlockSpec((tm, D), lambda i, j: (i, 0)),
                  pl.BlockSpec((D, 2 * tf), lambda i, j: (0, j)),
                  pl.BlockSpec((tf, D), lambda i, j: (j, 0))],
        out_specs=pl.BlockSpec((tm, D), lambda i, j: (i, 0)),
        out_shape=jax.ShapeDtypeStruct((T, D), F32),
        compiler_params=_params("parallel", "arbitrary"),
        name="ffn_dense",
    )(x, h, wgu, w_down.astype(BF16))


MOE_ROWS = 256
MOE_TAIL = 128
RANK_BLK = 256


def _router_kernel(h_ref, w_ref, b_ref, gw_ref, pos_ref, cnt_ref):
    logits = _dot(h_ref[...], w_ref[...]) + b_ref[...]
    tt = logits.shape[0]
    lane = lax.broadcasted_iota(jnp.int32, logits.shape, 1)
    m1 = jnp.max(logits, axis=-1, keepdims=True)
    i1 = jnp.min(jnp.where(logits == m1, lane, LANE), axis=-1, keepdims=True)
    rest = jnp.where(lane == i1, -jnp.inf, logits)
    m2 = jnp.max(rest, axis=-1, keepdims=True)
    i2 = jnp.min(jnp.where(rest == m2, lane, LANE), axis=-1, keepdims=True)
    e2 = jnp.exp(m2 - m1)
    w1 = 1.0 / (1.0 + e2)
    w2 = e2 / (1.0 + e2)
    gw_ref[...] = jnp.where(lane == i1, w1, jnp.where(lane == i2, w2, 0.0))
    routed = (lane == i1) | (lane == i2)
    r = lax.broadcasted_iota(jnp.int32, (RANK_BLK, RANK_BLK), 0)
    c = lax.broadcasted_iota(jnp.int32, (RANK_BLK, RANK_BLK), 1)
    before = jnp.where(c < r, 1.0, 0.0).astype(BF16)
    run = jnp.zeros((1, LANE), F32)
    ranks = []
    for blk in range(tt // RANK_BLK):
        hit = jnp.where(routed[blk * RANK_BLK:(blk + 1) * RANK_BLK], 1.0, 0.0)
        ranks.append(_dot(before, hit.astype(BF16)) + run)
        run = run + jnp.sum(hit, axis=0, keepdims=True)
    pos = jnp.where(routed, jnp.concatenate(ranks, axis=0), -1.0)
    pos_ref[0] = pos.T[0:N_EXPERTS]
    cnt_ref[0] = jnp.broadcast_to(run, (8, LANE))


def _router(h, router_w, router_b, tt):
    T, D = h.shape
    nt = T // tt
    w = _pad_cols(router_w, LANE).astype(BF16)
    b = jnp.pad(router_b, (0, LANE - N_EXPERTS), constant_values=-jnp.inf).reshape(1, LANE)
    return pl.pallas_call(
        _router_kernel,
        grid=(nt,),
        in_specs=[pl.BlockSpec((tt, D), lambda i: (i, 0)),
                  pl.BlockSpec((D, LANE), lambda i: (0, 0)),
                  pl.BlockSpec((1, LANE), lambda i: (0, 0))],
        out_specs=[pl.BlockSpec((tt, LANE), lambda i: (i, 0)),
                   pl.BlockSpec((1, N_EXPERTS, tt), lambda i: (i, 0, 0)),
                   pl.BlockSpec((1, 8, LANE), lambda i: (i, 0, 0))],
        out_shape=[jax.ShapeDtypeStruct((T, LANE), F32),
                   jax.ShapeDtypeStruct((nt, N_EXPERTS, tt), F32),
                   jax.ShapeDtypeStruct((nt, 8, LANE), F32)],
        compiler_params=_params("parallel"),
        name="moe_router",
    )(h, w, b)


def _moe_kernel(nsb_ref, x_ref, h_ref, gw_ref, pos_ref, wgu_ref, wd_ref, o_ref, xc, yc):
    i = pl.program_id(0)
    e = pl.program_id(1)
    j = pl.program_id(2)
    n_full = nsb_ref[0, i * pl.num_programs(1) + e]
    has_tail = nsb_ref[1, i * pl.num_programs(1) + e]

    @pl.when((e == 0) & (j == 0))
    def _():
        o_ref[...] = x_ref[...]

    pos = pos_ref[0, pl.ds(e, 1), :]

    def for_blocks(fn):
        def body(sb, carry):
            fn(pl.multiple_of(sb * MOE_ROWS, MOE_ROWS), MOE_ROWS)
            return carry
        lax.fori_loop(0, n_full, body, 0)

        @pl.when(has_tail == 1)
        def _():
            fn(pl.multiple_of(n_full * MOE_ROWS, MOE_TAIL), MOE_TAIL)

    def select(r0, n):
        slot = lax.broadcasted_iota(jnp.int32, (n, 1), 0) + r0
        return jnp.where(pos == slot.astype(F32), 1.0, 0.0).astype(BF16)

    def gather(r0, n):
        xc[pl.ds(r0, n), :] = _dot(select(r0, n), h_ref[...]).astype(BF16)
        yc[pl.ds(r0, n), :] = jnp.zeros((n, yc.shape[1]), F32)

    def expert(r0, n):
        gu = _dot(xc[pl.ds(r0, n), :], wgu_ref[0])
        tf = gu.shape[1] // 2
        gate, up = gu[:, :tf], gu[:, tf:]
        act = (gate * _sigmoid(gate) * up).astype(BF16)
        yc[pl.ds(r0, n), :] += _dot(act, wd_ref[0])

    @pl.when(j == 0)
    def _():
        for_blocks(gather)

    for_blocks(expert)

    @pl.when(j == pl.num_programs(2) - 1)
    def _():
        gw = gw_ref[...]
        lane = lax.broadcasted_iota(jnp.int32, gw.shape, 1)
        w_e = jnp.sum(jnp.where(lane == e, gw, 0.0), axis=-1, keepdims=True)

        def scatter(r0, n):
            o_ref[...] += w_e * _dot_tn(select(r0, n), yc[pl.ds(r0, n), :].astype(BF16))
        for_blocks(scatter)


def _moe(x, h, gw, pos, counts, w_gu, w_down, tt, tf):
    T, D = x.shape
    E, FF, _ = w_down.shape
    wgu = _interleave_gate_up(w_gu, tf)
    n_tail = ((counts[:, 0, :E].astype(jnp.int32) + MOE_TAIL - 1) // MOE_TAIL).reshape(-1)
    per_full = MOE_ROWS // MOE_TAIL
    n_sb = jnp.stack([n_tail // per_full, n_tail % per_full])
    once = dict(pipeline_mode=pl.Buffered(1))
    return pl.pallas_call(
        _moe_kernel,
        grid_spec=pltpu.PrefetchScalarGridSpec(
            num_scalar_prefetch=1,
            grid=(T // tt, E, FF // tf),
            in_specs=[pl.BlockSpec((tt, D), lambda i, e, j, n: (i, 0), **once),
                      pl.BlockSpec((tt, D), lambda i, e, j, n: (i, 0), **once),
                      pl.BlockSpec((tt, LANE), lambda i, e, j, n: (i, 0)),
                      pl.BlockSpec((1, E, tt), lambda i, e, j, n: (i, 0, 0)),
                      pl.BlockSpec((1, D, 2 * tf), lambda i, e, j, n: (e, 0, j)),
                      pl.BlockSpec((1, tf, D), lambda i, e, j, n: (e, j, 0))],
            out_specs=pl.BlockSpec((tt, D), lambda i, e, j, n: (i, 0), **once),
            scratch_shapes=[pltpu.VMEM((tt, D), BF16), pltpu.VMEM((tt, D), F32)]),
        out_shape=jax.ShapeDtypeStruct((T, D), F32),
        compiler_params=_params("parallel", "arbitrary", "arbitrary"),
        name="moe_experts",
    )(n_sb, x, h, gw, pos, wgu, w_down.astype(BF16))


def _pad_cols(w, n):
    return jnp.pad(w, ((0, 0), (0, n - w.shape[1])))


def kernel(x, mix_norm_g, w_in_first, w_in_rest, rwkv_mu_first, rwkv_mu_rest, nsa_qk_g, nsa_cmp_pos, nsa_cmp_w1, nsa_cmp_w2, nsa_w_o, conv_w, conv_vec, conv_w_o, rwkv_vec, rwkv_w_up, rwkv_a_up, rwkv_g_up, rwkv_vres_up, rwkv_vres_b, rwkv_w_o, w_out, ffn_norm_g, dense_w_gu, dense_w_down, router_w, router_b, moe_w_gu, moe_w_down):
    B, S, D = x.shape
    T = B * S
    depth = mix_norm_g.shape[0]
    off_conv = NSA_IN
    off_gate = off_conv + 2 * CONV_CH
    off_rwkv = off_gate + 3 * D
    tm = min(512, T)
    xt = x.reshape(T, D)
    v_first = None
    for layer in range(depth):
        if layer == 0:
            w_in, mu, vres_up, vres_b = w_in_first, rwkv_mu_first, None, None
        else:
            w_in, mu = w_in_rest[layer - 1], rwkv_mu_rest[layer - 1]
            vres_up, vres_b = rwkv_vres_up[layer - 1], rwkv_vres_b[layer - 1]
        g1 = mix_norm_g[layer]
        w_nsa = _pad_cols(w_in[:, :off_conv], NSA_IN_PAD).astype(BF16)
        w_conv = w_in[:, off_conv:off_gate].astype(BF16)
        w_gate = w_in[:, off_gate:off_rwkv].astype(BF16)
        w_rwkv = _pad_cols(w_in[:, off_rwkv:], RWKV_IN_PAD).astype(BF16)
        p_nsa, p_conv, p_gate, p_rwkv = _rms_proj(xt, g1, [w_nsa, w_conv, w_gate, w_rwkv], min(256, T))
        p_nsa = p_nsa.reshape(B, S, NSA_IN_PAD)
        p_conv = p_conv.reshape(B, S, 2 * CONV_CH)
        p_rwkv = p_rwkv.reshape(B, S, RWKV_IN_PAD)

        a_nsa = _nsa_branch(p_nsa, nsa_qk_g[layer], nsa_cmp_pos[layer], nsa_cmp_w1[layer],
                            nsa_cmp_w2[layer], B, S)
        a_conv = _conformer(p_conv, conv_w[layer], conv_vec[layer], ts=min(512, S))
        r, lw, k, v, kk, kka, g = _rwkv_prep(
            p_rwkv, mu, rwkv_vec[layer], rwkv_w_up[layer], rwkv_a_up[layer], rwkv_g_up[layer],
            v_first, vres_up, vres_b, ts=min(256, S))
        if layer == 0:
            v_first = v
        a_rwkv = _rwkv_scan(r, lw, k, v, kk, kka, g, rwkv_vec[layer], tt=min(512, S))

        xt, h2 = _merge(xt, a_nsa.reshape(T, NSA_Q_W), a_conv.reshape(T, CONV_CH),
                        a_rwkv.reshape(T, RWKV_W), p_gate, nsa_w_o[layer], conv_w_o[layer],
                        rwkv_w_o[layer], w_out[layer], ffn_norm_g[layer], tm)
        if layer % 2 == 0:
            xt = _ffn(xt, h2, dense_w_gu[layer // 2], dense_w_down[layer // 2], min(512, T), 1408)
        else:
            tt = min(2048, T)
            gw, pos, counts = _router(h2, router_w[layer // 2], router_b[layer // 2], tt)
            xt = _moe(xt, h2, gw, pos, counts, moe_w_gu[layer // 2], moe_w_down[layer // 2], tt, 896)
    return xt.reshape(B, S, D)
```

```python
import functools

import jax
import jax.numpy as jnp
from jax import lax
from jax.experimental import pallas as pl
from jax.experimental.pallas import tpu as pltpu

F32 = jnp.float32
BF16 = jnp.bfloat16
HI = lax.Precision.HIGHEST

NSA_HEADS = 8
NSA_KV_HEADS = 2
NSA_GROUP = NSA_HEADS // NSA_KV_HEADS
HEAD_DIM = 64
NSA_Q_W = NSA_HEADS * HEAD_DIM
NSA_KV_W = NSA_KV_HEADS * HEAD_DIM
CMP_STRIDE = 16
CMP_LEN = 32
SEL_BLOCK = 64
N_SEL = 8
WINDOW = 512
SEL_FORCE = 1e4
CONV_CH = 512
CONV_K = 31
RWKV_HEADS = 8
RWKV_W = RWKV_HEADS * HEAD_DIM
DECAY_LORA = 64
ICLR_LORA = 64
GATE_LORA = 160
VRES_LORA = 32
N_EXPERTS = 8
ROPE_THETA = 10000.0
RMS_EPS = 1e-6
LN_EPS = 1e-5
GN_EPS = 64e-5
NEG_INF = -1e30
LOG2E = 1.4426950408889634
ACC_ROWS = 72

NSA_IN = NSA_Q_W + 6 * NSA_KV_W + 3 * NSA_HEADS
NSA_IN_PAD = 1408
NSA_GATE_BLK = (NSA_Q_W + 6 * NSA_KV_W) // 128
RWKV_IN_PAD = 1920
LANE = 128
VMEM_LIMIT = 56 * 1024 * 1024

CHUNK = 64


def _dot(a, b, precision=None):
    return jnp.dot(a, b, preferred_element_type=F32, precision=precision)


def _dot_nt(a, b, precision=None):
    return lax.dot_general(a, b, (((1,), (1,)), ((), ())), preferred_element_type=F32,
                           precision=precision)


def _dot_tn(a, b, precision=None):
    return lax.dot_general(a, b, (((0,), (0,)), ((), ())), preferred_element_type=F32,
                           precision=precision)


def _sigmoid(x):
    return 1.0 / (1.0 + jnp.exp(-x))


def _params(*sem):
    return pltpu.CompilerParams(dimension_semantics=sem, vmem_limit_bytes=VMEM_LIMIT)


def _rms_proj_kernel(x_ref, g_ref, *refs):
    n = len(refs) // 2
    x = x_ref[...]
    ms = jnp.mean(x * x, axis=-1, keepdims=True)
    h = (x * lax.rsqrt(ms + RMS_EPS) * g_ref[...]).astype(BF16)
    for w_ref, o_ref in zip(refs[:n], refs[n:]):
        o_ref[...] = _dot(h, w_ref[...])


def _rms_proj(x, g, ws, tm):
    T, D = x.shape
    once = dict(pipeline_mode=pl.Buffered(1))
    return pl.pallas_call(
        _rms_proj_kernel,
        grid=(T // tm,),
        in_specs=[pl.BlockSpec((tm, D), lambda i: (i, 0)),
                  pl.BlockSpec((1, D), lambda i: (0, 0))]
                 + [pl.BlockSpec(w.shape, lambda i: (0, 0), **once) for w in ws],
        out_specs=[pl.BlockSpec((tm, w.shape[1]), lambda i: (i, 0)) for w in ws],
        out_shape=[jax.ShapeDtypeStruct((T, w.shape[1]), F32) for w in ws],
        compiler_params=_params("parallel"),
        name="rms_proj",
    )(x, g.reshape(1, D), *ws)


def _rope_dense(x, cos, sin_signed):
    w = x.shape[1]
    lane = lax.broadcasted_iota(jnp.int32, x.shape, 1)
    first_half = (lane % HEAD_DIM) < (HEAD_DIM // 2)
    rot = jnp.where(first_half, pltpu.roll(x, w - HEAD_DIM // 2, 1), pltpu.roll(x, HEAD_DIM // 2, 1))
    return x * cos + rot * sin_signed


def _head_rms(x, bd, g):
    ms = _dot_split_lhs(x * x, bd, 3)
    return x * lax.rsqrt(ms + RMS_EPS) * g


def _nsa_prep_kernel(p_ref, cos_ref, sin_ref, gq_ref, gks_ref, gkw_ref, bd_ref,
                     q_ref, kc_ref, vc_ref, ks_ref, vs_ref, kw_ref, vw_ref):
    cos = cos_ref[...]
    sin = sin_ref[...]
    bd = bd_ref[...]
    q = p_ref[0, :, 0:NSA_Q_W]
    q = _rope_dense(_head_rms(q, bd, gq_ref[...]), cos, sin) * (HEAD_DIM ** -0.5 * LOG2E)
    for h in range(NSA_HEADS):
        q_ref[0, h] = q[:, h * HEAD_DIM:(h + 1) * HEAD_DIM].astype(BF16)

    def slab(i):
        return p_ref[0, :, NSA_Q_W + i * NSA_KV_W:NSA_Q_W + (i + 1) * NSA_KV_W]

    cos_k = cos[:, :NSA_KV_W]
    sin_k = sin[:, :NSA_KV_W]
    bd_k = bd[:NSA_KV_W, :NSA_KV_W]
    ks = _rope_dense(_head_rms(slab(2), bd_k, gks_ref[...]), cos_k, sin_k)
    kw = _rope_dense(_head_rms(slab(4), bd_k, gkw_ref[...]), cos_k, sin_k)
    for val, ref in ((slab(0), kc_ref), (slab(1), vc_ref), (ks, ks_ref), (kw, kw_ref)):
        for g in range(NSA_KV_HEADS):
            ref[0, g] = val[:, g * HEAD_DIM:(g + 1) * HEAD_DIM].astype(BF16)
    lane = lax.broadcasted_iota(jnp.int32, (1, NSA_KV_W), 1)
    tail = jnp.where(lane == HEAD_DIM, 1.0, 0.0)
    for val, ref in ((slab(3), vs_ref), (slab(5), vw_ref)):
        for g in range(NSA_KV_HEADS):
            head_first = val if g == 0 else pltpu.roll(val, NSA_KV_W - g * HEAD_DIM, 1)
            ref[0, g] = jnp.where(lane < HEAD_DIM, head_first, tail).astype(BF16)


def _nsa_prep(p, cos, sin, qk_g, ts):
    B, S, _ = p.shape
    bd = jnp.kron(jnp.eye(NSA_HEADS, dtype=F32), jnp.full((HEAD_DIM, HEAD_DIM), 1.0 / HEAD_DIM, F32)).astype(BF16)
    gq = jnp.tile(qk_g[0], NSA_HEADS).reshape(1, NSA_Q_W)
    gks = jnp.tile(qk_g[2], NSA_KV_HEADS).reshape(1, NSA_KV_W)
    gkw = jnp.tile(qk_g[3], NSA_KV_HEADS).reshape(1, NSA_KV_W)
    kv_shape = jax.ShapeDtypeStruct((B, NSA_KV_HEADS, S, HEAD_DIM), BF16)
    kv_spec = pl.BlockSpec((1, NSA_KV_HEADS, ts, HEAD_DIM), lambda b, i: (b, 0, i, 0))
    aug_shape = jax.ShapeDtypeStruct((B, NSA_KV_HEADS, S, NSA_KV_W), BF16)
    aug_spec = pl.BlockSpec((1, NSA_KV_HEADS, ts, NSA_KV_W), lambda b, i: (b, 0, i, 0))
    const = lambda shape: pl.BlockSpec(shape, lambda b, i: (0,) * len(shape))
    return pl.pallas_call(
        _nsa_prep_kernel,
        grid=(B, S // ts),
        in_specs=[pl.BlockSpec((1, ts, NSA_IN_PAD), lambda b, i: (b, i, 0)),
                  pl.BlockSpec((ts, NSA_Q_W), lambda b, i: (i, 0)),
                  pl.BlockSpec((ts, NSA_Q_W), lambda b, i: (i, 0)),
                  const((1, NSA_Q_W)), const((1, NSA_KV_W)), const((1, NSA_KV_W)),
                  const((NSA_Q_W, NSA_Q_W))],
        out_specs=[pl.BlockSpec((1, NSA_HEADS, ts, HEAD_DIM), lambda b, i: (b, 0, i, 0)),
                   kv_spec, kv_spec, kv_spec, aug_spec, kv_spec, aug_spec],
        out_shape=[jax.ShapeDtypeStruct((B, NSA_HEADS, S, HEAD_DIM), BF16),
                   kv_shape, kv_shape, kv_shape, aug_shape, kv_shape, aug_shape],
        compiler_params=_params("parallel", "parallel"),
        name="nsa_prep",
    )(p, cos, sin, gq, gks, gkw, bd)


def _gelu_tanh(x):
    return 0.5 * x * (1.0 + jnp.tanh(0.7978845608028654 * (x + 0.044715 * x * x * x)))


def _compress_kernel(a_ref, posa_ref, posb_ref, w1a_ref, w1b_ref, w2_ref, g_ref, cos_ref, sin_ref,
                     perm_ref, o_ref, *, is_key):
    a = a_ref[0]
    ncp = a.shape[0]
    w1a = w1a_ref[...]
    w1b = w1b_ref[...]
    p1 = _dot(a, w1a)
    p2 = _dot(a, w1b)
    hp = _dot(posa_ref[...], w1a) + _dot(posb_ref[...], w1b)
    h = _gelu_tanh(p1 + pltpu.roll(p2, ncp - 1, 0) + hp[0:1])
    o = _dot(h.astype(BF16), w2_ref[...])
    if is_key:
        ms = jnp.mean(o * o, axis=-1, keepdims=True)
        o = o * lax.rsqrt(ms + RMS_EPS) * g_ref[...]
        o = o * cos_ref[...] + _dot(o, perm_ref[...], precision=HI) * sin_ref[...]
    row = lax.broadcasted_iota(jnp.int32, o.shape, 0)
    o_ref[0] = jnp.where(row < ncp - 1, o, 0.0).astype(BF16)


def _compress(t, pos_emb, w1, w2, g, cos_c, sin_c, is_key):
    BG, ncp, cw = t.shape
    hid = w1.shape[1]
    pos = pos_emb.reshape(1, CMP_LEN * HEAD_DIM)
    posa = jnp.tile(pos[:, :cw], (8, 1)).astype(BF16)
    posb = jnp.tile(pos[:, cw:], (8, 1)).astype(BF16)
    half = HEAD_DIM // 2
    perm = jnp.roll(jnp.eye(HEAD_DIM, dtype=F32), half, axis=0)
    const = lambda shape: pl.BlockSpec(shape, lambda b: (0,) * len(shape))
    return pl.pallas_call(
        functools.partial(_compress_kernel, is_key=is_key),
        grid=(BG,),
        in_specs=[pl.BlockSpec((1, ncp, cw), lambda b: (b, 0, 0)),
                  const((8, cw)), const((8, cw)), const((cw, hid)), const((cw, hid)),
                  const((hid, HEAD_DIM)), const((1, HEAD_DIM)),
                  const((ncp, HEAD_DIM)), const((ncp, HEAD_DIM)), const((HEAD_DIM, HEAD_DIM))],
        out_specs=pl.BlockSpec((1, ncp, HEAD_DIM), lambda b: (b, 0, 0)),
        out_shape=jax.ShapeDtypeStruct((BG, ncp, HEAD_DIM), BF16),
        compiler_params=_params("parallel"),
        name="nsa_compress_k" if is_key else "nsa_compress_v",
    )(t, posa, posb, w1[:cw].astype(BF16), w1[cw:].astype(BF16), w2.astype(BF16),
      g.reshape(1, HEAD_DIM), cos_c, sin_c, perm)


def _exp_cols(s):
    m = jnp.max(s, axis=0, keepdims=True)
    e = jnp.exp2(s - m)
    return e, jnp.sum(e, axis=0, keepdims=True)


def _heads_to_rows(o, tq):
    return jnp.concatenate([o[:, h * tq:(h + 1) * tq] for h in range(NSA_GROUP)], axis=0)


def _nsa_attn_kernel(q_ref, kc_ref, vc_ref, ks_ref, vs_ref, kw_ref, vw_ref, gl_ref, o_ref, selb_ref, s_ref,
                     *, tq, tk, seq):
    g = pl.program_id(1)
    i = pl.program_id(2)
    rows = NSA_GROUP * tq
    ncp = seq // CMP_STRIDE
    ns = seq // SEL_BLOCK
    n_sel = min(N_SEL, ns)
    q = q_ref[0].reshape(rows, HEAD_DIM)
    t0 = i * tq
    t_q = t0 + lax.broadcasted_iota(jnp.int32, (1, tq), 1)

    def per_head(x):
        return jnp.concatenate([x] * NSA_GROUP, axis=1)

    c_end = lax.broadcasted_iota(jnp.int32, (ncp, 1), 0) * CMP_STRIDE + (CMP_LEN - 1)
    bias = jnp.where(c_end <= t_q, 0.0, NEG_INF)
    e, l = _exp_cols(_dot_nt(kc_ref[0], q) + per_head(bias))
    p_c = e * (per_head(jnp.where(t_q >= CMP_LEN - 1, 1.0, 0.0)) / l)
    o_c = _dot_tn(vc_ref[0], p_c.astype(BF16))

    p_sum = p_c[:, 0:tq]
    for h in range(1, NSA_GROUP):
        p_sum = p_sum + p_c[:, h * tq:(h + 1) * tq]
    s0 = lax.broadcasted_iota(jnp.int32, (ns, ncp), 0) * SEL_BLOCK
    c0 = lax.broadcasted_iota(jnp.int32, (ns, ncp), 1) * CMP_STRIDE
    overlap = jnp.where((c0 < s0 + SEL_BLOCK) & (c0 + CMP_LEN > s0), 1.0, 0.0).astype(BF16)
    imp = _dot_split_rhs(overlap, p_sum, 2)
    j = lax.broadcasted_iota(jnp.int32, (ns, tq), 0).astype(F32)
    cur = (t_q // SEL_BLOCK).astype(F32)
    forced = (j == 0.0) | (j == cur) | (j == cur - 1.0)
    score = jnp.where(forced, SEL_FORCE, jnp.where(j <= cur, imp, -1.0))
    for _ in range(n_sel):
        mx = jnp.max(score, axis=0, keepdims=True)
        first = jnp.min(jnp.where(score == mx, j, float(ns)), axis=0, keepdims=True)
        score = jnp.where(j == first, -jnp.inf, score)
    selb_ref[...] = jnp.where(score == -jnp.inf, 0.0, NEG_INF)

    wlen = WINDOW + tq
    w0 = pl.multiple_of(jnp.maximum(t0 - WINDOW, 0), tq)
    kpos = w0 + lax.broadcasted_iota(jnp.int32, (wlen, 1), 0)
    bias = jnp.where((kpos <= t_q) & (kpos > t_q - WINDOW), 0.0, NEG_INF)
    s = _dot_nt(kw_ref[0, 0, pl.ds(w0, wlen), :], q) + per_head(bias)
    e = jnp.exp2(s - jnp.max(s, axis=0, keepdims=True))
    num = _dot_tn(vw_ref[0, 0, pl.ds(w0, wlen), :], e.astype(BF16))[0:ACC_ROWS]
    o_w = num[0:HEAD_DIM] / num[HEAD_DIM:HEAD_DIM + 1]

    key_off = lax.broadcasted_iota(jnp.int32, (SEL_BLOCK, 1), 0)

    last_tile = seq // tk - 1

    def scores(jt, slot):
        jt = jnp.minimum(jt, last_tile)
        k0 = pl.multiple_of(jt * tk, tk)
        bias = []
        for b in range(tk // SEL_BLOCK):
            picked = selb_ref[pl.ds(jt * (tk // SEL_BLOCK) + b, 1), :]
            causal = k0 + b * SEL_BLOCK + key_off <= t_q
            bias.append(jnp.where(causal, picked, NEG_INF))
        bias = jnp.concatenate(bias, axis=0)
        s_ref[slot] = _dot_nt(ks_ref[0, 0, pl.ds(k0, tk), :], q) + per_head(bias)

    def absorb(jt, slot, carry):
        m_run, acc = carry
        k0 = pl.multiple_of(jt * tk, tk)
        s = s_ref[slot]
        m_new = jnp.maximum(m_run, jnp.max(s, axis=0, keepdims=True))
        e = jnp.exp2(s - m_new)
        alpha = jnp.exp2(m_run - m_new)
        acc = alpha * acc + _dot_tn(vs_ref[0, 0, pl.ds(k0, tk), :], e.astype(BF16))[0:ACC_ROWS]
        return m_new, acc

    def sel_pair(it, carry):
        scores(2 * it + 1, 1)
        carry = absorb(2 * it, 0, carry)
        scores(2 * it + 2, 0)
        return absorb(2 * it + 1, 1, carry)

    n_tiles = (t0 + tq - 1) // tk + 1
    init = (jnp.full((1, rows), NEG_INF, F32), jnp.zeros((ACC_ROWS, rows), F32))
    scores(0, 0)
    _, acc_s = lax.fori_loop(0, (n_tiles + 1) // 2, sel_pair, init)
    o_s = acc_s[0:HEAD_DIM] / acc_s[HEAD_DIM:HEAD_DIM + 1]

    gw = NSA_GROUP * HEAD_DIM
    n = lax.broadcasted_iota(jnp.int32, (3 * gw, LANE), 0)
    col = lax.broadcasted_iota(jnp.int32, (3 * gw, LANE), 1)
    head = g * NSA_GROUP + (n % gw) // HEAD_DIM
    spread = jnp.where(col == head * 3 + n // gw, 1.0, 0.0).astype(BF16)
    gates = None
    for piece in _split_bf16(_sigmoid(gl_ref[0]), 3):
        part = _dot_nt(spread, piece)
        gates = part if gates is None else gates + part
    out = gates[0:gw] * _heads_to_rows(o_c, tq)
    out = out + gates[gw:2 * gw] * _heads_to_rows(o_s, tq)
    out = out + gates[2 * gw:] * _heads_to_rows(o_w, tq)
    o_ref[0] = out.T.astype(BF16)


def _nsa_attn(q, kc, vc, ks, vs, kw, vw, p_nsa, tq, tk):
    B, H, S, _ = q.shape
    G = NSA_KV_HEADS
    ncp = S // CMP_STRIDE
    assert S % (2 * tk) == 0
    cmp_spec = pl.BlockSpec((1, ncp, HEAD_DIM), lambda b, g, i: (b * G + g, 0, 0))
    seq_spec = pl.BlockSpec((1, 1, S, HEAD_DIM), lambda b, g, i: (b, g, 0, 0))
    aug_spec = pl.BlockSpec((1, 1, S, NSA_KV_W), lambda b, g, i: (b, g, 0, 0))
    return pl.pallas_call(
        functools.partial(_nsa_attn_kernel, tq=tq, tk=tk, seq=S),
        grid=(B, G, S // tq),
        in_specs=[pl.BlockSpec((1, NSA_GROUP, tq, HEAD_DIM), lambda b, g, i: (b, g, i, 0)),
                  cmp_spec, cmp_spec, seq_spec, aug_spec, seq_spec, aug_spec,
                  pl.BlockSpec((1, tq, LANE), lambda b, g, i: (b, i, NSA_GATE_BLK))],
        out_specs=pl.BlockSpec((1, tq, NSA_GROUP * HEAD_DIM), lambda b, g, i: (b, i, g)),
        out_shape=jax.ShapeDtypeStruct((B, S, NSA_Q_W), BF16),
        scratch_shapes=[pltpu.VMEM((S // SEL_BLOCK, tq), F32),
                        pltpu.VMEM((2, tk, NSA_GROUP * tq), F32)],
        compiler_params=_params("parallel", "parallel", "arbitrary"),
        name="nsa_attn",
    )(q, kc, vc, ks, vs, kw, vw, p_nsa)


def _rope_tables(S):
    half = HEAD_DIM // 2
    inv = ROPE_THETA ** (-jnp.arange(half, dtype=F32) / half)
    ang = jnp.arange(S, dtype=F32)[:, None] * inv[None, :]
    cos, sin = jnp.cos(ang), jnp.sin(ang)
    cos64 = jnp.concatenate([cos, cos], axis=1)
    sin64 = jnp.concatenate([-sin, sin], axis=1)
    return cos64, sin64


def _nsa_branch(p_nsa, qk_g, cmp_pos, cmp_w1, cmp_w2, B, S):
    cos64, sin64 = _rope_tables(S)
    cos = jnp.tile(cos64, (1, NSA_HEADS))
    sin = jnp.tile(sin64, (1, NSA_HEADS))
    q, kc, vc, ks, vs, kw, vw = _nsa_prep(p_nsa, cos, sin, qk_g, ts=min(256, S))
    ncp = S // CMP_STRIDE
    cmp_end = jnp.minimum(jnp.arange(ncp) * CMP_STRIDE + CMP_LEN - 1, S - 1)
    cw = CMP_STRIDE * HEAD_DIM
    kc = kc.reshape(B * NSA_KV_HEADS, ncp, cw)
    vc = vc.reshape(B * NSA_KV_HEADS, ncp, cw)
    k_cmp = _compress(kc, cmp_pos[0], cmp_w1[0], cmp_w2[0], qk_g[1], cos64[cmp_end], sin64[cmp_end], True)
    v_cmp = _compress(vc, cmp_pos[1], cmp_w1[1], cmp_w2[1], qk_g[1], cos64[cmp_end], sin64[cmp_end], False)
    return _nsa_attn(q, k_cmp, v_cmp, ks, vs, kw, vw, p_nsa, tq=256, tk=256)


CONV_HALO = 32


def _conv_kernel(cur_ref, halo_ref, w_ref, vec_ref, o_ref, buf, y_scr, shifted, *, ts):
    i = pl.program_id(1)

    def glu(p):
        return p[:, :CONV_CH] * _sigmoid(p[:, CONV_CH:])

    buf[0:CONV_HALO, :] = jnp.where(i > 0, glu(halo_ref[0]), 0.0)
    buf[CONV_HALO:, :] = glu(cur_ref[0])
    rt = min(ts, 128)
    shift = CONV_HALO - (CONV_K - 1)
    for c in range(CONV_CH // LANE):
        cs = slice(c * LANE, (c + 1) * LANE)
        for b in range(8):
            span = ts + 8 * ((CONV_K - 1 - b) // 8)
            shifted[b, 0:span, :] = buf[shift + b:shift + b + span, cs]
        for r in range(ts // rt):
            acc = jnp.zeros((rt, LANE), F32)
            for j in range(CONV_K):
                acc = acc + w_ref[j:j + 1, cs] * shifted[j % 8, r * rt + 8 * (j // 8):r * rt + 8 * (j // 8) + rt, :]
            y_scr[r * rt:(r + 1) * rt, cs] = acc
    y = y_scr[...] + vec_ref[0:1, :]
    mu = jnp.mean(y, axis=-1, keepdims=True)
    yc = y - mu
    var = jnp.mean(yc * yc, axis=-1, keepdims=True)
    y = yc * lax.rsqrt(var + LN_EPS) * vec_ref[1:2, :] + vec_ref[2:3, :]
    o_ref[0] = (y * _sigmoid(y)).astype(BF16)


def _conformer(p, conv_w, conv_vec, ts):
    B, S, _ = p.shape
    w = jnp.pad(conv_w, ((0, 32 - CONV_K), (0, 0)))
    vec = jnp.pad(conv_vec, ((0, 5), (0, 0)))
    hb = ts // CONV_HALO
    return pl.pallas_call(
        functools.partial(_conv_kernel, ts=ts),
        grid=(B, S // ts),
        in_specs=[pl.BlockSpec((1, ts, 2 * CONV_CH), lambda b, i: (b, i, 0)),
                  pl.BlockSpec((1, CONV_HALO, 2 * CONV_CH), lambda b, i: (b, jnp.maximum(i * hb - 1, 0), 0)),
                  pl.BlockSpec((32, CONV_CH), lambda b, i: (0, 0)),
                  pl.BlockSpec((8, CONV_CH), lambda b, i: (0, 0))],
        out_specs=pl.BlockSpec((1, ts, CONV_CH), lambda b, i: (b, i, 0)),
        out_shape=jax.ShapeDtypeStruct((B, S, CONV_CH), BF16),
        scratch_shapes=[pltpu.VMEM((CONV_HALO + ts, CONV_CH), F32), pltpu.VMEM((ts, CONV_CH), F32),
                        pltpu.VMEM((8, ts + 8 * ((CONV_K - 1) // 8), LANE), F32)],
        compiler_params=_params("parallel", "parallel"),
        name="conformer",
    )(p, p, w, vec)


RW_LORA_OFF = 3 * RWKV_W
RW_GATE_OFF = RW_LORA_OFF + DECAY_LORA + ICLR_LORA
RW_VRES_OFF = RWKV_IN_PAD - LANE


def _rwkv_prep_kernel(*refs, ts, has_vres):
    if has_vres:
        (cur_ref, halo_ref, mu_ref, vec_ref, wup_ref, aup_ref, gup_ref, bd_ref, vf_ref, vb_ref, vup_ref,
         r_ref, lw_ref, k_ref, v_ref, kk_ref, kka_ref, g_ref) = refs
    else:
        (cur_ref, halo_ref, mu_ref, vec_ref, wup_ref, aup_ref, gup_ref, bd_ref,
         r_ref, lw_ref, k_ref, v_ref, kk_ref, kka_ref, g_ref) = refs
    i = pl.program_id(1)
    p = cur_ref[0]
    last = jnp.where(i > 0, halo_ref[0, 7:8, :], 0.0)
    row = lax.broadcasted_iota(jnp.int32, p.shape, 0)
    prev = jnp.where(row == 0, last, pltpu.roll(p, 1, 0))
    u = p + (prev - p) * mu_ref[...]
    C = RWKV_W
    r, k, v = u[:, :C], u[:, C:2 * C], u[:, 2 * C:3 * C]
    lora = u[:, RW_LORA_OFF:RW_GATE_OFF]
    w_in = vec_ref[0:1, :] + _dot(jnp.tanh(lora).astype(BF16), wup_ref[...])
    sp = jnp.maximum(-w_in, 0.0) + jnp.log(1.0 + jnp.exp(-jnp.abs(w_in)))
    lw = -jnp.exp(-sp - 0.5)
    a = _sigmoid(vec_ref[1:2, :] + _dot(lora.astype(BF16), aup_ref[...]))
    g = _dot(_sigmoid(u[:, RW_GATE_OFF:]).astype(BF16), gup_ref[...])
    if has_vres:
        mix = _sigmoid(vb_ref[...] + _dot(u[:, RW_VRES_OFF:].astype(BF16), vup_ref[...]))
        v = v + (vf_ref[0] - v) * mix
    kk = k * vec_ref[2:3, :]
    ss = _dot_split_lhs(kk * kk, bd_ref[...], 3)
    kk = kk / jnp.maximum(jnp.sqrt(ss), 1e-12)
    k = k * (1.0 + (a - 1.0) * vec_ref[3:4, :])
    for val, ref in ((r, r_ref), (lw, lw_ref), (k, k_ref), (v, v_ref), (kk, kk_ref), (kk * a, kka_ref),
                     (g, g_ref)):
        ref[0] = val


def _pad_rows(w, n):
    return jnp.pad(w, ((0, n - w.shape[0]), (0, 0)))


def _rwkv_prep(p, mu, vec, w_up, a_up, g_up, v_first, vres_up, vres_b, ts):
    B, S, _ = p.shape
    C = RWKV_W
    has_vres = v_first is not None
    mu_p = jnp.pad(mu, (0, RWKV_IN_PAD - mu.shape[0])).reshape(1, RWKV_IN_PAD)
    n_lora = DECAY_LORA + ICLR_LORA
    wup = _pad_rows(w_up, n_lora).astype(BF16)
    aup = jnp.pad(a_up, ((DECAY_LORA, 0), (0, 0))).astype(BF16)
    gup = _pad_rows(g_up, RWKV_IN_PAD - RW_GATE_OFF).astype(BF16)
    bd = jnp.kron(jnp.eye(RWKV_HEADS, dtype=F32), jnp.ones((HEAD_DIM, HEAD_DIM), F32)).astype(BF16)
    const = lambda shape: pl.BlockSpec(shape, lambda b, i: (0,) * len(shape))
    dense = pl.BlockSpec((1, ts, C), lambda b, i: (b, i, 0))
    in_specs = [pl.BlockSpec((1, ts, RWKV_IN_PAD), lambda b, i: (b, i, 0)),
                pl.BlockSpec((1, 8, RWKV_IN_PAD), lambda b, i: (b, jnp.maximum(i * (ts // 8) - 1, 0), 0)),
                const((1, RWKV_IN_PAD)), const((8, C)), const((n_lora, C)), const((n_lora, C)),
                const((RWKV_IN_PAD - RW_GATE_OFF, C)), const((C, C))]
    args = [p, p, mu_p, jnp.pad(vec, ((0, 1), (0, 0))), wup, aup, gup, bd]
    if has_vres:
        off = RWKV_IN_PAD - LANE
        lo = 3 * C + n_lora + GATE_LORA - off
        vup = jnp.pad(vres_up, ((lo, LANE - lo - VRES_LORA), (0, 0))).astype(BF16)
        in_specs += [dense, const((1, C)), const((LANE, C))]
        args += [v_first, vres_b.reshape(1, C), vup]
    return pl.pallas_call(
        functools.partial(_rwkv_prep_kernel, ts=ts, has_vres=has_vres),
        grid=(B, S // ts),
        in_specs=in_specs,
        out_specs=[dense] * 7,
        out_shape=[jax.ShapeDtypeStruct((B, S, C), F32)] * 7,
        compiler_params=_params("parallel", "parallel"),
        name="rwkv_prep",
    )(*args)


GROUP_HEADS = 4
GW = GROUP_HEADS * HEAD_DIM
SR = GROUP_HEADS * CHUNK
SEQ_PARTS = 2
PREP_CHUNKS = 4


def _split_bf16(x, n):
    parts = []
    for _ in range(n - 1):
        h = x.astype(BF16)
        parts.append(h)
        x = x - h.astype(F32)
    parts.append(x.astype(BF16))
    return parts


def _dot_split_rhs(a, x, n):
    parts = _split_bf16(x, n)
    out = _dot(a, parts[0])
    for p in parts[1:]:
        out = out + _dot(a, p)
    return out


def _dot_split_lhs(x, b, n):
    parts = _split_bf16(x, n)
    out = _dot(parts[0], b)
    for p in parts[1:]:
        out = out + _dot(p, b)
    return out


def _unit_lower_inverses(mats, eye, in16, in32):
    b = lambda m: m.astype(BF16)
    ds = [b(jnp.where(in16, a, 0.0)) for a in mats]
    xs = [eye - d for d in ds]
    ps = [b(_dot(d, d)) for d in ds]
    for step in range(3):
        xs = [x + _dot(b(x), p) for x, p in zip(xs, ps)]
        if step < 2:
            ps = [b(_dot(p, p)) for p in ps]
    for lowers in ([jnp.where(in32 & jnp.logical_not(in16), a, 0.0) for a in mats],
                   [jnp.where(in32, 0.0, a) for a in mats]):
        xbs = [b(x) for x in xs]
        mids = [b(_dot(xb, b(lo))) for xb, lo in zip(xbs, lowers)]
        xs = [x - _dot(mid, xb) for x, mid, xb in zip(xs, mids, xbs)]
    return xs


def _rwkv_scan_kernel(r_ref, lw_ref, k_ref, v_ref, kk_ref, kka_ref, g_ref, gn_ref, o_ref, state,
                      kap_s, bt_s, rt_s, arb_s, tinv_s, u_s, y0_s, sv_s, gl_s, y_s, *, tt):
    @pl.when(pl.program_id(1) == 0)
    def _():
        state[...] = jnp.zeros_like(state)

    ri = lax.broadcasted_iota(jnp.int32, (SR, GW), 0)
    ci = lax.broadcasted_iota(jnp.int32, (SR, GW), 1)
    same_head = (ri // CHUNK) == (ci // HEAD_DIM)
    strict = (ci % CHUNK) < (ri % CHUNK)
    incl = (ci % CHUNK) <= (ri % CHUNK)
    in16 = (ri // 16) == (ci // 16)
    in32 = (ri // 32) == (ci // 32)
    eye = jnp.where(ri == ci, 1.0, 0.0)
    head_ones = jnp.where(same_head, 1.0, 0.0).astype(BF16)
    tr = lax.broadcasted_iota(jnp.int32, (CHUNK, CHUNK), 0)
    tc = lax.broadcasted_iota(jnp.int32, (CHUNK, CHUNK), 1)
    tri = jnp.where(tc <= tr, 1.0, 0.0).astype(BF16)
    b = lambda m: m.astype(BF16)

    def stack(x):
        return b(jnp.where(same_head, jnp.concatenate([x] * GROUP_HEADS, axis=0), 0.0))

    n_groups = RWKV_HEADS // GROUP_HEADS

    def chunk_rows(c):
        return pl.ds(pl.multiple_of(c * CHUNK, CHUNK), CHUNK)

    def prepare_pair(it, carry):
        items = [(PREP_CHUNKS * it + u, gi) for u in range(PREP_CHUNKS) for gi in range(n_groups)]
        n = range(len(items))
        kap, bt, kt, rt, vs, gl = [], [], [], [], [], []
        for c, gi in items:
            rows = chunk_rows(c)
            lanes = slice(gi * GW, (gi + 1) * GW)
            lw = lw_ref[0, rows, lanes]
            cs = _dot_split_rhs(tri, lw, 3)
            g_in = jnp.exp(cs)
            g_inv = jnp.exp(-cs)
            kap.append(stack(kk_ref[0, rows, lanes] * jnp.exp(cs - lw)))
            bt.append(stack(kka_ref[0, rows, lanes] * g_inv))
            kt.append(stack(k_ref[0, rows, lanes] * g_inv))
            rt.append(stack(r_ref[0, rows, lanes] * g_in))
            vs.append(stack(v_ref[0, rows, lanes]))
            gl.append(jnp.broadcast_to(g_in[CHUNK - 1:CHUNK, :], (8, GW)))
        a_kb = [jnp.where(strict, _dot_nt(kap[i], bt[i]), 0.0) for i in n]
        a_kk = [b(jnp.where(strict, _dot_nt(kap[i], kt[i]), 0.0)) for i in n]
        a_rk = [b(jnp.where(incl, _dot_nt(rt[i], kt[i]), 0.0)) for i in n]
        a_rb = [b(jnp.where(incl, _dot_nt(rt[i], bt[i]), 0.0)) for i in n]
        u = [_dot(a_kk[i], vs[i]) for i in n]
        y0 = [_dot(a_rk[i], vs[i]) for i in n]
        sv = [_dot_tn(vs[i], kt[i]) for i in n]
        t_inv = _unit_lower_inverses(a_kb, eye, in16, in32)
        for i, (c, gi) in enumerate(items):
            slot = c * n_groups + gi
            kap_s[slot] = kap[i]
            bt_s[slot] = bt[i]
            rt_s[slot] = rt[i]
            arb_s[slot] = a_rb[i]
            tinv_s[slot] = b(t_inv[i])
            u_s[slot] = u[i]
            y0_s[slot] = y0[i]
            sv_s[slot] = sv[i]
            gl_s[slot] = gl[i]
        return carry

    lax.fori_loop(0, tt // (PREP_CHUNKS * CHUNK), prepare_pair, 0)

    def advance(c, carry):
        half = SR // SEQ_PARTS
        items = [(gi, slice(hh * half, (hh + 1) * half)) for gi in range(n_groups) for hh in range(SEQ_PARTS)]
        n = range(len(items))
        blk = lambda ref, i: ref[c * n_groups + items[i][0], items[i][1], items[i][1]]
        s0 = [state[gi, d, d] for gi, d in items]
        s0b = [b(s) for s in s0]
        rhs = [b(_dot_nt(blk(kap_s, i), s0b[i]) + blk(u_s, i)) for i in n]
        pb = [b(_dot(blk(tinv_s, i), rhs[i])) for i in n]
        for i, (gi, d) in enumerate(items):
            decay = gl_s[c * n_groups + gi, 0:1, d]
            state[gi, d, d] = (s0[i] + blk(sv_s, i) - _dot_tn(pb[i], blk(bt_s, i))) * decay
        for i, (gi, d) in enumerate(items):
            y = _dot_nt(blk(rt_s, i), s0b[i]) + blk(y0_s, i) - _dot(blk(arb_s, i), pb[i])
            yd = y[0:CHUNK]
            for h in range(1, half // CHUNK):
                yd = yd + y[h * CHUNK:(h + 1) * CHUNK]
            y_s[chunk_rows(c), gi * GW + d.start:gi * GW + d.stop] = yd
        return carry

    lax.fori_loop(0, tt // CHUNK, advance, 0)

    def finish_pair(it, carry):
        items = [(chunk_rows(2 * it + u), slice(gi * GW, (gi + 1) * GW)) for u in range(2) for gi in range(n_groups)]
        yd = [y_s[rows, lanes] for rows, lanes in items]
        ym = [_dot_split_lhs(y, head_ones, 2) * (1.0 / HEAD_DIM) for y in yd]
        yc = [y - m for y, m in zip(yd, ym)]
        yv = [_dot_split_lhs(c * c, head_ones, 2) * (1.0 / HEAD_DIM) for c in yc]
        rk = [r_ref[0, rows, lanes] * k_ref[0, rows, lanes] * gn_ref[2:3, lanes] for rows, lanes in items]
        bonus = [_dot_split_lhs(x, head_ones, 2) for x in rk]
        for i, (rows, lanes) in enumerate(items):
            yn = yc[i] * lax.rsqrt(yv[i] + GN_EPS) * gn_ref[0:1, lanes] + gn_ref[1:2, lanes]
            o_ref[0, rows, lanes] = b((yn + bonus[i] * v_ref[0, rows, lanes]) * g_ref[0, rows, lanes])
        return carry

    lax.fori_loop(0, tt // (2 * CHUNK), finish_pair, 0)


def _rwkv_scan(r, lw, k, v, kk, kka, g, vec, tt):
    B, S, C = r.shape
    gn = jnp.pad(jnp.stack([vec[5], vec[6], vec[4]]), ((0, 5), (0, 0)))
    dense = pl.BlockSpec((1, tt, C), lambda b, i: (b, i, 0))
    slots = (tt // CHUNK) * (C // GW)
    assert tt % (PREP_CHUNKS * CHUNK) == 0
    stacked = lambda dtype: pltpu.VMEM((slots, SR, GW), dtype)
    return pl.pallas_call(
        functools.partial(_rwkv_scan_kernel, tt=tt),
        grid=(B, S // tt),
        in_specs=[dense] * 7 + [pl.BlockSpec((8, C), lambda b, i: (0, 0))],
        out_specs=dense,
        out_shape=jax.ShapeDtypeStruct((B, S, C), BF16),
        scratch_shapes=[pltpu.VMEM((C // GW, GW, GW), F32)] + [stacked(BF16)] * 5 + [stacked(F32)] * 3
                       + [pltpu.VMEM((slots, 8, GW), F32), pltpu.VMEM((tt, C), F32)],
        compiler_params=_params("parallel", "arbitrary"),
        name="rwkv_scan",
    )(r, lw, k, v, kk, kka, g, gn)


def _merge_kernel(x_ref, an_ref, ac_ref, ar_ref, gl_ref, wn_ref, wc_ref, wr_ref, wo_ref, g2_ref,
                  xo_ref, h_ref):
    D = x_ref.shape[1]
    o_n = _dot(an_ref[...], wn_ref[...])
    mixed = _sigmoid(gl_ref[:, 0:D]) * o_n
    o_c = _dot(ac_ref[...], wc_ref[...])
    mixed = mixed + _sigmoid(gl_ref[:, D:2 * D]) * o_c
    o_r = _dot(ar_ref[...], wr_ref[...])
    mixed = mixed + _sigmoid(gl_ref[:, 2 * D:3 * D]) * o_r
    x = x_ref[...] + _dot(mixed.astype(BF16), wo_ref[...])
    xo_ref[...] = x
    ms = jnp.mean(x * x, axis=-1, keepdims=True)
    h_ref[...] = (x * lax.rsqrt(ms + RMS_EPS) * g2_ref[...]).astype(BF16)


def _merge(x, a_nsa, a_conv, a_rwkv, gl, w_n, w_c, w_r, w_o, g2, tm):
    T, D = x.shape
    row = lambda w: pl.BlockSpec((tm, w), lambda i: (i, 0))
    const = lambda shape: pl.BlockSpec(shape, lambda i: (0,) * len(shape))
    return pl.pallas_call(
        _merge_kernel,
        grid=(T // tm,),
        in_specs=[row(D), row(a_nsa.shape[1]), row(a_conv.shape[1]), row(a_rwkv.shape[1]), row(3 * D),
                  const(w_n.shape), const(w_c.shape), const(w_r.shape), const(w_o.shape), const((1, D))],
        out_specs=[row(D), row(D)],
        out_shape=[jax.ShapeDtypeStruct((T, D), F32), jax.ShapeDtypeStruct((T, D), BF16)],
        compiler_params=_params("parallel"),
        name="merge",
    )(x, a_nsa, a_conv, a_rwkv, gl, w_n.astype(BF16), w_c.astype(BF16), w_r.astype(BF16),
      w_o.astype(BF16), g2.reshape(1, D))


def _cast_kernel(w_ref, o_ref):
    o_ref[...] = w_ref[...].astype(BF16)


def _interleave_gate_up(w_gu, tf):
    E, d, two_ff = w_gu.shape
    nj = two_ff // (2 * tf)
    return pl.pallas_call(
        _cast_kernel,
        grid=(E, nj, 2),
        in_specs=[pl.BlockSpec((1, d, tf), lambda e, j, half: (e, 0, half * nj + j))],
        out_specs=pl.BlockSpec((1, d, tf), lambda e, j, half: (e, 0, 2 * j + half)),
        out_shape=jax.ShapeDtypeStruct((E, d, two_ff), BF16),
        compiler_params=_params("parallel", "parallel", "parallel"),
        name="ffn_weight_layout",
    )(w_gu)


def _ffn_kernel(x_ref, h_ref, wgu_ref, wd_ref, o_ref):
    j = pl.program_id(1)
    h = h_ref[...]
    gu = _dot(h, wgu_ref[...])
    tf = gu.shape[1] // 2
    gate, up = gu[:, :tf], gu[:, tf:]
    act = (gate * _sigmoid(gate) * up).astype(BF16)
    y = _dot(act, wd_ref[...])

    @pl.when(j == 0)
    def _():
        o_ref[...] = x_ref[...] + y

    @pl.when(j > 0)
    def _():
        o_ref[...] += y


def _ffn(x, h, w_gu, w_down, tm, tf):
    T, D = x.shape
    FF = w_down.shape[0]
    wgu = _interleave_gate_up(w_gu[None], tf)[0]
    return pl.pallas_call(
        _ffn_kernel,
        grid=(T // tm, FF // tf),
        in_specs=[pl.BlockSpec((tm, D), lambda i, j: (i, 0)),
                  pl.BlockSpec((tm, D), lambda i, j: (i, 0)),
                  pl.BlockSpec((D, 2 * tf), lambda i, j: (0, j)),
                  pl.BlockSpec((tf, D), lambda i, j: (j, 0))],
        out_specs=pl.BlockSpec((tm, D), lambda i, j: (i, 0)),
        out_shape=jax.ShapeDtypeStruct((T, D), F32),
        compiler_params=_params("parallel", "arbitrary"),
        name="ffn_dense",
    )(x, h, wgu, w_down.astype(BF16))


MOE_ROWS = 256
MOE_TAIL = 128
RANK_BLK = 256


def _router_kernel(h_ref, w_ref, b_ref, gw_ref, pos_ref, cnt_ref):
    logits = _dot(h_ref[...], w_ref[...]) + b_ref[...]
    tt = logits.shape[0]
    lane = lax.broadcasted_iota(jnp.int32, logits.shape, 1)
    m1 = jnp.max(logits, axis=-1, keepdims=True)
    i1 = jnp.min(jnp.where(logits == m1, lane, LANE), axis=-1, keepdims=True)
    rest = jnp.where(lane == i1, -jnp.inf, logits)
    m2 = jnp.max(rest, axis=-1, keepdims=True)
    i2 = jnp.min(jnp.where(rest == m2, lane, LANE), axis=-1, keepdims=True)
    e2 = jnp.exp(m2 - m1)
    w1 = 1.0 / (1.0 + e2)
    w2 = e2 / (1.0 + e2)
    gw_ref[...] = jnp.where(lane == i1, w1, jnp.where(lane == i2, w2, 0.0))
    routed = (lane == i1) | (lane == i2)
    r = lax.broadcasted_iota(jnp.int32, (RANK_BLK, RANK_BLK), 0)
    c = lax.broadcasted_iota(jnp.int32, (RANK_BLK, RANK_BLK), 1)
    before = jnp.where(c < r, 1.0, 0.0).astype(BF16)
    run = jnp.zeros((1, LANE), F32)
    ranks = []
    for blk in range(tt // RANK_BLK):
        hit = jnp.where(routed[blk * RANK_BLK:(blk + 1) * RANK_BLK], 1.0, 0.0)
        ranks.append(_dot(before, hit.astype(BF16)) + run)
        run = run + jnp.sum(hit, axis=0, keepdims=True)
    pos = jnp.where(routed, jnp.concatenate(ranks, axis=0), -1.0)
    pos_ref[0] = pos.T[0:N_EXPERTS]
    cnt_ref[0] = jnp.broadcast_to(run, (8, LANE))


def _router(h, router_w, router_b, tt):
    T, D = h.shape
    nt = T // tt
    w = _pad_cols(router_w, LANE).astype(BF16)
    b = jnp.pad(router_b, (0, LANE - N_EXPERTS), constant_values=-jnp.inf).reshape(1, LANE)
    return pl.pallas_call(
        _router_kernel,
        grid=(nt,),
        in_specs=[pl.BlockSpec((tt, D), lambda i: (i, 0)),
                  pl.BlockSpec((D, LANE), lambda i: (0, 0)),
                  pl.BlockSpec((1, LANE), lambda i: (0, 0))],
        out_specs=[pl.BlockSpec((tt, LANE), lambda i: (i, 0)),
                   pl.BlockSpec((1, N_EXPERTS, tt), lambda i: (i, 0, 0)),
                   pl.BlockSpec((1, 8, LANE), lambda i: (i, 0, 0))],
        out_shape=[jax.ShapeDtypeStruct((T, LANE), F32),
                   jax.ShapeDtypeStruct((nt, N_EXPERTS, tt), F32),
                   jax.ShapeDtypeStruct((nt, 8, LANE), F32)],
        compiler_params=_params("parallel"),
        name="moe_router",
    )(h, w, b)


def _moe_kernel(nsb_ref, x_ref, h_ref, gw_ref, pos_ref, wgu_ref, wd_ref, o_ref, xc, yc):
    i = pl.program_id(0)
    e = pl.program_id(1)
    j = pl.program_id(2)
    n_full = nsb_ref[0, i * pl.num_programs(1) + e]
    has_tail = nsb_ref[1, i * pl.num_programs(1) + e]

    @pl.when((e == 0) & (j == 0))
    def _():
        o_ref[...] = x_ref[...]

    pos = pos_ref[0, pl.ds(e, 1), :]

    def for_blocks(fn):
        def body(sb, carry):
            fn(pl.multiple_of(sb * MOE_ROWS, MOE_ROWS), MOE_ROWS)
            return carry
        lax.fori_loop(0, n_full, body, 0)

        @pl.when(has_tail == 1)
        def _():
            fn(pl.multiple_of(n_full * MOE_ROWS, MOE_TAIL), MOE_TAIL)

    def select(r0, n):
        slot = lax.broadcasted_iota(jnp.int32, (n, 1), 0) + r0
        return jnp.where(pos == slot.astype(F32), 1.0, 0.0).astype(BF16)

    def gather(r0, n):
        xc[pl.ds(r0, n), :] = _dot(select(r0, n), h_ref[...]).astype(BF16)
        yc[pl.ds(r0, n), :] = jnp.zeros((n, yc.shape[1]), F32)

    def expert(r0, n):
        gu = _dot(xc[pl.ds(r0, n), :], wgu_ref[0])
        tf = gu.shape[1] // 2
        gate, up = gu[:, :tf], gu[:, tf:]
        act = (gate * _sigmoid(gate) * up).astype(BF16)
        yc[pl.ds(r0, n), :] += _dot(act, wd_ref[0])

    @pl.when(j == 0)
    def _():
        for_blocks(gather)

    for_blocks(expert)

    @pl.when(j == pl.num_programs(2) - 1)
    def _():
        gw = gw_ref[...]
        lane = lax.broadcasted_iota(jnp.int32, gw.shape, 1)
        w_e = jnp.sum(jnp.where(lane == e, gw, 0.0), axis=-1, keepdims=True)

        def scatter(r0, n):
            o_ref[...] += w_e * _dot_tn(select(r0, n), yc[pl.ds(r0, n), :].astype(BF16))
        for_blocks(scatter)


def _moe(x, h, gw, pos, counts, w_gu, w_down, tt, tf):
    T, D = x.shape
    E, FF, _ = w_down.shape
    wgu = _interleave_gate_up(w_gu, tf)
    n_tail = ((counts[:, 0, :E].astype(jnp.int32) + MOE_TAIL - 1) // MOE_TAIL).reshape(-1)
    per_full = MOE_ROWS // MOE_TAIL
    n_sb = jnp.stack([n_tail // per_full, n_tail % per_full])
    once = dict(pipeline_mode=pl.Buffered(1))
    return pl.pallas_call(
        _moe_kernel,
        grid_spec=pltpu.PrefetchScalarGridSpec(
            num_scalar_prefetch=1,
            grid=(T // tt, E, FF // tf),
            in_specs=[pl.BlockSpec((tt, D), lambda i, e, j, n: (i, 0), **once),
                      pl.BlockSpec((tt, D), lambda i, e, j, n: (i, 0), **once),
                      pl.BlockSpec((tt, LANE), lambda i, e, j, n: (i, 0)),
                      pl.BlockSpec((1, E, tt), lambda i, e, j, n: (i, 0, 0)),
                      pl.BlockSpec((1, D, 2 * tf), lambda i, e, j, n: (e, 0, j)),
                      pl.BlockSpec((1, tf, D), lambda i, e, j, n: (e, j, 0))],
            out_specs=pl.BlockSpec((tt, D), lambda i, e, j, n: (i, 0), **once),
            scratch_shapes=[pltpu.VMEM((tt, D), BF16), pltpu.VMEM((tt, D), F32)]),
        out_shape=jax.ShapeDtypeStruct((T, D), F32),
        compiler_params=_params("parallel", "arbitrary", "arbitrary"),
        name="moe_experts",
    )(n_sb, x, h, gw, pos, wgu, w_down.astype(BF16))


def _pad_cols(w, n):
    return jnp.pad(w, ((0, 0), (0, n - w.shape[1])))


def kernel(x, mix_norm_g, w_in_first, w_in_rest, rwkv_mu_first, rwkv_mu_rest, nsa_qk_g, nsa_cmp_pos, nsa_cmp_w1, nsa_cmp_w2, nsa_w_o, conv_w, conv_vec, conv_w_o, rwkv_vec, rwkv_w_up, rwkv_a_up, rwkv_g_up, rwkv_vres_up, rwkv_vres_b, rwkv_w_o, w_out, ffn_norm_g, dense_w_gu, dense_w_down, router_w, router_b, moe_w_gu, moe_w_down):
    B, S, D = x.shape
    T = B * S
    depth = mix_norm_g.shape[0]
    off_conv = NSA_IN
    off_gate = off_conv + 2 * CONV_CH
    off_rwkv = off_gate + 3 * D
    tm = min(512, T)
    xt = x.reshape(T, D)
    v_first = None
    for layer in range(depth):
        if layer == 0:
            w_in, mu, vres_up, vres_b = w_in_first, rwkv_mu_first, None, None
        else:
            w_in, mu = w_in_rest[layer - 1], rwkv_mu_rest[layer - 1]
            vres_up, vres_b = rwkv_vres_up[layer - 1], rwkv_vres_b[layer - 1]
        g1 = mix_norm_g[layer]
        w_nsa = _pad_cols(w_in[:, :off_conv], NSA_IN_PAD).astype(BF16)
        w_conv = w_in[:, off_conv:off_gate].astype(BF16)
        w_gate = w_in[:, off_gate:off_rwkv].astype(BF16)
        w_rwkv = _pad_cols(w_in[:, off_rwkv:], RWKV_IN_PAD).astype(BF16)
        p_nsa, p_conv, p_gate, p_rwkv = _rms_proj(xt, g1, [w_nsa, w_conv, w_gate, w_rwkv], min(256, T))
        p_nsa = p_nsa.reshape(B, S, NSA_IN_PAD)
        p_conv = p_conv.reshape(B, S, 2 * CONV_CH)
        p_rwkv = p_rwkv.reshape(B, S, RWKV_IN_PAD)

        a_nsa = _nsa_branch(p_nsa, nsa_qk_g[layer], nsa_cmp_pos[layer], nsa_cmp_w1[layer],
                            nsa_cmp_w2[layer], B, S)
        a_conv = _conformer(p_conv, conv_w[layer], conv_vec[layer], ts=min(512, S))
        r, lw, k, v, kk, kka, g = _rwkv_prep(
            p_rwkv, mu, rwkv_vec[layer], rwkv_w_up[layer], rwkv_a_up[layer], rwkv_g_up[layer],
            v_first, vres_up, vres_b, ts=min(256, S))
        if layer == 0:
            v_first = v
        a_rwkv = _rwkv_scan(r, lw, k, v, kk, kka, g, rwkv_vec[layer], tt=min(512, S))

        xt, h2 = _merge(xt, a_nsa.reshape(T, NSA_Q_W), a_conv.reshape(T, CONV_CH),
                        a_rwkv.reshape(T, RWKV_W), p_gate, nsa_w_o[layer], conv_w_o[layer],
                        rwkv_w_o[layer], w_out[layer], ffn_norm_g[layer], tm)
        if layer % 2 == 0:
            xt = _ffn(xt, h2, dense_w_gu[layer // 2], dense_w_down[layer // 2], min(512, T), 1408)
        else:
            tt = min(2048, T)
            gw, pos, counts = _router(h2, router_w[layer // 2], router_b[layer // 2], tt)
            xt = _moe(xt, h2, gw, pos, counts, moe_w_gu[layer // 2], moe_w_down[layer // 2], tt, 896)
    return xt.reshape(B, S, D)
```

```python
import functools

import jax
import jax.numpy as jnp
from jax import lax
from jax.experimental import pallas as pl
from jax.experimental.pallas import tpu as pltpu

F32 = jnp.float32
BF16 = jnp.bfloat16
HI = lax.Precision.HIGHEST

NSA_HEADS = 8
NSA_KV_HEADS = 2
NSA_GROUP = NSA_HEADS // NSA_KV_HEADS
HEAD_DIM = 64
NSA_Q_W = NSA_HEADS * HEAD_DIM
NSA_KV_W = NSA_KV_HEADS * HEAD_DIM
CMP_STRIDE = 16
CMP_LEN = 32
SEL_BLOCK = 64
N_SEL = 8
WINDOW = 512
SEL_FORCE = 1e4
CONV_CH = 512
CONV_K = 31
RWKV_HEADS = 8
RWKV_W = RWKV_HEADS * HEAD_DIM
DECAY_LORA = 64
ICLR_LORA = 64
GATE_LORA = 160
VRES_LORA = 32
N_EXPERTS = 8
ROPE_THETA = 10000.0
RMS_EPS = 1e-6
LN_EPS = 1e-5
GN_EPS = 64e-5
NEG_INF = -1e30
LOG2E = 1.4426950408889634
ACC_ROWS = 72

NSA_IN = NSA_Q_W + 6 * NSA_KV_W + 3 * NSA_HEADS
NSA_IN_PAD = 1408
NSA_GATE_BLK = (NSA_Q_W + 6 * NSA_KV_W) // 128
RWKV_IN_PAD = 1920
LANE = 128
VMEM_LIMIT = 56 * 1024 * 1024

CHUNK = 64


def _dot(a, b, precision=None):
    return jnp.dot(a, b, preferred_element_type=F32, precision=precision)


def _dot_nt(a, b, precision=None):
    return lax.dot_general(a, b, (((1,), (1,)), ((), ())), preferred_element_type=F32,
                           precision=precision)


def _dot_tn(a, b, precision=None):
    return lax.dot_general(a, b, (((0,), (0,)), ((), ())), preferred_element_type=F32,
                           precision=precision)


def _sigmoid(x):
    return 1.0 / (1.0 + jnp.exp(-x))


def _params(*sem):
    return pltpu.CompilerParams(dimension_semantics=sem, vmem_limit_bytes=VMEM_LIMIT)


def _rms_proj_kernel(x_ref, g_ref, *refs):
    n = len(refs) // 2
    x = x_ref[...]
    ms = jnp.mean(x * x, axis=-1, keepdims=True)
    h = (x * lax.rsqrt(ms + RMS_EPS) * g_ref[...]).astype(BF16)
    for w_ref, o_ref in zip(refs[:n], refs[n:]):
        o_ref[...] = _dot(h, w_ref[...])


def _rms_proj(x, g, ws, tm):
    T, D = x.shape
    once = dict(pipeline_mode=pl.Buffered(1))
    return pl.pallas_call(
        _rms_proj_kernel,
        grid=(T // tm,),
        in_specs=[pl.BlockSpec((tm, D), lambda i: (i, 0)),
                  pl.BlockSpec((1, D), lambda i: (0, 0))]
                 + [pl.BlockSpec(w.shape, lambda i: (0, 0), **once) for w in ws],
        out_specs=[pl.BlockSpec((tm, w.shape[1]), lambda i: (i, 0)) for w in ws],
        out_shape=[jax.ShapeDtypeStruct((T, w.shape[1]), F32) for w in ws],
        compiler_params=_params("parallel"),
        name="rms_proj",
    )(x, g.reshape(1, D), *ws)


def _rope_dense(x, cos, sin_signed):
    w = x.shape[1]
    lane = lax.broadcasted_iota(jnp.int32, x.shape, 1)
    first_half = (lane % HEAD_DIM) < (HEAD_DIM // 2)
    rot = jnp.where(first_half, pltpu.roll(x, w - HEAD_DIM // 2, 1), pltpu.roll(x, HEAD_DIM // 2, 1))
    return x * cos + rot * sin_signed


def _head_rms(x, bd, g):
    ms = _dot_split_lhs(x * x, bd, 3)
    return x * lax.rsqrt(ms + RMS_EPS) * g


def _nsa_prep_kernel(p_ref, cos_ref, sin_ref, gq_ref, gks_ref, gkw_ref, bd_ref,
                     q_ref, kc_ref, vc_ref, ks_ref, vs_ref, kw_ref, vw_ref):
    cos = cos_ref[...]
    sin = sin_ref[...]
    bd = bd_ref[...]
    q = p_ref[0, :, 0:NSA_Q_W]
    q = _rope_dense(_head_rms(q, bd, gq_ref[...]), cos, sin) * (HEAD_DIM ** -0.5 * LOG2E)
    for h in range(NSA_HEADS):
        q_ref[0, h] = q[:, h * HEAD_DIM:(h + 1) * HEAD_DIM].astype(BF16)

    def slab(i):
        return p_ref[0, :, NSA_Q_W + i * NSA_KV_W:NSA_Q_W + (i + 1) * NSA_KV_W]

    cos_k = cos[:, :NSA_KV_W]
    sin_k = sin[:, :NSA_KV_W]
    bd_k = bd[:NSA_KV_W, :NSA_KV_W]
    ks = _rope_dense(_head_rms(slab(2), bd_k, gks_ref[...]), cos_k, sin_k)
    kw = _rope_dense(_head_rms(slab(4), bd_k, gkw_ref[...]), cos_k, sin_k)
    for val, ref in ((slab(0), kc_ref), (slab(1), vc_ref), (ks, ks_ref), (kw, kw_ref)):
        for g in range(NSA_KV_HEADS):
            ref[0, g] = val[:, g * HEAD_DIM:(g + 1) * HEAD_DIM].astype(BF16)
    lane = lax.broadcasted_iota(jnp.int32, (1, NSA_KV_W), 1)
    tail = jnp.where(lane == HEAD_DIM, 1.0, 0.0)
    for val, ref in ((slab(3), vs_ref), (slab(5), vw_ref)):
        for g in range(NSA_KV_HEADS):
            head_first = val if g == 0 else pltpu.roll(val, NSA_KV_W - g * HEAD_DIM, 1)
            ref[0, g] = jnp.where(lane < HEAD_DIM, head_first, tail).astype(BF16)


def _nsa_prep(p, cos, sin, qk_g, ts):
    B, S, _ = p.shape
    bd = jnp.kron(jnp.eye(NSA_HEADS, dtype=F32), jnp.full((HEAD_DIM, HEAD_DIM), 1.0 / HEAD_DIM, F32)).astype(BF16)
    gq = jnp.tile(qk_g[0], NSA_HEADS).reshape(1, NSA_Q_W)
    gks = jnp.tile(qk_g[2], NSA_KV_HEADS).reshape(1, NSA_KV_W)
    gkw = jnp.tile(qk_g[3], NSA_KV_HEADS).reshape(1, NSA_KV_W)
    kv_shape = jax.ShapeDtypeStruct((B, NSA_KV_HEADS, S, HEAD_DIM), BF16)
    kv_spec = pl.BlockSpec((1, NSA_KV_HEADS, ts, HEAD_DIM), lambda b, i: (b, 0, i, 0))
    aug_shape = jax.ShapeDtypeStruct((B, NSA_KV_HEADS, S, NSA_KV_W), BF16)
    aug_spec = pl.BlockSpec((1, NSA_KV_HEADS, ts, NSA_KV_W), lambda b, i: (b, 0, i, 0))
    const = lambda shape: pl.BlockSpec(shape, lambda b, i: (0,) * len(shape))
    return pl.pallas_call(
        _nsa_prep_kernel,
        grid=(B, S // ts),
        in_specs=[pl.BlockSpec((1, ts, NSA_IN_PAD), lambda b, i: (b, i, 0)),
                  pl.BlockSpec((ts, NSA_Q_W), lambda b, i: (i, 0)),
                  pl.BlockSpec((ts, NSA_Q_W), lambda b, i: (i, 0)),
                  const((1, NSA_Q_W)), const((1, NSA_KV_W)), const((1, NSA_KV_W)),
                  const((NSA_Q_W, NSA_Q_W))],
        out_specs=[pl.BlockSpec((1, NSA_HEADS, ts, HEAD_DIM), lambda b, i: (b, 0, i, 0)),
                   kv_spec, kv_spec, kv_spec, aug_spec, kv_spec, aug_spec],
        out_shape=[jax.ShapeDtypeStruct((B, NSA_HEADS, S, HEAD_DIM), BF16),
                   kv_shape, kv_shape, kv_shape, aug_shape, kv_shape, aug_shape],
        compiler_params=_params("parallel", "parallel"),
        name="nsa_prep",
    )(p, cos, sin, gq, gks, gkw, bd)


def _gelu_tanh(x):
    return 0.5 * x * (1.0 + jnp.tanh(0.7978845608028654 * (x + 0.044715 * x * x * x)))


def _compress_kernel(a_ref, posa_ref, posb_ref, w1a_ref, w1b_ref, w2_ref, g_ref, cos_ref, sin_ref,
                     perm_ref, o_ref, *, is_key):
    a = a_ref[0]
    ncp = a.shape[0]
    w1a = w1a_ref[...]
    w1b = w1b_ref[...]
    p1 = _dot(a, w1a)
    p2 = _dot(a, w1b)
    hp = _dot(posa_ref[...], w1a) + _dot(posb_ref[...], w1b)
    h = _gelu_tanh(p1 + pltpu.roll(p2, ncp - 1, 0) + hp[0:1])
    o = _dot(h.astype(BF16), w2_ref[...])
    if is_key:
        ms = jnp.mean(o * o, axis=-1, keepdims=True)
        o = o * lax.rsqrt(ms + RMS_EPS) * g_ref[...]
        o = o * cos_ref[...] + _dot(o, perm_ref[...], precision=HI) * sin_ref[...]
    row = lax.broadcasted_iota(jnp.int32, o.shape, 0)
    o_ref[0] = jnp.where(row < ncp - 1, o, 0.0).astype(BF16)


def _compress(t, pos_emb, w1, w2, g, cos_c, sin_c, is_key):
    BG, ncp, cw = t.shape
    hid = w1.shape[1]
    pos = pos_emb.reshape(1, CMP_LEN * HEAD_DIM)
    posa = jnp.tile(pos[:, :cw], (8, 1)).astype(BF16)
    posb = jnp.tile(pos[:, cw:], (8, 1)).astype(BF16)
    half = HEAD_DIM // 2
    perm = jnp.roll(jnp.eye(HEAD_DIM, dtype=F32), half, axis=0)
    const = lambda shape: pl.BlockSpec(shape, lambda b: (0,) * len(shape))
    return pl.pallas_call(
        functools.partial(_compress_kernel, is_key=is_key),
        grid=(BG,),
        in_specs=[pl.BlockSpec((1, ncp, cw), lambda b: (b, 0, 0)),
                  const((8, cw)), const((8, cw)), const((cw, hid)), const((cw, hid)),
                  const((hid, HEAD_DIM)), const((1, HEAD_DIM)),
                  const((ncp, HEAD_DIM)), const((ncp, HEAD_DIM)), const((HEAD_DIM, HEAD_DIM))],
        out_specs=pl.BlockSpec((1, ncp, HEAD_DIM), lambda b: (b, 0, 0)),
        out_shape=jax.ShapeDtypeStruct((BG, ncp, HEAD_DIM), BF16),
        compiler_params=_params("parallel"),
        name="nsa_compress_k" if is_key else "nsa_compress_v",
    )(t, posa, posb, w1[:cw].astype(BF16), w1[cw:].astype(BF16), w2.astype(BF16),
      g.reshape(1, HEAD_DIM), cos_c, sin_c, perm)


def _exp_cols(s):
    m = jnp.max(s, axis=0, keepdims=True)
    e = jnp.exp2(s - m)
    return e, jnp.sum(e, axis=0, keepdims=True)


def _heads_to_rows(o, tq):
    return jnp.concatenate([o[:, h * tq:(h + 1) * tq] for h in range(NSA_GROUP)], axis=0)


def _nsa_attn_kernel(q_ref, kc_ref, vc_ref, ks_ref, vs_ref, kw_ref, vw_ref, gl_ref, o_ref, selb_ref, s_ref,
                     *, tq, tk, seq):
    g = pl.program_id(1)
    i = pl.program_id(2)
    rows = NSA_GROUP * tq
    ncp = seq // CMP_STRIDE
    ns = seq // SEL_BLOCK
    n_sel = min(N_SEL, ns)
    q = q_ref[0].reshape(rows, HEAD_DIM)
    t0 = i * tq
    t_q = t0 + lax.broadcasted_iota(jnp.int32, (1, tq), 1)

    def per_head(x):
        return jnp.concatenate([x] * NSA_GROUP, axis=1)

    c_end = lax.broadcasted_iota(jnp.int32, (ncp, 1), 0) * CMP_STRIDE + (CMP_LEN - 1)
    bias = jnp.where(c_end <= t_q, 0.0, NEG_INF)
    e, l = _exp_cols(_dot_nt(kc_ref[0], q) + per_head(bias))
    p_c = e * (per_head(jnp.where(t_q >= CMP_LEN - 1, 1.0, 0.0)) / l)
    o_c = _dot_tn(vc_ref[0], p_c.astype(BF16))

    p_sum = p_c[:, 0:tq]
    for h in range(1, NSA_GROUP):
        p_sum = p_sum + p_c[:, h * tq:(h + 1) * tq]
    s0 = lax.broadcasted_iota(jnp.int32, (ns, ncp), 0) * SEL_BLOCK
    c0 = lax.broadcasted_iota(jnp.int32, (ns, ncp), 1) * CMP_STRIDE
    overlap = jnp.where((c0 < s0 + SEL_BLOCK) & (c0 + CMP_LEN > s0), 1.0, 0.0).astype(BF16)
    imp = _dot_split_rhs(overlap, p_sum, 2)
    j = lax.broadcasted_iota(jnp.int32, (ns, tq), 0).astype(F32)
    cur = (t_q // SEL_BLOCK).astype(F32)
    forced = (j == 0.0) | (j == cur) | (j == cur - 1.0)
    score = jnp.where(forced, SEL_FORCE, jnp.where(j <= cur, imp, -1.0))
    for _ in range(n_sel):
        mx = jnp.max(score, axis=0, keepdims=True)
        first = jnp.min(jnp.where(score == mx, j, float(ns)), axis=0, keepdims=True)
        score = jnp.where(j == first, -jnp.inf, score)
    selb_ref[...] = jnp.where(score == -jnp.inf, 0.0, NEG_INF)

    wlen = WINDOW + tq
    w0 = pl.multiple_of(jnp.maximum(t0 - WINDOW, 0), tq)
    kpos = w0 + lax.broadcasted_iota(jnp.int32, (wlen, 1), 0)
    bias = jnp.where((kpos <= t_q) & (kpos > t_q - WINDOW), 0.0, NEG_INF)
    s = _dot_nt(kw_ref[0, 0, pl.ds(w0, wlen), :], q) + per_head(bias)
    e = jnp.exp2(s - jnp.max(s, axis=0, keepdims=True))
    num = _dot_tn(vw_ref[0, 0, pl.ds(w0, wlen), :], e.astype(BF16))[0:ACC_ROWS]
    o_w = num[0:HEAD_DIM] / num[HEAD_DIM:HEAD_DIM + 1]

    key_off = lax.broadcasted_iota(jnp.int32, (SEL_BLOCK, 1), 0)

    last_tile = seq // tk - 1

    def scores(jt, slot):
        jt = jnp.minimum(jt, last_tile)
        k0 = pl.multiple_of(jt * tk, tk)
        bias = []
        for b in range(tk // SEL_BLOCK):
            picked = selb_ref[pl.ds(jt * (tk // SEL_BLOCK) + b, 1), :]
            causal = k0 + b * SEL_BLOCK + key_off <= t_q
            bias.append(jnp.where(causal, picked, NEG_INF))
        bias = jnp.concatenate(bias, axis=0)
        s_ref[slot] = _dot_nt(ks_ref[0, 0, pl.ds(k0, tk), :], q) + per_head(bias)

    def absorb(jt, slot, carry):
        m_run, acc = carry
        k0 = pl.multiple_of(jt * tk, tk)
        s = s_ref[slot]
        m_new = jnp.maximum(m_run, jnp.max(s, axis=0, keepdims=True))
        e = jnp.exp2(s - m_new)
        alpha = jnp.exp2(m_run - m_new)
        acc = alpha * acc + _dot_tn(vs_ref[0, 0, pl.ds(k0, tk), :], e.astype(BF16))[0:ACC_ROWS]
        return m_new, acc

    def sel_pair(it, carry):
        scores(2 * it + 1, 1)
        carry = absorb(2 * it, 0, carry)
        scores(2 * it + 2, 0)
        return absorb(2 * it + 1, 1, carry)

    n_tiles = (t0 + tq - 1) // tk + 1
    init = (jnp.full((1, rows), NEG_INF, F32), jnp.zeros((ACC_ROWS, rows), F32))
    scores(0, 0)
    _, acc_s = lax.fori_loop(0, (n_tiles + 1) // 2, sel_pair, init)
    o_s = acc_s[0:HEAD_DIM] / acc_s[HEAD_DIM:HEAD_DIM + 1]

    gw = NSA_GROUP * HEAD_DIM
    n = lax.broadcasted_iota(jnp.int32, (3 * gw, LANE), 0)
    col = lax.broadcasted_iota(jnp.int32, (3 * gw, LANE), 1)
    head = g * NSA_GROUP + (n % gw) // HEAD_DIM
    spread = jnp.where(col == head * 3 + n // gw, 1.0, 0.0).astype(BF16)
    gates = None
    for piece in _split_bf16(_sigmoid(gl_ref[0]), 3):
        part = _dot_nt(spread, piece)
        gates = part if gates is None else gates + part
    out = gates[0:gw] * _heads_to_rows(o_c, tq)
    out = out + gates[gw:2 * gw] * _heads_to_rows(o_s, tq)
    out = out + gates[2 * gw:] * _heads_to_rows(o_w, tq)
    o_ref[0] = out.T.astype(BF16)


def _nsa_attn(q, kc, vc, ks, vs, kw, vw, p_nsa, tq, tk):
    B, H, S, _ = q.shape
    G = NSA_KV_HEADS
    ncp = S // CMP_STRIDE
    assert S % (2 * tk) == 0
    cmp_spec = pl.BlockSpec((1, ncp, HEAD_DIM), lambda b, g, i: (b * G + g, 0, 0))
    seq_spec = pl.BlockSpec((1, 1, S, HEAD_DIM), lambda b, g, i: (b, g, 0, 0))
    aug_spec = pl.BlockSpec((1, 1, S, NSA_KV_W), lambda b, g, i: (b, g, 0, 0))
    return pl.pallas_call(
        functools.partial(_nsa_attn_kernel, tq=tq, tk=tk, seq=S),
        grid=(B, G, S // tq),
        in_specs=[pl.BlockSpec((1, NSA_GROUP, tq, HEAD_DIM), lambda b, g, i: (b, g, i, 0)),
                  cmp_spec, cmp_spec, seq_spec, aug_spec, seq_spec, aug_spec,
                  pl.BlockSpec((1, tq, LANE), lambda b, g, i: (b, i, NSA_GATE_BLK))],
        out_specs=pl.BlockSpec((1, tq, NSA_GROUP * HEAD_DIM), lambda b, g, i: (b, i, g)),
        out_shape=jax.ShapeDtypeStruct((B, S, NSA_Q_W), BF16),
        scratch_shapes=[pltpu.VMEM((S // SEL_BLOCK, tq), F32),
                        pltpu.VMEM((2, tk, NSA_GROUP * tq), F32)],
        compiler_params=_params("parallel", "parallel", "arbitrary"),
        name="nsa_attn",
    )(q, kc, vc, ks, vs, kw, vw, p_nsa)


def _rope_tables(S):
    half = HEAD_DIM // 2
    inv = ROPE_THETA ** (-jnp.arange(half, dtype=F32) / half)
    ang = jnp.arange(S, dtype=F32)[:, None] * inv[None, :]
    cos, sin = jnp.cos(ang), jnp.sin(ang)
    cos64 = jnp.concatenate([cos, cos], axis=1)
    sin64 = jnp.concatenate([-sin, sin], axis=1)
    return cos64, sin64


def _nsa_branch(p_nsa, qk_g, cmp_pos, cmp_w1, cmp_w2, B, S):
    cos64, sin64 = _rope_tables(S)
    cos = jnp.tile(cos64, (1, NSA_HEADS))
    sin = jnp.tile(sin64, (1, NSA_HEADS))
    q, kc, vc, ks, vs, kw, vw = _nsa_prep(p_nsa, cos, sin, qk_g, ts=min(512, S))
    ncp = S // CMP_STRIDE
    cmp_end = jnp.minimum(jnp.arange(ncp) * CMP_STRIDE + CMP_LEN - 1, S - 1)
    cw = CMP_STRIDE * HEAD_DIM
    kc = kc.reshape(B * NSA_KV_HEADS, ncp, cw)
    vc = vc.reshape(B * NSA_KV_HEADS, ncp, cw)
    k_cmp = _compress(kc, cmp_pos[0], cmp_w1[0], cmp_w2[0], qk_g[1], cos64[cmp_end], sin64[cmp_end], True)
    v_cmp = _compress(vc, cmp_pos[1], cmp_w1[1], cmp_w2[1], qk_g[1], cos64[cmp_end], sin64[cmp_end], False)
    return _nsa_attn(q, k_cmp, v_cmp, ks, vs, kw, vw, p_nsa, tq=256, tk=256)


CONV_HALO = 32


def _conv_kernel(cur_ref, halo_ref, w_ref, vec_ref, o_ref, buf, y_scr, shifted, *, ts):
    i = pl.program_id(1)

    def glu(p):
        return p[:, :CONV_CH] * _sigmoid(p[:, CONV_CH:])

    buf[0:CONV_HALO, :] = jnp.where(i > 0, glu(halo_ref[0]), 0.0)
    buf[CONV_HALO:, :] = glu(cur_ref[0])
    rt = min(ts, 128)
    shift = CONV_HALO - (CONV_K - 1)
    for c in range(CONV_CH // LANE):
        cs = slice(c * LANE, (c + 1) * LANE)
        for b in range(8):
            span = ts + 8 * ((CONV_K - 1 - b) // 8)
            shifted[b, 0:span, :] = buf[shift + b:shift + b + span, cs]
        for r in range(ts // rt):
            acc = jnp.zeros((rt, LANE), F32)
            for j in range(CONV_K):
                acc = acc + w_ref[j:j + 1, cs] * shifted[j % 8, r * rt + 8 * (j // 8):r * rt + 8 * (j // 8) + rt, :]
            y_scr[r * rt:(r + 1) * rt, cs] = acc
    y = y_scr[...] + vec_ref[0:1, :]
    mu = jnp.mean(y, axis=-1, keepdims=True)
    yc = y - mu
    var = jnp.mean(yc * yc, axis=-1, keepdims=True)
    y = yc * lax.rsqrt(var + LN_EPS) * vec_ref[1:2, :] + vec_ref[2:3, :]
    o_ref[0] = (y * _sigmoid(y)).astype(BF16)


def _conformer(p, conv_w, conv_vec, ts):
    B, S, _ = p.shape
    w = jnp.pad(conv_w, ((0, 32 - CONV_K), (0, 0)))
    vec = jnp.pad(conv_vec, ((0, 5), (0, 0)))
    hb = ts // CONV_HALO
    return pl.pallas_call(
        functools.partial(_conv_kernel, ts=ts),
        grid=(B, S // ts),
        in_specs=[pl.BlockSpec((1, ts, 2 * CONV_CH), lambda b, i: (b, i, 0)),
                  pl.BlockSpec((1, CONV_HALO, 2 * CONV_CH), lambda b, i: (b, jnp.maximum(i * hb - 1, 0), 0)),
                  pl.BlockSpec((32, CONV_CH), lambda b, i: (0, 0)),
                  pl.BlockSpec((8, CONV_CH), lambda b, i: (0, 0))],
        out_specs=pl.BlockSpec((1, ts, CONV_CH), lambda b, i: (b, i, 0)),
        out_shape=jax.ShapeDtypeStruct((B, S, CONV_CH), BF16),
        scratch_shapes=[pltpu.VMEM((CONV_HALO + ts, CONV_CH), F32), pltpu.VMEM((ts, CONV_CH), F32),
                        pltpu.VMEM((8, ts + 8 * ((CONV_K - 1) // 8), LANE), F32)],
        compiler_params=_params("parallel", "parallel"),
        name="conformer",
    )(p, p, w, vec)


RW_LORA_OFF = 3 * RWKV_W
RW_GATE_OFF = RW_LORA_OFF + DECAY_LORA + ICLR_LORA
RW_VRES_OFF = RWKV_IN_PAD - LANE


def _rwkv_prep_kernel(*refs, ts, has_vres):
    if has_vres:
        (cur_ref, halo_ref, mu_ref, vec_ref, wup_ref, aup_ref, gup_ref, bd_ref, vf_ref, vb_ref, vup_ref,
         r_ref, lw_ref, k_ref, v_ref, kk_ref, kka_ref, g_ref) = refs
    else:
        (cur_ref, halo_ref, mu_ref, vec_ref, wup_ref, aup_ref, gup_ref, bd_ref,
         r_ref, lw_ref, k_ref, v_ref, kk_ref, kka_ref, g_ref) = refs
    i = pl.program_id(1)
    p = cur_ref[0]
    last = jnp.where(i > 0, halo_ref[0, 7:8, :], 0.0)
    row = lax.broadcasted_iota(jnp.int32, p.shape, 0)
    prev = jnp.where(row == 0, last, pltpu.roll(p, 1, 0))
    u = p + (prev - p) * mu_ref[...]
    C = RWKV_W
    r, k, v = u[:, :C], u[:, C:2 * C], u[:, 2 * C:3 * C]
    lora = u[:, RW_LORA_OFF:RW_GATE_OFF]
    w_in = vec_ref[0:1, :] + _dot(jnp.tanh(lora).astype(BF16), wup_ref[...])
    sp = jnp.maximum(-w_in, 0.0) + jnp.log(1.0 + jnp.exp(-jnp.abs(w_in)))
    lw = -jnp.exp(-sp - 0.5)
    a = _sigmoid(vec_ref[1:2, :] + _dot(lora.astype(BF16), aup_ref[...]))
    g = _dot(_sigmoid(u[:, RW_GATE_OFF:]).astype(BF16), gup_ref[...])
    if has_vres:
        mix = _sigmoid(vb_ref[...] + _dot(u[:, RW_VRES_OFF:].astype(BF16), vup_ref[...]))
        v = v + (vf_ref[0] - v) * mix
    kk = k * vec_ref[2:3, :]
    ss = _dot_split_lhs(kk * kk, bd_ref[...], 3)
    kk = kk / jnp.maximum(jnp.sqrt(ss), 1e-12)
    k = k * (1.0 + (a - 1.0) * vec_ref[3:4, :])
    for val, ref in ((r, r_ref), (lw, lw_ref), (k, k_ref), (v, v_ref), (kk, kk_ref), (kk * a, kka_ref),
                     (g, g_ref)):
        ref[0] = val


def _pad_rows(w, n):
    return jnp.pad(w, ((0, n - w.shape[0]), (0, 0)))


def _rwkv_prep(p, mu, vec, w_up, a_up, g_up, v_first, vres_up, vres_b, ts):
    B, S, _ = p.shape
    C = RWKV_W
    has_vres = v_first is not None
    mu_p = jnp.pad(mu, (0, RWKV_IN_PAD - mu.shape[0])).reshape(1, RWKV_IN_PAD)
    n_lora = DECAY_LORA + ICLR_LORA
    wup = _pad_rows(w_up, n_lora).astype(BF16)
    aup = jnp.pad(a_up, ((DECAY_LORA, 0), (0, 0))).astype(BF16)
    gup = _pad_rows(g_up, RWKV_IN_PAD - RW_GATE_OFF).astype(BF16)
    bd = jnp.kron(jnp.eye(RWKV_HEADS, dtype=F32), jnp.ones((HEAD_DIM, HEAD_DIM), F32)).astype(BF16)
    const = lambda shape: pl.BlockSpec(shape, lambda b, i: (0,) * len(shape))
    dense = pl.BlockSpec((1, ts, C), lambda b, i: (b, i, 0))
    in_specs = [pl.BlockSpec((1, ts, RWKV_IN_PAD), lambda b, i: (b, i, 0)),
                pl.BlockSpec((1, 8, RWKV_IN_PAD), lambda b, i: (b, jnp.maximum(i * (ts // 8) - 1, 0), 0)),
                const((1, RWKV_IN_PAD)), const((8, C)), const((n_lora, C)), const((n_lora, C)),
                const((RWKV_IN_PAD - RW_GATE_OFF, C)), const((C, C))]
    args = [p, p, mu_p, jnp.pad(vec, ((0, 1), (0, 0))), wup, aup, gup, bd]
    if has_vres:
        off = RWKV_IN_PAD - LANE
        lo = 3 * C + n_lora + GATE_LORA - off
        vup = jnp.pad(vres_up, ((lo, LANE - lo - VRES_LORA), (0, 0))).astype(BF16)
        in_specs += [dense, const((1, C)), const((LANE, C))]
        args += [v_first, vres_b.reshape(1, C), vup]
    return pl.pallas_call(
        functools.partial(_rwkv_prep_kernel, ts=ts, has_vres=has_vres),
        grid=(B, S // ts),
        in_specs=in_specs,
        out_specs=[dense] * 7,
        out_shape=[jax.ShapeDtypeStruct((B, S, C), F32)] * 7,
        compiler_params=_params("parallel", "parallel"),
        name="rwkv_prep",
    )(*args)


GROUP_HEADS = 4
GW = GROUP_HEADS * HEAD_DIM
SR = GROUP_HEADS * CHUNK
SEQ_PARTS = 2
PREP_CHUNKS = 4


def _split_bf16(x, n):
    parts = []
    for _ in range(n - 1):
        h = x.astype(BF16)
        parts.append(h)
        x = x - h.astype(F32)
    parts.append(x.astype(BF16))
    return parts


def _dot_split_rhs(a, x, n):
    parts = _split_bf16(x, n)
    out = _dot(a, parts[0])
    for p in parts[1:]:
        out = out + _dot(a, p)
    return out


def _dot_split_lhs(x, b, n):
    parts = _split_bf16(x, n)
    out = _dot(parts[0], b)
    for p in parts[1:]:
        out = out + _dot(p, b)
    return out


def _unit_lower_inverses(mats, eye, in16, in32):
    b = lambda m: m.astype(BF16)
    ds = [b(jnp.where(in16, a, 0.0)) for a in mats]
    xs = [eye - d for d in ds]
    ps = [b(_dot(d, d)) for d in ds]
    for step in range(3):
        xs = [x + _dot(b(x), p) for x, p in zip(xs, ps)]
        if step < 2:
            ps = [b(_dot(p, p)) for p in ps]
    for lowers in ([jnp.where(in32 & jnp.logical_not(in16), a, 0.0) for a in mats],
                   [jnp.where(in32, 0.0, a) for a in mats]):
        xbs = [b(x) for x in xs]
        mids = [b(_dot(xb, b(lo))) for xb, lo in zip(xbs, lowers)]
        xs = [x - _dot(mid, xb) for x, mid, xb in zip(xs, mids, xbs)]
    return xs


def _rwkv_scan_kernel(r_ref, lw_ref, k_ref, v_ref, kk_ref, kka_ref, g_ref, gn_ref, o_ref, state,
                      kap_s, bt_s, rt_s, arb_s, tinv_s, u_s, y0_s, sv_s, gl_s, y_s, *, tt):
    @pl.when(pl.program_id(1) == 0)
    def _():
        state[...] = jnp.zeros_like(state)

    ri = lax.broadcasted_iota(jnp.int32, (SR, GW), 0)
    ci = lax.broadcasted_iota(jnp.int32, (SR, GW), 1)
    same_head = (ri // CHUNK) == (ci // HEAD_DIM)
    strict = (ci % CHUNK) < (ri % CHUNK)
    incl = (ci % CHUNK) <= (ri % CHUNK)
    in16 = (ri // 16) == (ci // 16)
    in32 = (ri // 32) == (ci // 32)
    eye = jnp.where(ri == ci, 1.0, 0.0)
    head_ones = jnp.where(same_head, 1.0, 0.0).astype(BF16)
    tr = lax.broadcasted_iota(jnp.int32, (CHUNK, CHUNK), 0)
    tc = lax.broadcasted_iota(jnp.int32, (CHUNK, CHUNK), 1)
    tri = jnp.where(tc <= tr, 1.0, 0.0).astype(BF16)
    b = lambda m: m.astype(BF16)

    def stack(x):
        return b(jnp.where(same_head, jnp.concatenate([x] * GROUP_HEADS, axis=0), 0.0))

    n_groups = RWKV_HEADS // GROUP_HEADS

    def chunk_rows(c):
        return pl.ds(pl.multiple_of(c * CHUNK, CHUNK), CHUNK)

    def prepare_pair(it, carry):
        items = [(PREP_CHUNKS * it + u, gi) for u in range(PREP_CHUNKS) for gi in range(n_groups)]
        n = range(len(items))
        kap, bt, kt, rt, vs, gl = [], [], [], [], [], []
        for c, gi in items:
            rows = chunk_rows(c)
            lanes = slice(gi * GW, (gi + 1) * GW)
            lw = lw_ref[0, rows, lanes]
            cs = _dot_split_rhs(tri, lw, 3)
            g_in = jnp.exp(cs)
            g_inv = jnp.exp(-cs)
            kap.append(stack(kk_ref[0, rows, lanes] * jnp.exp(cs - lw)))
            bt.append(stack(kka_ref[0, rows, lanes] * g_inv))
            kt.append(stack(k_ref[0, rows, lanes] * g_inv))
            rt.append(stack(r_ref[0, rows, lanes] * g_in))
            vs.append(stack(v_ref[0, rows, lanes]))
            gl.append(jnp.broadcast_to(g_in[CHUNK - 1:CHUNK, :], (8, GW)))
        a_kb = [jnp.where(strict, _dot_nt(kap[i], bt[i]), 0.0) for i in n]
        a_kk = [b(jnp.where(strict, _dot_nt(kap[i], kt[i]), 0.0)) for i in n]
        a_rk = [b(jnp.where(incl, _dot_nt(rt[i], kt[i]), 0.0)) for i in n]
        a_rb = [b(jnp.where(incl, _dot_nt(rt[i], bt[i]), 0.0)) for i in n]
        u = [_dot(a_kk[i], vs[i]) for i in n]
        y0 = [_dot(a_rk[i], vs[i]) for i in n]
        sv = [_dot_tn(vs[i], kt[i]) for i in n]
        t_inv = _unit_lower_inverses(a_kb, eye, in16, in32)
        for i, (c, gi) in enumerate(items):
            slot = c * n_groups + gi
            kap_s[slot] = kap[i]
            bt_s[slot] = bt[i]
            rt_s[slot] = rt[i]
            arb_s[slot] = a_rb[i]
            tinv_s[slot] = b(t_inv[i])
            u_s[slot] = u[i]
            y0_s[slot] = y0[i]
            sv_s[slot] = sv[i]
            gl_s[slot] = gl[i]
        return carry

    lax.fori_loop(0, tt // (PREP_CHUNKS * CHUNK), prepare_pair, 0)

    def advance(c, carry):
        half = SR // SEQ_PARTS
        items = [(gi, slice(hh * half, (hh + 1) * half)) for gi in range(n_groups) for hh in range(SEQ_PARTS)]
        n = range(len(items))
        blk = lambda ref, i: ref[c * n_groups + items[i][0], items[i][1], items[i][1]]
        s0 = [state[gi, d, d] for gi, d in items]
        s0b = [b(s) for s in s0]
        rhs = [b(_dot_nt(blk(kap_s, i), s0b[i]) + blk(u_s, i)) for i in n]
        pb = [b(_dot(blk(tinv_s, i), rhs[i])) for i in n]
        for i, (gi, d) in enumerate(items):
            decay = gl_s[c * n_groups + gi, 0:1, d]
            state[gi, d, d] = (s0[i] + blk(sv_s, i) - _dot_tn(pb[i], blk(bt_s, i))) * decay
        for i, (gi, d) in enumerate(items):
            y = _dot_nt(blk(rt_s, i), s0b[i]) + blk(y0_s, i) - _dot(blk(arb_s, i), pb[i])
            yd = y[0:CHUNK]
            for h in range(1, half // CHUNK):
                yd = yd + y[h * CHUNK:(h + 1) * CHUNK]
            y_s[chunk_rows(c), gi * GW + d.start:gi * GW + d.stop] = yd
        return carry

    lax.fori_loop(0, tt // CHUNK, advance, 0)

    def finish_pair(it, carry):
        items = [(chunk_rows(2 * it + u), slice(gi * GW, (gi + 1) * GW)) for u in range(2) for gi in range(n_groups)]
        yd = [y_s[rows, lanes] for rows, lanes in items]
        ym = [_dot_split_lhs(y, head_ones, 2) * (1.0 / HEAD_DIM) for y in yd]
        yc = [y - m for y, m in zip(yd, ym)]
        yv = [_dot_split_lhs(c * c, head_ones, 2) * (1.0 / HEAD_DIM) for c in yc]
        rk = [r_ref[0, rows, lanes] * k_ref[0, rows, lanes] * gn_ref[2:3, lanes] for rows, lanes in items]
        bonus = [_dot_split_lhs(x, head_ones, 2) for x in rk]
        for i, (rows, lanes) in enumerate(items):
            yn = yc[i] * lax.rsqrt(yv[i] + GN_EPS) * gn_ref[0:1, lanes] + gn_ref[1:2, lanes]
            o_ref[0, rows, lanes] = b((yn + bonus[i] * v_ref[0, rows, lanes]) * g_ref[0, rows, lanes])
        return carry

    lax.fori_loop(0, tt // (2 * CHUNK), finish_pair, 0)


def _rwkv_scan(r, lw, k, v, kk, kka, g, vec, tt):
    B, S, C = r.shape
    gn = jnp.pad(jnp.stack([vec[5], vec[6], vec[4]]), ((0, 5), (0, 0)))
    dense = pl.BlockSpec((1, tt, C), lambda b, i: (b, i, 0))
    slots = (tt // CHUNK) * (C // GW)
    assert tt % (PREP_CHUNKS * CHUNK) == 0
    stacked = lambda dtype: pltpu.VMEM((slots, SR, GW), dtype)
    return pl.pallas_call(
        functools.partial(_rwkv_scan_kernel, tt=tt),
        grid=(B, S // tt),
        in_specs=[dense] * 7 + [pl.BlockSpec((8, C), lambda b, i: (0, 0))],
        out_specs=dense,
        out_shape=jax.ShapeDtypeStruct((B, S, C), BF16),
        scratch_shapes=[pltpu.VMEM((C // GW, GW, GW), F32)] + [stacked(BF16)] * 5 + [stacked(F32)] * 3
                       + [pltpu.VMEM((slots, 8, GW), F32), pltpu.VMEM((tt, C), F32)],
        compiler_params=_params("parallel", "arbitrary"),
        name="rwkv_scan",
    )(r, lw, k, v, kk, kka, g, gn)


def _merge_kernel(x_ref, an_ref, ac_ref, ar_ref, gl_ref, wn_ref, wc_ref, wr_ref, wo_ref, g2_ref,
                  xo_ref, h_ref):
    D = x_ref.shape[1]
    o_n = _dot(an_ref[...], wn_ref[...])
    mixed = _sigmoid(gl_ref[:, 0:D]) * o_n
    o_c = _dot(ac_ref[...], wc_ref[...])
    mixed = mixed + _sigmoid(gl_ref[:, D:2 * D]) * o_c
    o_r = _dot(ar_ref[...], wr_ref[...])
    mixed = mixed + _sigmoid(gl_ref[:, 2 * D:3 * D]) * o_r
    x = x_ref[...] + _dot(mixed.astype(BF16), wo_ref[...])
    xo_ref[...] = x
    ms = jnp.mean(x * x, axis=-1, keepdims=True)
    h_ref[...] = (x * lax.rsqrt(ms + RMS_EPS) * g2_ref[...]).astype(BF16)


def _merge(x, a_nsa, a_conv, a_rwkv, gl, w_n, w_c, w_r, w_o, g2, tm):
    T, D = x.shape
    row = lambda w: pl.BlockSpec((tm, w), lambda i: (i, 0))
    const = lambda shape: pl.BlockSpec(shape, lambda i: (0,) * len(shape))
    return pl.pallas_call(
        _merge_kernel,
        grid=(T // tm,),
        in_specs=[row(D), row(a_nsa.shape[1]), row(a_conv.shape[1]), row(a_rwkv.shape[1]), row(3 * D),
                  const(w_n.shape), const(w_c.shape), const(w_r.shape), const(w_o.shape), const((1, D))],
        out_specs=[row(D), row(D)],
        out_shape=[jax.ShapeDtypeStruct((T, D), F32), jax.ShapeDtypeStruct((T, D), BF16)],
        compiler_params=_params("parallel"),
        name="merge",
    )(x, a_nsa, a_conv, a_rwkv, gl, w_n.astype(BF16), w_c.astype(BF16), w_r.astype(BF16),
      w_o.astype(BF16), g2.reshape(1, D))


def _cast_kernel(w_ref, o_ref):
    o_ref[...] = w_ref[...].astype(BF16)


def _interleave_gate_up(w_gu, tf):
    E, d, two_ff = w_gu.shape
    nj = two_ff // (2 * tf)
    return pl.pallas_call(
        _cast_kernel,
        grid=(E, nj, 2),
        in_specs=[pl.BlockSpec((1, d, tf), lambda e, j, half: (e, 0, half * nj + j))],
        out_specs=pl.BlockSpec((1, d, tf), lambda e, j, half: (e, 0, 2 * j + half)),
        out_shape=jax.ShapeDtypeStruct((E, d, two_ff), BF16),
        compiler_params=_params("parallel", "parallel", "parallel"),
        name="ffn_weight_layout",
    )(w_gu)


def _ffn_kernel(x_ref, h_ref, wgu_ref, wd_ref, o_ref):
    j = pl.program_id(1)
    h = h_ref[...]
    gu = _dot(h, wgu_ref[...])
    tf = gu.shape[1] // 2
    gate, up = gu[:, :tf], gu[:, tf:]
    act = (gate * _sigmoid(gate) * up).astype(BF16)
    y = _dot(act, wd_ref[...])

    @pl.when(j == 0)
    def _():
        o_ref[...] = x_ref[...] + y

    @pl.when(j > 0)
    def _():
        o_ref[...] += y


def _ffn(x, h, w_gu, w_down, tm, tf):
    T, D = x.shape
    FF = w_down.shape[0]
    wgu = _interleave_gate_up(w_gu[None], tf)[0]
    return pl.pallas_call(
        _ffn_kernel,
        grid=(T // tm, FF // tf),
        in_specs=[pl.BlockSpec((tm, D), lambda i, j: (i, 0)),
                  pl.BlockSpec((tm, D), lambda i, j: (i, 0)),
                  pl.BlockSpec((D, 2 * tf), lambda i, j: (0, j)),
                  pl.BlockSpec((tf, D), lambda i, j: (j, 0))],
        out_specs=pl.BlockSpec((tm, D), lambda i, j: (i, 0)),
        out_shape=jax.ShapeDtypeStruct((T, D), F32),
        compiler_params=_params("parallel", "arbitrary"),
        name="ffn_dense",
    )(x, h, wgu, w_down.astype(BF16))


MOE_ROWS = 256
MOE_TAIL = 128
RANK_BLK = 256


def _router_kernel(h_ref, w_ref, b_ref, gw_ref, pos_ref, cnt_ref):
    logits = _dot(h_ref[...], w_ref[...]) + b_ref[...]
    tt = logits.shape[0]
    lane = lax.broadcasted_iota(jnp.int32, logits.shape, 1)
    m1 = jnp.max(logits, axis=-1, keepdims=True)
    i1 = jnp.min(jnp.where(logits == m1, lane, LANE), axis=-1, keepdims=True)
    rest = jnp.where(lane == i1, -jnp.inf, logits)
    m2 = jnp.max(rest, axis=-1, keepdims=True)
    i2 = jnp.min(jnp.where(rest == m2, lane, LANE), axis=-1, keepdims=True)
    e2 = jnp.exp(m2 - m1)
    w1 = 1.0 / (1.0 + e2)
    w2 = e2 / (1.0 + e2)
    gw_ref[...] = jnp.where(lane == i1, w1, jnp.where(lane == i2, w2, 0.0))
    routed = (lane == i1) | (lane == i2)
    r = lax.broadcasted_iota(jnp.int32, (RANK_BLK, RANK_BLK), 0)
    c = lax.broadcasted_iota(jnp.int32, (RANK_BLK, RANK_BLK), 1)
    before = jnp.where(c < r, 1.0, 0.0).astype(BF16)
    run = jnp.zeros((1, LANE), F32)
    ranks = []
    for blk in range(tt // RANK_BLK):
        hit = jnp.where(routed[blk * RANK_BLK:(blk + 1) * RANK_BLK], 1.0, 0.0)
        ranks.append(_dot(before, hit.astype(BF16)) + run)
        run = run + jnp.sum(hit, axis=0, keepdims=True)
    pos = jnp.where(routed, jnp.concatenate(ranks, axis=0), -1.0)
    pos_ref[0] = pos.T[0:N_EXPERTS]
    cnt_ref[0] = jnp.broadcast_to(run, (8, LANE))


def _router(h, router_w, router_b, tt):
    T, D = h.shape
    nt = T // tt
    w = _pad_cols(router_w, LANE).astype(BF16)
    b = jnp.pad(router_b, (0, LANE - N_EXPERTS), constant_values=-jnp.inf).reshape(1, LANE)
    return pl.pallas_call(
        _router_kernel,
        grid=(nt,),
        in_specs=[pl.BlockSpec((tt, D), lambda i: (i, 0)),
                  pl.BlockSpec((D, LANE), lambda i: (0, 0)),
                  pl.BlockSpec((1, LANE), lambda i: (0, 0))],
        out_specs=[pl.BlockSpec((tt, LANE), lambda i: (i, 0)),
                   pl.BlockSpec((1, N_EXPERTS, tt), lambda i: (i, 0, 0)),
                   pl.BlockSpec((1, 8, LANE), lambda i: (i, 0, 0))],
        out_shape=[jax.ShapeDtypeStruct((T, LANE), F32),
                   jax.ShapeDtypeStruct((nt, N_EXPERTS, tt), F32),
                   jax.ShapeDtypeStruct((nt, 8, LANE), F32)],
        compiler_params=_params("parallel"),
        name="moe_router",
    )(h, w, b)


def _moe_kernel(nsb_ref, x_ref, h_ref, gw_ref, pos_ref, wgu_ref, wd_ref, o_ref, xc, yc):
    i = pl.program_id(0)
    e = pl.program_id(1)
    j = pl.program_id(2)
    n_full = nsb_ref[0, i * pl.num_programs(1) + e]
    has_tail = nsb_ref[1, i * pl.num_programs(1) + e]

    @pl.when((e == 0) & (j == 0))
    def _():
        o_ref[...] = x_ref[...]

    pos = pos_ref[0, pl.ds(e, 1), :]

    def for_blocks(fn):
        def body(sb, carry):
            fn(pl.multiple_of(sb * MOE_ROWS, MOE_ROWS), MOE_ROWS)
            return carry
        lax.fori_loop(0, n_full, body, 0)

        @pl.when(has_tail == 1)
        def _():
            fn(pl.multiple_of(n_full * MOE_ROWS, MOE_TAIL), MOE_TAIL)

    def select(r0, n):
        slot = lax.broadcasted_iota(jnp.int32, (n, 1), 0) + r0
        return jnp.where(pos == slot.astype(F32), 1.0, 0.0).astype(BF16)

    def gather(r0, n):
        xc[pl.ds(r0, n), :] = _dot(select(r0, n), h_ref[...]).astype(BF16)
        yc[pl.ds(r0, n), :] = jnp.zeros((n, yc.shape[1]), F32)

    def expert(r0, n):
        gu = _dot(xc[pl.ds(r0, n), :], wgu_ref[0])
        tf = gu.shape[1] // 2
        gate, up = gu[:, :tf], gu[:, tf:]
        act = (gate * _sigmoid(gate) * up).astype(BF16)
        yc[pl.ds(r0, n), :] += _dot(act, wd_ref[0])

    @pl.when(j == 0)
    def _():
        for_blocks(gather)

    for_blocks(expert)

    @pl.when(j == pl.num_programs(2) - 1)
    def _():
        gw = gw_ref[...]
        lane = lax.broadcasted_iota(jnp.int32, gw.shape, 1)
        w_e = jnp.sum(jnp.where(lane == e, gw, 0.0), axis=-1, keepdims=True)

        def scatter(r0, n):
            o_ref[...] += w_e * _dot_tn(select(r0, n), yc[pl.ds(r0, n), :].astype(BF16))
        for_blocks(scatter)


def _moe(x, h, gw, pos, counts, w_gu, w_down, tt, tf):
    T, D = x.shape
    E, FF, _ = w_down.shape
    wgu = _interleave_gate_up(w_gu, tf)
    n_tail = ((counts[:, 0, :E].astype(jnp.int32) + MOE_TAIL - 1) // MOE_TAIL).reshape(-1)
    per_full = MOE_ROWS // MOE_TAIL
    n_sb = jnp.stack([n_tail // per_full, n_tail % per_full])
    once = dict(pipeline_mode=pl.Buffered(1))
    return pl.pallas_call(
        _moe_kernel,
        grid_spec=pltpu.PrefetchScalarGridSpec(
            num_scalar_prefetch=1,
            grid=(T // tt, E, FF // tf),
            in_specs=[pl.BlockSpec((tt, D), lambda i, e, j, n: (i, 0), **once),
                      pl.BlockSpec((tt, D), lambda i, e, j, n: (i, 0), **once),
                      pl.BlockSpec((tt, LANE), lambda i, e, j, n: (i, 0)),
                      pl.BlockSpec((1, E, tt), lambda i, e, j, n: (i, 0, 0)),
                      pl.BlockSpec((1, D, 2 * tf), lambda i, e, j, n: (e, 0, j)),
                      pl.BlockSpec((1, tf, D), lambda i, e, j, n: (e, j, 0))],
            out_specs=pl.BlockSpec((tt, D), lambda i, e, j, n: (i, 0), **once),
            scratch_shapes=[pltpu.VMEM((tt, D), BF16), pltpu.VMEM((tt, D), F32)]),
        out_shape=jax.ShapeDtypeStruct((T, D), F32),
        compiler_params=_params("parallel", "arbitrary", "arbitrary"),
        name="moe_experts",
    )(n_sb, x, h, gw, pos, wgu, w_down.astype(BF16))


def _pad_cols(w, n):
    return jnp.pad(w, ((0, 0), (0, n - w.shape[1])))


def kernel(x, mix_norm_g, w_in_first, w_in_rest, rwkv_mu_first, rwkv_mu_rest, nsa_qk_g, nsa_cmp_pos, nsa_cmp_w1, nsa_cmp_w2, nsa_w_o, conv_w, conv_vec, conv_w_o, rwkv_vec, rwkv_w_up, rwkv_a_up, rwkv_g_up, rwkv_vres_up, rwkv_vres_b, rwkv_w_o, w_out, ffn_norm_g, dense_w_gu, dense_w_down, router_w, router_b, moe_w_gu, moe_w_down):
    B, S, D = x.shape
    T = B * S
    depth = mix_norm_g.shape[0]
    off_conv = NSA_IN
    off_gate = off_conv + 2 * CONV_CH
    off_rwkv = off_gate + 3 * D
    tm = min(512, T)
    xt = x.reshape(T, D)
    v_first = None
    for layer in range(depth):
        if layer == 0:
            w_in, mu, vres_up, vres_b = w_in_first, rwkv_mu_first, None, None
        else:
            w_in, mu = w_in_rest[layer - 1], rwkv_mu_rest[layer - 1]
            vres_up, vres_b = rwkv_vres_up[layer - 1], rwkv_vres_b[layer - 1]
        g1 = mix_norm_g[layer]
        w_nsa = _pad_cols(w_in[:, :off_conv], NSA_IN_PAD).astype(BF16)
        w_conv = w_in[:, off_conv:off_gate].astype(BF16)
        w_gate = w_in[:, off_gate:off_rwkv].astype(BF16)
        w_rwkv = _pad_cols(w_in[:, off_rwkv:], RWKV_IN_PAD).astype(BF16)
        p_nsa, p_conv, p_gate, p_rwkv = _rms_proj(xt, g1, [w_nsa, w_conv, w_gate, w_rwkv], min(256, T))
        p_nsa = p_nsa.reshape(B, S, NSA_IN_PAD)
        p_conv = p_conv.reshape(B, S, 2 * CONV_CH)
        p_rwkv = p_rwkv.reshape(B, S, RWKV_IN_PAD)

        a_nsa = _nsa_branch(p_nsa, nsa_qk_g[layer], nsa_cmp_pos[layer], nsa_cmp_w1[layer],
                            nsa_cmp_w2[layer], B, S)
        a_conv = _conformer(p_conv, conv_w[layer], conv_vec[layer], ts=min(512, S))
        r, lw, k, v, kk, kka, g = _rwkv_prep(
            p_rwkv, mu, rwkv_vec[layer], rwkv_w_up[layer], rwkv_a_up[layer], rwkv_g_up[layer],
            v_first, vres_up, vres_b, ts=min(512, S))
        if layer == 0:
            v_first = v
        a_rwkv = _rwkv_scan(r, lw, k, v, kk, kka, g, rwkv_vec[layer], tt=min(512, S))

        xt, h2 = _merge(xt, a_nsa.reshape(T, NSA_Q_W), a_conv.reshape(T, CONV_CH),
                        a_rwkv.reshape(T, RWKV_W), p_gate, nsa_w_o[layer], conv_w_o[layer],
                        rwkv_w_o[layer], w_out[layer], ffn_norm_g[layer], tm)
        if layer % 2 == 0:
            xt = _ffn(xt, h2, dense_w_gu[layer // 2], dense_w_down[layer // 2], min(512, T), 1408)
        else:
            tt = min(2048, T)
            gw, pos, counts = _router(h2, router_w[layer // 2], router_b[layer // 2], tt)
            xt = _moe(xt, h2, gw, pos, counts, moe_w_gu[layer // 2], moe_w_down[layer // 2], tt, 896)
    return xt.reshape(B, S, D)
```
